```python
import math
import jax, jax.numpy as jnp
from jax import lax
import numpy as np

D_MODEL = 1024
BATCH = 8
SEQ = 2048
DEPTH = 2

MEM_TOKENS = 256
HEAD_DIM = 64
MIX_WIDTH = D_MODEL
MEM_HEADS = 4
MEM_WIDTH = MEM_HEADS * HEAD_DIM
MAIN_WIDTH = MIX_WIDTH - MEM_WIDTH
S5_GROUP_DIM = 16
S5_GROUPS = MAIN_WIDTH // S5_GROUP_DIM
S5_STATE = 64
S5_DT_MIN = 0.001
S5_DT_MAX = 0.1
MOBA_HEADS = MAIN_WIDTH // HEAD_DIM
MOBA_KV_HEADS = 4
MOBA_GROUP = MOBA_HEADS // MOBA_KV_HEADS
KV_WIDTH = MOBA_KV_HEADS * HEAD_DIM
MOBA_BLOCK = 256
MOBA_TOP_BLOCKS = 3
MOBA_Q_CHUNK = 16
D_FF_DENSE = 256 * ((8 * D_MODEL // 3 + 255) // 256)
N_EXPERTS = 8
TOP_K = 2
D_FF_EXPERT = 7 * D_MODEL // 2
N_A_LAYERS = DEPTH // 2
RMS_EPS = 1e-6
NEG_INF = -1e30

kernel_name = "yoco_s5_moba_mem_moe_trunk"


def rms_norm(x, g):
    xf = x.astype(jnp.float32)
    y = xf * lax.rsqrt(jnp.mean(xf * xf, axis=-1, keepdims=True) + RMS_EPS)
    return (y * g.astype(jnp.float32)).astype(x.dtype)


def s5_mixer(u, lam_re, lam_im, log_dt, b_re, b_im, c_re, c_im, d_skip, w_glu, b_glu):
    f32 = jnp.float32
    bsz, s, _ = u.shape
    uf = u.astype(f32).reshape(bsz, s, S5_GROUPS, S5_GROUP_DIM)
    lam = lax.complex(lam_re.astype(f32), lam_im.astype(f32))
    dt = jnp.exp(log_dt.astype(f32))[:, None]
    lam_bar = jnp.exp(lam * dt)
    b = lax.complex(b_re.astype(f32), b_im.astype(f32))
    b_bar = ((lam_bar - 1.0) / lam)[..., None] * b
    bu = jnp.einsum('gpn,bsgn->bsgp', b_bar, uf)
    a = jnp.broadcast_to(lam_bar, bu.shape)

    def combine(left, right):
        a_l, b_l = left
        a_r, b_r = right
        return a_r * a_l, a_r * b_l + b_r

    _, states = lax.associative_scan(combine, (a, bu), axis=1)
    c = lax.complex(c_re.astype(f32), c_im.astype(f32))
    y = jnp.einsum('gnp,bsgp->bsgn', c, states).real \
        + d_skip.astype(f32).reshape(S5_GROUPS, S5_GROUP_DIM) * uf
    y = y.reshape(bsz, s, MAIN_WIDTH)
    g = jax.nn.gelu(y)
    out = g * jax.nn.sigmoid(g @ w_glu.astype(f32) + b_glu.astype(f32))
    return out.astype(u.dtype)


def memory_attention(q, mem, mem_norm, w_mem_k, w_mem_v, q_gain, k_gain):
    bsz, s, _ = q.shape
    mem_n = rms_norm(mem, mem_norm)
    qh = rms_norm(q.reshape(bsz, s, MEM_HEADS, HEAD_DIM), q_gain)
    kh = rms_norm((mem_n @ w_mem_k).reshape(bsz, MEM_TOKENS, MEM_HEADS, HEAD_DIM), k_gain)
    vh = (mem_n @ w_mem_v).reshape(bsz, MEM_TOKENS, MEM_HEADS, HEAD_DIM)
    logits = jnp.einsum('bshd,bmhd->bhsm', qh, kh).astype(jnp.float32) * (HEAD_DIM ** -0.5)
    p = jax.nn.softmax(logits, axis=-1).astype(vh.dtype)
    o = jnp.einsum('bhsm,bmhd->bshd', p, vh)
    return o.reshape(bsz, s, MEM_WIDTH)


def shared_kv(h, kv_norm, w_k, w_v, k_gain):
    bsz, s, _ = h.shape
    hn = rms_norm(h, kv_norm)
    k = rms_norm((hn @ w_k).reshape(bsz, s, MOBA_KV_HEADS, HEAD_DIM), k_gain)
    v = (hn @ w_v).reshape(bsz, s, MOBA_KV_HEADS, HEAD_DIM)
    nb = -(-s // MOBA_BLOCK)
    pad = nb * MOBA_BLOCK - s
    k = jnp.pad(k, ((0, 0), (0, pad), (0, 0), (0, 0)))
    v = jnp.pad(v, ((0, 0), (0, pad), (0, 0), (0, 0)))
    k_blocks = k.reshape(bsz, nb, MOBA_BLOCK, MOBA_KV_HEADS, HEAD_DIM).transpose(0, 3, 1, 2, 4)
    v_blocks = v.reshape(bsz, nb, MOBA_BLOCK, MOBA_KV_HEADS, HEAD_DIM).transpose(0, 3, 1, 2, 4)
    k_means = jnp.mean(k_blocks.astype(jnp.float32), axis=3).astype(k.dtype)
    return k_blocks, v_blocks, k_means


def moba_attention(q, k_blocks, v_blocks, k_means, q_gain):
    bsz, s, _ = q.shape
    nb = k_blocks.shape[2]
    k_sel = min(MOBA_TOP_BLOCKS, nb)
    n_chunks = s // MOBA_Q_CHUNK
    scale = HEAD_DIM ** -0.5
    qh = rms_norm(q.reshape(bsz, s, MOBA_HEADS, HEAD_DIM), q_gain)
    qh = qh.reshape(bsz, s, MOBA_KV_HEADS, MOBA_GROUP, HEAD_DIM)
    b_idx = jnp.arange(bsz)[:, None, None, None, None]
    h_idx = jnp.arange(MOBA_KV_HEADS)[None, :, None, None, None]
    blk_ids = jnp.arange(nb)
    slot_ids = jnp.arange(k_sel)

    def chunk(ci):
        start = ci * MOBA_Q_CHUNK
        qc = lax.dynamic_slice_in_dim(qh, start, MOBA_Q_CHUNK, axis=1)
        blk = start // MOBA_BLOCK
        gate = jnp.einsum('bqkgd,bknd->bkgqn', qc, k_means).astype(jnp.float32)
        gate = jnp.where(blk_ids < blk, gate, NEG_INF)
        _, top_i = lax.top_k(gate, k_sel)
        valid = slot_ids < blk
        ks = k_blocks[b_idx, h_idx, top_i]
        vs = v_blocks[b_idx, h_idx, top_i]
        s_sel = jnp.einsum('bqkgd,bkgqnjd->bkgqnj', qc, ks).astype(jnp.float32) * scale
        s_sel = jnp.where(valid[:, None], s_sel, NEG_INF)
        k_own = lax.dynamic_index_in_dim(k_blocks, blk, axis=2, keepdims=False)
        v_own = lax.dynamic_index_in_dim(v_blocks, blk, axis=2, keepdims=False)
        s_own = jnp.einsum('bqkgd,bkjd->bkgqj', qc, k_own).astype(jnp.float32) * scale
        q_pos = start + jnp.arange(MOBA_Q_CHUNK)
        k_pos = blk * MOBA_BLOCK + jnp.arange(MOBA_BLOCK)
        s_own = jnp.where(k_pos[None, :] <= q_pos[:, None], s_own, NEG_INF)
        logits = jnp.concatenate(
            [s_sel.reshape(bsz, MOBA_KV_HEADS, MOBA_GROUP, MOBA_Q_CHUNK, k_sel * MOBA_BLOCK), s_own],
            axis=-1)
        p = jax.nn.softmax(logits, axis=-1).astype(vs.dtype)
        p_sel = p[..., :k_sel * MOBA_BLOCK].reshape(
            bsz, MOBA_KV_HEADS, MOBA_GROUP, MOBA_Q_CHUNK, k_sel, MOBA_BLOCK)
        p_own = p[..., k_sel * MOBA_BLOCK:]
        o = jnp.einsum('bkgqnj,bkgqnjd->bqkgd', p_sel, vs) \
            + jnp.einsum('bkgqj,bkjd->bqkgd', p_own, v_own)
        return o

    outs = lax.map(chunk, jnp.arange(n_chunks))
    return outs.transpose(1, 0, 2, 3, 4, 5).reshape(bsz, s, MAIN_WIDTH)


def swiglu(h, w_gate, w_up, w_down):
    return (jax.nn.silu(h @ w_gate) * (h @ w_up)) @ w_down


def moe_swiglu(h, w_router, w_gate, w_up, w_down):
    bsz, s, d = h.shape
    t = bsz * s
    ht = h.reshape(t, d)
    logits = (ht @ w_router).astype(jnp.float32)
    top_vals, top_idx = lax.top_k(logits, TOP_K)
    gates = jax.nn.softmax(top_vals, axis=-1)
    flat_e = top_idx.reshape(-1)
    order = jnp.argsort(flat_e)
    tok = order // TOP_K
    xs = ht[tok]
    group_sizes = jnp.bincount(flat_e, length=N_EXPERTS).astype(jnp.int32)
    g = lax.ragged_dot(xs, w_gate, group_sizes)
    u = lax.ragged_dot(xs, w_up, group_sizes)
    y = lax.ragged_dot(jax.nn.silu(g) * u, w_down, group_sizes)
    y = y * gates.reshape(-1)[order][:, None].astype(y.dtype)
    out = jax.ops.segment_sum(y, tok, num_segments=t)
    return out.reshape(bsz, s, d)


def setup_inputs(seed: int = 0) -> dict:
    key = jax.random.key(seed)
    keys = iter(jax.random.split(key, 64))
    f32 = jnp.float32

    def normal(shape, scale):
        return scale * jax.random.normal(next(keys), shape, f32)

    def gain(n):
        return 1.0 + normal((n,), 0.02)

    n_idx = jnp.arange(S5_STATE, dtype=f32)[None, :]
    inp = {}
    inp["x"] = normal((BATCH, SEQ, D_MODEL), 1.0)
    inp["mem"] = normal((BATCH, MEM_TOKENS, D_MODEL), 1.0)
    inp["l0_mix_norm"] = gain(D_MODEL)
    inp["l0_w_in"] = normal((D_MODEL, MIX_WIDTH), D_MODEL ** -0.5)
    inp["l0_s5_lam_re"] = -0.5 + normal((S5_GROUPS, S5_STATE), 0.01)
    inp["l0_s5_lam_im"] = math.pi * n_idx + normal((S5_GROUPS, S5_STATE), 0.01)
    inp["l0_s5_log_dt"] = jax.random.uniform(next(keys), (S5_GROUPS,), f32,
                                             math.log(S5_DT_MIN), math.log(S5_DT_MAX))
    inp["l0_s5_b_re"] = normal((S5_GROUPS, S5_STATE, S5_GROUP_DIM), (2 * S5_GROUP_DIM) ** -0.5)
    inp["l0_s5_b_im"] = normal((S5_GROUPS, S5_STATE, S5_GROUP_DIM), (2 * S5_GROUP_DIM) ** -0.5)
    inp["l0_s5_c_re"] = normal((S5_GROUPS, S5_GROUP_DIM, S5_STATE), S5_STATE ** -0.5)
    inp["l0_s5_c_im"] = normal((S5_GROUPS, S5_GROUP_DIM, S5_STATE), S5_STATE ** -0.5)
    inp["l0_s5_d"] = normal((MAIN_WIDTH,), 1.0)
    inp["l0_s5_w_glu"] = normal((MAIN_WIDTH, MAIN_WIDTH), MAIN_WIDTH ** -0.5)
    inp["l0_s5_b_glu"] = normal((MAIN_WIDTH,), 0.01)
    inp["l0_mem_norm"] = gain(D_MODEL)
    inp["l0_w_mem_k"] = normal((D_MODEL, MEM_WIDTH), D_MODEL ** -0.5)
    inp["l0_w_mem_v"] = normal((D_MODEL, MEM_WIDTH), D_MODEL ** -0.5)
    inp["l0_mem_q_gain"] = gain(HEAD_DIM)
    inp["l0_mem_k_gain"] = gain(HEAD_DIM)
    inp["l0_w_out"] = normal((MIX_WIDTH, D_MODEL), MIX_WIDTH ** -0.5)
    inp["l0_ffn_norm"] = gain(D_MODEL)
    inp["l0_ffn_w_gate"] = normal((D_MODEL, D_FF_DENSE), D_MODEL ** -0.5)
    inp["l0_ffn_w_up"] = normal((D_MODEL, D_FF_DENSE), D_MODEL ** -0.5)
    inp["l0_ffn_w_down"] = normal((D_FF_DENSE, D_MODEL), D_FF_DENSE ** -0.5)
    inp["kv_norm"] = gain(D_MODEL)
    inp["kv_w_k"] = normal((D_MODEL, KV_WIDTH), D_MODEL ** -0.5)
    inp["kv_w_v"] = normal((D_MODEL, KV_WIDTH), D_MODEL ** -0.5)
    inp["kv_k_gain"] = gain(HEAD_DIM)
    inp["l1_mix_norm"] = gain(D_MODEL)
    inp["l1_w_in"] = normal((D_MODEL, MIX_WIDTH), D_MODEL ** -0.5)
    inp["l1_moba_q_gain"] = gain(HEAD_DIM)
    inp["l1_mem_norm"] = gain(D_MODEL)
    inp["l1_w_mem_k"] = normal((D_MODEL, MEM_WIDTH), D_MODEL ** -0.5)
    inp["l1_w_mem_v"] = normal((D_MODEL, MEM_WIDTH), D_MODEL ** -0.5)
    inp["l1_mem_q_gain"] = gain(HEAD_DIM)
    inp["l1_mem_k_gain"] = gain(HEAD_DIM)
    inp["l1_w_out"] = normal((MIX_WIDTH, D_MODEL), MIX_WIDTH ** -0.5)
    inp["l1_ffn_norm"] = gain(D_MODEL)
    inp["l1_moe_router"] = normal((D_MODEL, N_EXPERTS), D_MODEL ** -0.5)
    inp["l1_moe_w_gate"] = normal((N_EXPERTS, D_MODEL, D_FF_EXPERT), D_MODEL ** -0.5)
    inp["l1_moe_w_up"] = normal((N_EXPERTS, D_MODEL, D_FF_EXPERT), D_MODEL ** -0.5)
    inp["l1_moe_w_down"] = normal((N_EXPERTS, D_FF_EXPERT, D_MODEL), D_FF_EXPERT ** -0.5)
    return inp


def reference(x, mem,
              l0_mix_norm, l0_w_in, l0_s5_lam_re, l0_s5_lam_im, l0_s5_log_dt,
              l0_s5_b_re, l0_s5_b_im, l0_s5_c_re, l0_s5_c_im, l0_s5_d, l0_s5_w_glu, l0_s5_b_glu,
              l0_mem_norm, l0_w_mem_k, l0_w_mem_v, l0_mem_q_gain, l0_mem_k_gain,
              l0_w_out, l0_ffn_norm, l0_ffn_w_gate, l0_ffn_w_up, l0_ffn_w_down,
              kv_norm, kv_w_k, kv_w_v, kv_k_gain,
              l1_mix_norm, l1_w_in, l1_moba_q_gain,
              l1_mem_norm, l1_w_mem_k, l1_w_mem_v, l1_mem_q_gain, l1_mem_k_gain,
              l1_w_out, l1_ffn_norm, l1_moe_router, l1_moe_w_gate, l1_moe_w_up, l1_moe_w_down):
    layers = [
        dict(mix_norm=l0_mix_norm, w_in=l0_w_in,
             main=(l0_s5_lam_re, l0_s5_lam_im, l0_s5_log_dt, l0_s5_b_re, l0_s5_b_im,
                   l0_s5_c_re, l0_s5_c_im, l0_s5_d, l0_s5_w_glu, l0_s5_b_glu),
             mem=(l0_mem_norm, l0_w_mem_k, l0_w_mem_v, l0_mem_q_gain, l0_mem_k_gain),
             w_out=l0_w_out, ffn_norm=l0_ffn_norm,
             ffn=(l0_ffn_w_gate, l0_ffn_w_up, l0_ffn_w_down)),
        dict(mix_norm=l1_mix_norm, w_in=l1_w_in,
             main=(l1_moba_q_gain,),
             mem=(l1_mem_norm, l1_w_mem_k, l1_w_mem_v, l1_mem_q_gain, l1_mem_k_gain),
             w_out=l1_w_out, ffn_norm=l1_ffn_norm,
             ffn=(l1_moe_router, l1_moe_w_gate, l1_moe_w_up, l1_moe_w_down)),
    ]
    kv = None
    for i in range(DEPTH):
        p = layers[i]
        h = rms_norm(x, p["mix_norm"])
        proj = h @ p["w_in"]
        main_in = proj[..., :MAIN_WIDTH]
        q_mem = proj[..., MAIN_WIDTH:]
        if i < N_A_LAYERS:
            main = s5_mixer(main_in, *p["main"])
        else:
            if kv is None:
                kv = shared_kv(x, kv_norm, kv_w_k, kv_w_v, kv_k_gain)
            main = moba_attention(main_in, *kv, *p["main"])
        mem_out = memory_attention(q_mem, mem, *p["mem"])
        x = x + jnp.concatenate([main, mem_out], axis=-1) @ p["w_out"]
        h = rms_norm(x, p["ffn_norm"])
        if i % 2 == 0:
            x = x + swiglu(h, *p["ffn"])
        else:
            x = x + moe_swiglu(h, *p["ffn"])
    return x
```

```python
import functools

import jax
import jax.numpy as jnp
from jax import lax
from jax.experimental import pallas as pl
from jax.experimental.pallas import tpu as pltpu

F32 = jnp.float32
BF16 = jnp.bfloat16
I32 = jnp.int32

RMS_EPS = 1e-6
NEG_INF = -1e30
HEAD_DIM = 64
HEAD_SLAB = 128
MEM_HEADS = 4
MOBA_KV_HEADS = 4
MOBA_GROUP = 3
MOBA_BLOCK = 256
MOBA_TOP_BLOCKS = 3
S5_GROUP_DIM = 16
S5_STATE = 64
S5_GROUPS_PER_SLAB = HEAD_SLAB // S5_GROUP_DIM
S5_SLAB_STATES = S5_GROUPS_PER_SLAB * S5_STATE
N_EXPERTS = 8
ROUTER_LANES = 128
SUBLANES = 8

VMEM_LIMIT_BYTES = 56 * 1024 * 1024

_NT = (((1,), (1,)), ((), ()))
_TN = (((0,), (0,)), ((), ()))


def _cparams(sem):
    return pltpu.CompilerParams(dimension_semantics=sem, vmem_limit_bytes=VMEM_LIMIT_BYTES)


def _rms(x, g):
    return x * lax.rsqrt(jnp.mean(x * x, axis=-1, keepdims=True) + RMS_EPS) * g


def _slab_norm_t(qt, n_heads, scale):
    outs = []
    for h in range(n_heads):
        s = qt[HEAD_SLAB * h:HEAD_SLAB * (h + 1), :]
        ms = jnp.sum(s * s, axis=0, keepdims=True) * (1.0 / HEAD_DIM)
        outs.append(s * (lax.rsqrt(ms + RMS_EPS) * scale))
    return outs


def _inproj0_kernel(x_ref, g_ref, wm_ref, wqt_ref, main_ref, qt_ref):
    hb = _rms(x_ref[...], g_ref[...]).astype(BF16)
    main_ref[...] = jnp.dot(hb, wm_ref[...], preferred_element_type=F32)
    qt = lax.dot_general(wqt_ref[...], hb, _NT, preferred_element_type=F32)
    for h, s in enumerate(_slab_norm_t(qt, MEM_HEADS, HEAD_DIM ** -0.5)):
        qt_ref[0, HEAD_SLAB * h:HEAD_SLAB * (h + 1), :] = s.astype(BF16)


def _inproj0(x2d, g, wm, wqt, batch, seq, tm):
    t, d = x2d.shape
    per_b = seq // tm
    nm = wm.shape[1]
    nq = wqt.shape[0]
    return pl.pallas_call(
        _inproj0_kernel,
        grid=(t // tm,),
        in_specs=[
            pl.BlockSpec((tm, d), lambda i: (i, 0)),
            pl.BlockSpec((1, d), lambda i: (0, 0)),
            pl.BlockSpec((d, nm), lambda i: (0, 0)),
            pl.BlockSpec((nq, d), lambda i: (0, 0)),
        ],
        out_specs=[
            pl.BlockSpec((tm, nm), lambda i: (i, 0)),
            pl.BlockSpec((1, nq, tm), lambda i: (i // per_b, 0, i % per_b)),
        ],
        out_shape=[
            jax.ShapeDtypeStruct((t, nm), F32),
            jax.ShapeDtypeStruct((batch, nq, seq), BF16),
        ],
        compiler_params=_cparams(("parallel",)),
        name="inproj0",
    )(x2d, g, wm, wqt)


def _mem_kv_kernel(mem_ref, g_ref, wk_ref, wvt_ref, ones_ref, kg_ref, k_ref, vt_ref):
    mb = _rms(mem_ref[0], g_ref[...]).astype(BF16)
    k = jnp.dot(mb, wk_ref[...], preferred_element_type=F32)
    ms = jnp.dot(k * k, ones_ref[...], preferred_element_type=F32,
                 precision=lax.Precision.HIGHEST) * (1.0 / HEAD_DIM)
    k_ref[0] = (k * lax.rsqrt(ms + RMS_EPS) * kg_ref[...]).astype(BF16)
    vt_ref[0] = lax.dot_general(wvt_ref[...], mb, _NT, preferred_element_type=F32).astype(BF16)


def _mem_kv(mem, g, wk, wvt, ones, kg):
    b, m, d = mem.shape
    w = wk.shape[1]
    return pl.pallas_call(
        _mem_kv_kernel,
        grid=(b,),
        in_specs=[
            pl.BlockSpec((1, m, d), lambda i: (i, 0, 0)),
            pl.BlockSpec((1, d), lambda i: (0, 0)),
            pl.BlockSpec((d, w), lambda i: (0, 0)),
            pl.BlockSpec((w, d), lambda i: (0, 0)),
            pl.BlockSpec((w, w), lambda i: (0, 0)),
            pl.BlockSpec((1, w), lambda i: (0, 0)),
        ],
        out_specs=[
            pl.BlockSpec((1, m, w), lambda i: (i, 0, 0)),
            pl.BlockSpec((1, w, m), lambda i: (i, 0, 0)),
        ],
        out_shape=[
            jax.ShapeDtypeStruct((b, m, w), BF16),
            jax.ShapeDtypeStruct((b, w, m), BF16),
        ],
        compiler_params=_cparams(("parallel",)),
        name="mem_kv",
    )(mem, g, wk, wvt, ones, kg)


def _mem_attn_kernel(qt_ref, k_ref, vt_ref, o_ref):
    for h in range(MEM_HEADS):
        pair = h // 2
        k2 = k_ref[0, :, HEAD_SLAB * pair:HEAD_SLAB * (pair + 1)]
        q = qt_ref[0, HEAD_SLAB * h:HEAD_SLAB * (h + 1), :]
        s = jnp.dot(k2, q, preferred_element_type=F32)
        m = jnp.max(s, axis=0, keepdims=True)
        p = jnp.exp(s - m)
        l = jnp.sum(p, axis=0, keepdims=True)
        v = vt_ref[0, HEAD_DIM * h:HEAD_DIM * (h + 1), :]
        o = jnp.dot(v, p.astype(BF16), preferred_element_type=F32)
        o_ref[0, HEAD_DIM * h:HEAD_DIM * (h + 1), :] = (o / l).astype(BF16)


def _mem_attn(qt, k, vt, tq):
    b, nq, s = qt.shape
    m, w = k.shape[1], k.shape[2]
    return pl.pallas_call(
        _mem_attn_kernel,
        grid=(b, s // tq),
        in_specs=[
            pl.BlockSpec((1, nq, tq), lambda i, j: (i, 0, j)),
            pl.BlockSpec((1, m, w), lambda i, j: (i, 0, 0)),
            pl.BlockSpec((1, w, m), lambda i, j: (i, 0, 0)),
        ],
        out_specs=pl.BlockSpec((1, w, tq), lambda i, j: (i, 0, j)),
        out_shape=jax.ShapeDtypeStruct((b, w, s), BF16),
        compiler_params=_cparams(("parallel", "parallel")),
        name="mem_attn",
    )(qt, k, vt)


def _s5_kernel(u_ref, bd_ref, cd_ref, ar_ref, ai_ref, d_ref, wglu_ref, bglu_ref, o_ref,
               buf_ref, st_ref, *, ts, n_slabs):
    half = S5_SLAB_STATES
    width = 2 * half

    @pl.when(pl.program_id(0) == 0)
    def _():
        st_ref[...] = jnp.zeros_like(st_ref)

    u = u_ref[...]
    ub = u.astype(BF16)
    for j in range(n_slabs):
        buf_ref[:, width * j:width * (j + 1)] = jnp.dot(
            ub[:, HEAD_SLAB * j:HEAD_SLAB * (j + 1)], bd_ref[j], preferred_element_type=F32)

    for j in range(n_slabs):
        re = slice(width * j, width * j + half)
        im = slice(width * j + half, width * (j + 1))
        ar = jnp.broadcast_to(ar_ref[j:j + 1, :], (SUBLANES, half))
        ai = jnp.broadcast_to(ai_ref[j:j + 1, :], (SUBLANES, half))

        def step(t, carry, re=re, im=im, ar=ar, ai=ai):
            xr, xi = carry
            rows = pl.ds(pl.multiple_of(t * SUBLANES, SUBLANES), SUBLANES)
            nxr = ar * xr - ai * xi + buf_ref[rows, re]
            nxi = ar * xi + ai * xr + buf_ref[rows, im]
            buf_ref[rows, re] = nxr
            buf_ref[rows, im] = nxi
            return nxr, nxi

        xr, xi = lax.fori_loop(0, ts, step, (st_ref[:, re], st_ref[:, im]), unroll=8)
        st_ref[:, re] = xr
        st_ref[:, im] = xi

    ys = []
    for j in range(n_slabs):
        ys.append(jnp.dot(buf_ref[:, width * j:width * (j + 1)].astype(BF16), cd_ref[j],
                          preferred_element_type=F32))
    y = jnp.concatenate(ys, axis=1) + d_ref[...] * u
    g = jax.nn.gelu(y)
    z = jnp.dot(g.astype(BF16), wglu_ref[...], preferred_element_type=F32) + bglu_ref[...]
    o_ref[...] = (g / (1.0 + jnp.exp(-z))).astype(BF16)


def _s5(u_tm, bd, cd, ar, ai, dskip, wglu, bglu, batch, ts):
    rows, c = u_tm.shape
    n_slabs = c // HEAD_SLAB
    width = 2 * S5_SLAB_STATES
    tile = ts * batch
    return pl.pallas_call(
        functools.partial(_s5_kernel, ts=ts, n_slabs=n_slabs),
        grid=(rows // tile,),
        in_specs=[
            pl.BlockSpec((tile, c), lambda i: (i, 0)),
            pl.BlockSpec((n_slabs, HEAD_SLAB, width), lambda i: (0, 0, 0)),
            pl.BlockSpec((n_slabs, width, HEAD_SLAB), lambda i: (0, 0, 0)),
            pl.BlockSpec((n_slabs, S5_SLAB_STATES), lambda i: (0, 0)),
            pl.BlockSpec((n_slabs, S5_SLAB_STATES), lambda i: (0, 0)),
            pl.BlockSpec((1, c), lambda i: (0, 0)),
            pl.BlockSpec((c, c), lambda i: (0, 0)),
            pl.BlockSpec((1, c), lambda i: (0, 0)),
        ],
        out_specs=pl.BlockSpec((tile, c), lambda i: (i, 0)),
        out_shape=jax.ShapeDtypeStruct((rows, c), BF16),
        scratch_shapes=[
            pltpu.VMEM((tile, n_slabs * width), F32),
            pltpu.VMEM((batch, n_slabs * width), F32),
        ],
        compiler_params=_cparams(("arbitrary",)),
        name="s5_mixer",
    )(u_tm, bd, cd, ar, ai, dskip, wglu, bglu)


def _outproj_body(a, a_transposed, bt_ref, wa_ref, wb_ref, x_ref, g_ref):
    dn = _TN if a_transposed else (((1,), (0,)), ((), ()))
    y = lax.dot_general(a, wa_ref[...], dn, preferred_element_type=F32)
    y = y + lax.dot_general(bt_ref[0], wb_ref[...], _TN, preferred_element_type=F32)
    xn = x_ref[...] + y
    return xn, _rms(xn, g_ref[...])


def _outproj0_kernel(a_ref, bt_ref, wa_ref, wb_ref, x_ref, g_ref, xo_ref, h_ref):
    xn, h = _outproj_body(a_ref[...], False, bt_ref, wa_ref, wb_ref, x_ref, g_ref)
    xo_ref[...] = xn
    h_ref[...] = h.astype(BF16)


def _outproj0(a, bt, wa, wb, x2d, g, seq, tm):
    t, d = x2d.shape
    per_b = seq // tm
    ka, kb = a.shape[1], bt.shape[1]
    return pl.pallas_call(
        _outproj0_kernel,
        grid=(t // tm,),
        in_specs=[
            pl.BlockSpec((tm, ka), lambda i: (i, 0)),
            pl.BlockSpec((1, kb, tm), lambda i: (i // per_b, 0, i % per_b)),
            pl.BlockSpec((ka, d), lambda i: (0, 0)),
            pl.BlockSpec((kb, d), lambda i: (0, 0)),
            pl.BlockSpec((tm, d), lambda i: (i, 0)),
            pl.BlockSpec((1, d), lambda i: (0, 0)),
        ],
        out_specs=[
            pl.BlockSpec((tm, d), lambda i: (i, 0)),
            pl.BlockSpec((tm, d), lambda i: (i, 0)),
        ],
        out_shape=[
            jax.ShapeDtypeStruct((t, d), F32),
            jax.ShapeDtypeStruct((t, d), BF16),
        ],
        compiler_params=_cparams(("parallel",)),
        name="outproj0",
    )(a, bt, wa, wb, x2d, g)


def _outproj1_kernel(at_ref, bt_ref, wa_ref, wb_ref, x_ref, g_ref, wr_ref,
                     xo_ref, h_ref, rec_ref, cnt_ref, run_ref, *, tm):
    i = pl.program_id(0)

    @pl.when(i == 0)
    def _():
        run_ref[...] = jnp.zeros_like(run_ref)

    xn, h = _outproj_body(at_ref[0], True, bt_ref, wa_ref, wb_ref, x_ref, g_ref)
    xo_ref[...] = xn
    h_ref[...] = h

    lane = lax.broadcasted_iota(I32, (tm, ROUTER_LANES), 1)
    logits = jnp.dot(h, wr_ref[...], preferred_element_type=F32, precision=lax.Precision.HIGHEST)
    logits = jnp.where(lane < N_EXPERTS, logits, -jnp.inf)
    m1 = jnp.max(logits, axis=-1, keepdims=True)
    e1 = jnp.min(jnp.where(logits == m1, lane, ROUTER_LANES), axis=-1, keepdims=True)
    rest = jnp.where(lane == e1, -jnp.inf, logits)
    m2 = jnp.max(rest, axis=-1, keepdims=True)
    e2 = jnp.min(jnp.where(rest == m2, lane, ROUTER_LANES), axis=-1, keepdims=True)
    w2 = jnp.exp(m2 - m1)
    g1 = 1.0 / (1.0 + w2)
    g2 = w2 / (1.0 + w2)

    hot = jnp.where((lane == e1) | (lane == e2), 1.0, 0.0)
    r_io = lax.broadcasted_iota(I32, (tm, tm), 0)
    c_io = lax.broadcasted_iota(I32, (tm, tm), 1)
    tril = jnp.where(c_io < r_io, 1.0, 0.0).astype(BF16)
    before = jnp.dot(tril, hot.astype(BF16), preferred_element_type=F32) + run_ref[...]
    r1 = jnp.sum(jnp.where(lane == e1, before, 0.0), axis=-1, keepdims=True)
    r2 = jnp.sum(jnp.where(lane == e2, before, 0.0), axis=-1, keepdims=True)
    run = run_ref[...] + jnp.sum(hot, axis=0, keepdims=True)
    run_ref[...] = run
    cnt_ref[...] = jnp.broadcast_to(run, cnt_ref.shape)

    rec = jnp.where(lane == 0, e1.astype(F32), 0.0)
    rec = jnp.where(lane == 1, e2.astype(F32), rec)
    rec = jnp.where(lane == 2, r1, rec)
    rec = jnp.where(lane == 3, r2, rec)
    rec = jnp.where(lane == 4, g1, rec)
    rec = jnp.where(lane == 5, g2, rec)
    rec_ref[...] = rec


def _outproj1(at, bt, wa, wb, x2d, g, wr, seq, tm):
    t, d = x2d.shape
    per_b = seq // tm
    ka, kb = at.shape[1], bt.shape[1]
    return pl.pallas_call(
        functools.partial(_outproj1_kernel, tm=tm),
        grid=(t // tm,),
        in_specs=[
            pl.BlockSpec((1, ka, tm), lambda i: (i // per_b, 0, i % per_b)),
            pl.BlockSpec((1, kb, tm), lambda i: (i // per_b, 0, i % per_b)),
            pl.BlockSpec((ka, d), lambda i: (0, 0)),
            pl.BlockSpec((kb, d), lambda i: (0, 0)),
            pl.BlockSpec((tm, d), lambda i: (i, 0)),
            pl.BlockSpec((1, d), lambda i: (0, 0)),
            pl.BlockSpec((d, ROUTER_LANES), lambda i: (0, 0)),
        ],
        out_specs=[
            pl.BlockSpec((tm, d), lambda i: (i, 0)),
            pl.BlockSpec((tm, d), lambda i: (i, 0)),
            pl.BlockSpec((tm, ROUTER_LANES), lambda i: (i, 0)),
            pl.BlockSpec((SUBLANES, ROUTER_LANES), lambda i: (0, 0)),
        ],
        out_shape=[
            jax.ShapeDtypeStruct((t, d), F32),
            jax.ShapeDtypeStruct((t, d), F32),
            jax.ShapeDtypeStruct((t, ROUTER_LANES), F32),
            jax.ShapeDtypeStruct((SUBLANES, ROUTER_LANES), F32),
        ],
        scratch_shapes=[pltpu.VMEM((1, ROUTER_LANES), F32)],
        compiler_params=_cparams(("arbitrary",)),
        name="outproj1_router",
    )(at, bt, wa, wb, x2d, g, wr)


def _ffn_kernel(te_ref, nu_ref, *refs, has_res, cast_in):
    if has_res:
        xs_ref, res_ref, wg_ref, wu_ref, wd_ref, o_ref, acc_ref, xb_ref = refs
    else:
        xs_ref, wg_ref, wu_ref, wd_ref, o_ref, acc_ref, xb_ref = refs
        res_ref = None
    i = pl.program_id(0)
    f = pl.program_id(1)
    nf = pl.num_programs(1)

    @pl.when(i < nu_ref[0])
    def _():
        @pl.when(f == 0)
        def _():
            acc_ref[...] = jnp.zeros_like(acc_ref)
            xb_ref[...] = xs_ref[...].astype(BF16) if cast_in else xs_ref[...]

        xb = xb_ref[...]
        gt = jnp.dot(xb, wg_ref[0], preferred_element_type=F32)
        up = jnp.dot(xb, wu_ref[0], preferred_element_type=F32)
        mid = (gt / (1.0 + jnp.exp(-gt))) * up
        acc_ref[...] += jnp.dot(mid.astype(BF16), wd_ref[0], preferred_element_type=F32)

        @pl.when(f == nf - 1)
        def _():
            o_ref[...] = acc_ref[...] + res_ref[...] if has_res else acc_ref[...]

    @pl.when((i >= nu_ref[0]) & (f == nf - 1))
    def _():
        o_ref[...] = jnp.zeros_like(o_ref)


def _ffn(tile_expert, n_used, xs, res, wg, wu, wd, tm, tf):
    rows, d = xs.shape
    ff = wg.shape[2]
    nf = ff // tf
    has_res = res is not None
    cast_in = xs.dtype != BF16

    def row_map(i, f, te, nu):
        return (jnp.minimum(i, nu[0] - 1), 0)

    def f_of(i, f, nu):
        return jnp.where(i < nu[0], f, nf - 1)

    def wcol_map(i, f, te, nu):
        return (te[jnp.minimum(i, nu[0] - 1)], 0, f_of(i, f, nu))

    def wrow_map(i, f, te, nu):
        return (te[jnp.minimum(i, nu[0] - 1)], f_of(i, f, nu), 0)

    in_specs = [pl.BlockSpec((tm, d), row_map)]
    args = [xs]
    if has_res:
        in_specs.append(pl.BlockSpec((tm, d), row_map))
        args.append(res)
    in_specs += [
        pl.BlockSpec((1, d, tf), wcol_map),
        pl.BlockSpec((1, d, tf), wcol_map),
        pl.BlockSpec((1, tf, d), wrow_map),
    ]
    args += [wg, wu, wd]
    grid_spec = pltpu.PrefetchScalarGridSpec(
        num_scalar_prefetch=2,
        grid=(rows // tm, nf),
        in_specs=in_specs,
        out_specs=pl.BlockSpec((tm, d), lambda i, f, te, nu: (i, 0)),
        scratch_shapes=[pltpu.VMEM((tm, d), F32), pltpu.VMEM((tm, d), BF16)],
    )
    return pl.pallas_call(
        functools.partial(_ffn_kernel, has_res=has_res, cast_in=cast_in),
        grid_spec=grid_spec,
        out_shape=jax.ShapeDtypeStruct((rows, d), F32),
        compiler_params=_cparams(("arbitrary", "arbitrary")),
        name="ffn_res" if has_res else "ffn_moe",
    )(tile_expert, n_used, *args)


def _proj1_kernel(x_ref, g1_ref, g2_ref, wqt_ref, wk_ref, wvt_ref, ones_ref, kg_ref, qg_ref,
                  qt_ref, qmt_ref, k_ref, km_ref, vt_ref, *, tm, n_q):
    x = x_ref[...]
    h1 = _rms(x, g1_ref[...]).astype(BF16)
    h2 = _rms(x, g2_ref[...]).astype(BF16)
    qt = lax.dot_general(wqt_ref[...], h1, _NT, preferred_element_type=F32)
    slabs = _slab_norm_t(qt, n_q + MEM_HEADS, HEAD_DIM ** -0.5)
    for h in range(n_q):
        qt_ref[0, HEAD_SLAB * h:HEAD_SLAB * (h + 1), :] = slabs[h].astype(BF16)
    for h in range(MEM_HEADS):
        qmt_ref[0, HEAD_SLAB * h:HEAD_SLAB * (h + 1), :] = slabs[n_q + h].astype(BF16)

    k = jnp.dot(h2, wk_ref[...], preferred_element_type=F32)
    ms = jnp.dot(k * k, ones_ref[...], preferred_element_type=F32,
                 precision=lax.Precision.HIGHEST) * (1.0 / HEAD_DIM)
    kn = k * lax.rsqrt(ms + RMS_EPS) * kg_ref[...]
    kq = kn * qg_ref[...]
    k_ref[...] = kq.astype(BF16)
    nblk = tm // MOBA_BLOCK
    for j in range(nblk):
        km_ref[j] = jnp.mean(kn[MOBA_BLOCK * j:MOBA_BLOCK * (j + 1), :], axis=0,
                             keepdims=True) * qg_ref[...]
    vt = lax.dot_general(wvt_ref[...], h2, _NT, preferred_element_type=F32).astype(BF16)
    for kv in range(MOBA_KV_HEADS):
        for j in range(nblk):
            vt_ref[0, kv, j] = vt[HEAD_DIM * kv:HEAD_DIM * (kv + 1),
                                  MOBA_BLOCK * j:MOBA_BLOCK * (j + 1)]


def _proj1(x2d, g1, g2, wqt, wk, wvt, ones, kg, qg, batch, seq, tm):
    t, d = x2d.shape
    per_b = seq // tm
    nq_rows = wqt.shape[0]
    n_q = nq_rows // HEAD_SLAB - MEM_HEADS
    kw = wk.shape[1]
    nblk = tm // MOBA_BLOCK
    nb_seq = seq // MOBA_BLOCK
    return pl.pallas_call(
        functools.partial(_proj1_kernel, tm=tm, n_q=n_q),
        grid=(t // tm,),
        in_specs=[
            pl.BlockSpec((tm, d), lambda i: (i, 0)),
            pl.BlockSpec((1, d), lambda i: (0, 0)),
            pl.BlockSpec((1, d), lambda i: (0, 0)),
            pl.BlockSpec((nq_rows, d), lambda i: (0, 0)),
            pl.BlockSpec((d, kw), lambda i: (0, 0)),
            pl.BlockSpec((kw, d), lambda i: (0, 0)),
            pl.BlockSpec((kw, kw), lambda i: (0, 0)),
            pl.BlockSpec((1, kw), lambda i: (0, 0)),
            pl.BlockSpec((1, kw), lambda i: (0, 0)),
        ],
        out_specs=[
            pl.BlockSpec((1, n_q * HEAD_SLAB, tm), lambda i: (i // per_b, 0, i % per_b)),
            pl.BlockSpec((1, MEM_HEADS * HEAD_SLAB, tm), lambda i: (i // per_b, 0, i % per_b)),
            pl.BlockSpec((tm, kw), lambda i: (i, 0)),
            pl.BlockSpec((nblk, 1, kw), lambda i: (i, 0, 0)),
            pl.BlockSpec((1, MOBA_KV_HEADS, nblk, HEAD_DIM, MOBA_BLOCK),
                         lambda i: (i // per_b, 0, i % per_b, 0, 0)),
        ],
        out_shape=[
            jax.ShapeDtypeStruct((batch, n_q * HEAD_SLAB, seq), BF16),
            jax.ShapeDtypeStruct((batch, MEM_HEADS * HEAD_SLAB, seq), BF16),
            jax.ShapeDtypeStruct((t, kw), BF16),
            jax.ShapeDtypeStruct((t // MOBA_BLOCK, 1, kw), F32),
            jax.ShapeDtypeStruct((batch, MOBA_KV_HEADS, nb_seq, HEAD_DIM, MOBA_BLOCK), BF16),
        ],
        compiler_params=_cparams(("parallel",)),
        name="proj1",
    )(x2d, g1, g2, wqt, wk, wvt, ones, kg, qg)


def _moba_kernel(qt_ref, k_ref, vt_ref, km_ref, o_ref, sel_ref, m_ref, l_ref, acc_ref):
    qb = pl.program_id(2)
    nb = km_ref.shape[1]
    bq = MOBA_BLOCK
    blk = lax.broadcasted_iota(I32, (nb, bq), 0)
    km = km_ref[0]

    for g in range(MOBA_GROUP):
        q = qt_ref[0, HEAD_SLAB * g:HEAD_SLAB * (g + 1), :].astype(F32)
        gate = jnp.dot(km, q, preferred_element_type=F32, precision=lax.Precision.HIGHEST)
        gate = jnp.where(blk < qb, gate, NEG_INF)
        rank = jnp.zeros((nb, bq), F32)
        for m in range(nb):
            gm = gate[m:m + 1, :]
            beats = jnp.where(gm > gate, 1.0, jnp.where((gm == gate) & (blk > m), 1.0, 0.0))
            rank = rank + beats
        sel_ref[g] = jnp.where((blk < qb) & (rank < MOBA_TOP_BLOCKS), 1.0, 0.0)
        m_ref[g] = jnp.full((1, bq), NEG_INF, F32)
        l_ref[g] = jnp.zeros((1, bq), F32)
        acc_ref[g] = jnp.zeros((HEAD_DIM, bq), F32)

    def attend(kb, vb, keep):
        for g in range(MOBA_GROUP):
            q = qt_ref[0, HEAD_SLAB * g:HEAD_SLAB * (g + 1), :]
            s = jnp.dot(kb, q, preferred_element_type=F32)
            s = jnp.where(keep(g), s, NEG_INF)
            m_old = m_ref[g]
            m_new = jnp.maximum(m_old, jnp.max(s, axis=0, keepdims=True))
            alpha = jnp.exp(m_old - m_new)
            p = jnp.exp(s - m_new)
            l_ref[g] = alpha * l_ref[g] + jnp.sum(p, axis=0, keepdims=True)
            acc_ref[g] = alpha * acc_ref[g] + jnp.dot(vb, p.astype(BF16),
                                                      preferred_element_type=F32)
            m_ref[g] = m_new

    def past(n, carry):
        kb = k_ref[0, pl.ds(pl.multiple_of(n * bq, bq), bq), :]
        attend(kb, vt_ref[0, 0, n], lambda g: sel_ref[g, pl.ds(n, 1), :] > 0.0)
        return carry

    lax.fori_loop(0, qb, past, 0)

    key_pos = lax.broadcasted_iota(I32, (bq, bq), 0)
    qry_pos = lax.broadcasted_iota(I32, (bq, bq), 1)
    kb = k_ref[0, pl.ds(pl.multiple_of(qb * bq, bq), bq), :]
    attend(kb, vt_ref[0, 0, qb], lambda g: key_pos <= qry_pos)

    for g in range(MOBA_GROUP):
        o_ref[0, HEAD_DIM * g:HEAD_DIM * (g + 1), :] = (acc_ref[g] / l_ref[g]).astype(BF16)


def _moba(qt, k3, vt5, km3):
    b, nq_rows, s = qt.shape
    kw = k3.shape[2]
    nb = s // MOBA_BLOCK
    rows_out = MOBA_GROUP * HEAD_DIM
    return pl.pallas_call(
        _moba_kernel,
        grid=(b, MOBA_KV_HEADS, nb),
        in_specs=[
            pl.BlockSpec((1, MOBA_GROUP * HEAD_SLAB, MOBA_BLOCK), lambda i, j, q: (i, j, q)),
            pl.BlockSpec((1, s, HEAD_SLAB), lambda i, j, q: (i, 0, j // 2)),
            pl.BlockSpec((1, 1, nb, HEAD_DIM, MOBA_BLOCK), lambda i, j, q: (i, j, 0, 0, 0)),
            pl.BlockSpec((1, nb, HEAD_SLAB), lambda i, j, q: (i, 0, j // 2)),
        ],
        out_specs=pl.BlockSpec((1, rows_out, MOBA_BLOCK), lambda i, j, q: (i, j, q)),
        out_shape=jax.ShapeDtypeStruct((b, MOBA_KV_HEADS * rows_out, s), BF16),
        scratch_shapes=[
            pltpu.VMEM((MOBA_GROUP, nb, MOBA_BLOCK), F32),
            pltpu.VMEM((MOBA_GROUP, 1, MOBA_BLOCK), F32),
            pltpu.VMEM((MOBA_GROUP, 1, MOBA_BLOCK), F32),
            pltpu.VMEM((MOBA_GROUP, HEAD_DIM, MOBA_BLOCK), F32),
        ],
        compiler_params=_cparams(("parallel", "parallel", "arbitrary")),
        name="moba_attn",
    )(qt, k3, vt5, km3)


def _row_copy(src_ref, src_row, dst_ref, dst_row, sem):
    return pltpu.make_async_copy(src_ref.at[pl.ds(src_row, 1)], dst_ref.at[pl.ds(dst_row, 1)], sem)


def _dispatch_kernel(dest_ref, h_ref, init_ref, xs_ref, sem, *, tm):
    del init_ref

    def issue(j, carry):
        _row_copy(h_ref, lax.rem(j, tm), xs_ref, dest_ref[0, 0, j], sem).start()
        return carry

    def drain(j, carry):
        _row_copy(h_ref, lax.rem(j, tm), xs_ref, dest_ref[0, 0, j], sem).wait()
        return carry

    lax.fori_loop(0, 2 * tm, issue, 0)
    lax.fori_loop(0, 2 * tm, drain, 0)


def _dispatch(dest3, h, xs_init, tm):
    t, d = h.shape
    return pl.pallas_call(
        functools.partial(_dispatch_kernel, tm=tm),
        grid=(t // tm,),
        in_specs=[
            pl.BlockSpec((1, 1, 2 * tm), lambda i: (i, 0, 0), memory_space=pltpu.SMEM),
            pl.BlockSpec((tm, d), lambda i: (i, 0)),
            pl.BlockSpec(memory_space=pl.ANY),
        ],
        out_specs=pl.BlockSpec(memory_space=pl.ANY),
        out_shape=jax.ShapeDtypeStruct(xs_init.shape, xs_init.dtype),
        scratch_shapes=[pltpu.SemaphoreType.DMA(())],
        input_output_aliases={2: 0},
        compiler_params=_cparams(("arbitrary",)),
        name="moe_dispatch",
    )(dest3, h, xs_init)


def _combine_kernel(dest_ref, ys_ref, x_ref, rec_ref, o_ref, buf_ref, sem, *, tm):
    def issue(j, carry):
        _row_copy(ys_ref, dest_ref[0, 0, j], buf_ref, j, sem).start()
        return carry

    def drain(j, carry):
        _row_copy(ys_ref, dest_ref[0, 0, j], buf_ref, j, sem).wait()
        return carry

    lax.fori_loop(0, 2 * tm, issue, 0)
    lax.fori_loop(0, 2 * tm, drain, 0)
    rec = rec_ref[...]
    g1 = rec[:, 4:5]
    g2 = rec[:, 5:6]
    o_ref[...] = x_ref[...] + (g1 * buf_ref[0:tm, :] + g2 * buf_ref[tm:2 * tm, :])


def _combine(dest3, ys, x2d, rec, tm):
    t, d = x2d.shape
    return pl.pallas_call(
        functools.partial(_combine_kernel, tm=tm),
        grid=(t // tm,),
        in_specs=[
            pl.BlockSpec((1, 1, 2 * tm), lambda i: (i, 0, 0), memory_space=pltpu.SMEM),
            pl.BlockSpec(memory_space=pl.ANY),
            pl.BlockSpec((tm, d), lambda i: (i, 0)),
            pl.BlockSpec((tm, ROUTER_LANES), lambda i: (i, 0)),
        ],
        out_specs=pl.BlockSpec((tm, d), lambda i: (i, 0)),
        out_shape=jax.ShapeDtypeStruct((t, d), F32),
        scratch_shapes=[pltpu.VMEM((2 * tm, d), F32), pltpu.SemaphoreType.DMA(())],
        compiler_params=_cparams(("arbitrary",)),
        name="moe_combine",
    )(dest3, ys, x2d, rec)


def _pad_heads_t(w_cols, offsets):
    d = w_cols.shape[0]
    nh = w_cols.shape[1] // HEAD_DIM
    wt = w_cols.T.reshape(nh, HEAD_DIM, d)
    hi = jnp.asarray(offsets, I32).reshape(nh, 1, 1) > 0
    z = jnp.zeros_like(wt)
    slab = jnp.concatenate([jnp.where(hi, z, wt), jnp.where(hi, wt, z)], axis=1)
    return slab.reshape(nh * HEAD_SLAB, d)


def _head_block_ones(width):
    idx = jnp.arange(width) // HEAD_DIM
    return (idx[:, None] == idx[None, :]).astype(F32)


def _tile_gain(g, n_heads):
    return jnp.tile(g.astype(F32), n_heads).reshape(1, n_heads * HEAD_DIM)


def _s5_params(lam_re, lam_im, log_dt, b_re, b_im, c_re, c_im):
    f32 = F32
    g, p = lam_re.shape
    n = b_re.shape[2]
    lam = lax.complex(lam_re.astype(f32), lam_im.astype(f32))
    dt = jnp.exp(log_dt.astype(f32))[:, None]
    lam_bar = jnp.exp(lam * dt)
    b_bar = ((lam_bar - 1.0) / lam)[..., None] * lax.complex(b_re.astype(f32), b_im.astype(f32))
    ns = g // S5_GROUPS_PER_SLAB
    eye = jnp.eye(S5_GROUPS_PER_SLAB, dtype=f32)

    def in_map(part):
        blk = part.reshape(ns, S5_GROUPS_PER_SLAB, p, n).transpose(0, 1, 3, 2)
        return jnp.einsum('sgnp,gh->sgnhp', blk, eye).reshape(
            ns, S5_GROUPS_PER_SLAB * n, S5_GROUPS_PER_SLAB * p)

    def out_map(part):
        blk = part.reshape(ns, S5_GROUPS_PER_SLAB, n, p).transpose(0, 1, 3, 2)
        return jnp.einsum('sgpn,gh->sgphn', blk, eye).reshape(
            ns, S5_GROUPS_PER_SLAB * p, S5_GROUPS_PER_SLAB * n)

    bd = jnp.concatenate([in_map(jnp.real(b_bar)), in_map(jnp.imag(b_bar))], axis=2)
    cd = jnp.concatenate([out_map(c_re.astype(f32)), out_map(-c_im.astype(f32))], axis=1)
    ar = jnp.real(lam_bar).reshape(ns, S5_GROUPS_PER_SLAB * p)
    ai = jnp.imag(lam_bar).reshape(ns, S5_GROUPS_PER_SLAB * p)
    return bd.astype(BF16), cd.astype(BF16), ar, ai


def kernel(x, mem, l0_mix_norm, l0_w_in, l0_s5_lam_re, l0_s5_lam_im, l0_s5_log_dt, l0_s5_b_re, l0_s5_b_im, l0_s5_c_re, l0_s5_c_im, l0_s5_d, l0_s5_w_glu, l0_s5_b_glu, l0_mem_norm, l0_w_mem_k, l0_w_mem_v, l0_mem_q_gain, l0_mem_k_gain, l0_w_out, l0_ffn_norm, l0_ffn_w_gate, l0_ffn_w_up, l0_ffn_w_down, kv_norm, kv_w_k, kv_w_v, kv_k_gain, l1_mix_norm, l1_w_in, l1_moba_q_gain, l1_mem_norm, l1_w_mem_k, l1_w_mem_v, l1_mem_q_gain, l1_mem_k_gain, l1_w_out, l1_ffn_norm, l1_moe_router, l1_moe_w_gate, l1_moe_w_up, l1_moe_w_down):
    batch, seq, d = x.shape
    t = batch * seq
    main_w = l0_s5_w_glu.shape[0]
    mem_w = l0_w_mem_k.shape[1]
    n_q = main_w // HEAD_DIM
    tm = 512
    row = lambda v: v.astype(F32).reshape(1, -1)

    mem_off = [HEAD_DIM * (h % 2) for h in range(MEM_HEADS)]
    moba_off = [HEAD_DIM * ((h // MOBA_GROUP) % 2) for h in range(n_q)]
    ones_kv = _head_block_ones(mem_w)
    x2d = x.reshape(t, d)

    main_in, qm0_t = _inproj0(
        x2d, row(l0_mix_norm), l0_w_in[:, :main_w].astype(BF16),
        _pad_heads_t(l0_w_in[:, main_w:], mem_off).astype(BF16), batch, seq, tm)
    k0, v0_t = _mem_kv(
        mem, row(l0_mem_norm), l0_w_mem_k.astype(BF16), l0_w_mem_v.T.astype(BF16), ones_kv,
        _tile_gain(l0_mem_k_gain, MEM_HEADS) * _tile_gain(l0_mem_q_gain, MEM_HEADS))
    mem0_t = _mem_attn(qm0_t, k0, v0_t, tm)

    bd, cd, ar, ai = _s5_params(l0_s5_lam_re, l0_s5_lam_im, l0_s5_log_dt, l0_s5_b_re, l0_s5_b_im,
                                l0_s5_c_re, l0_s5_c_im)
    u_tm = main_in.reshape(batch, seq, main_w).transpose(1, 0, 2).reshape(t, main_w)
    s5_tm = _s5(u_tm, bd, cd, ar, ai, row(l0_s5_d), l0_s5_w_glu.astype(BF16), row(l0_s5_b_glu),
                batch, 64)
    s5_out = s5_tm.reshape(seq, batch, main_w).transpose(1, 0, 2).reshape(t, main_w)

    x1, h1 = _outproj0(s5_out, mem0_t, l0_w_out[:main_w].astype(BF16),
                       l0_w_out[main_w:].astype(BF16), x2d, row(l0_ffn_norm), seq, tm)
    n_tiles = t // tm
    x2 = _ffn(jnp.zeros((n_tiles,), I32), jnp.full((1,), n_tiles, I32), h1, x1,
              l0_ffn_w_gate.astype(BF16)[None], l0_ffn_w_up.astype(BF16)[None],
              l0_ffn_w_down.astype(BF16)[None], tm, l0_ffn_w_gate.shape[1] // 2)

    wq1_t = jnp.concatenate([_pad_heads_t(l1_w_in[:, :main_w], moba_off),
                             _pad_heads_t(l1_w_in[:, main_w:], mem_off)], axis=0).astype(BF16)
    q_t, qm1_t, kq, km, v_t5 = _proj1(
        x2, row(l1_mix_norm), row(kv_norm), wq1_t, kv_w_k.astype(BF16), kv_w_v.T.astype(BF16),
        ones_kv, _tile_gain(kv_k_gain, MOBA_KV_HEADS), _tile_gain(l1_moba_q_gain, MOBA_KV_HEADS),
        batch, seq, tm)
    nb = seq // MOBA_BLOCK
    moba_t = _moba(q_t, kq.reshape(batch, seq, -1), v_t5, km.reshape(batch, nb, -1))
    k1, v1_t = _mem_kv(
        mem, row(l1_mem_norm), l1_w_mem_k.astype(BF16), l1_w_mem_v.T.astype(BF16), ones_kv,
        _tile_gain(l1_mem_k_gain, MEM_HEADS) * _tile_gain(l1_mem_q_gain, MEM_HEADS))
    mem1_t = _mem_attn(qm1_t, k1, v1_t, tm)

    wr = jnp.zeros((d, ROUTER_LANES), F32).at[:, :N_EXPERTS].set(l1_moe_router.astype(F32))
    x3, h3, rec, cnt = _outproj1(moba_t, mem1_t, l1_w_out[:main_w].astype(BF16),
                                 l1_w_out[main_w:].astype(BF16), x2, row(l1_ffn_norm), wr, seq, tm)

    tme = 512
    counts = cnt[0, :N_EXPERTS].astype(I32)
    padded = ((counts + tme - 1) // tme) * tme
    ends = jnp.cumsum(padded)
    starts = ends - padded
    max_tiles = (2 * t) // tme + N_EXPERTS
    tile_expert = jnp.minimum(
        jnp.searchsorted(ends // tme, jnp.arange(max_tiles, dtype=I32), side='right'),
        N_EXPERTS - 1).astype(I32)
    n_used = (ends[-1] // tme).astype(I32).reshape(1)
    e1 = rec[:, 0].astype(I32)
    e2 = rec[:, 1].astype(I32)
    d1 = starts[e1] + rec[:, 2].astype(I32)
    d2 = starts[e2] + rec[:, 3].astype(I32)
    tmd = 256
    dest3 = jnp.concatenate([d1.reshape(t // tmd, 1, tmd), d2.reshape(t // tmd, 1, tmd)], axis=2)

    xs = _dispatch(dest3, h3, jnp.zeros((max_tiles * tme, d), F32), tmd)
    ff = l1_moe_w_gate.shape[2]
    ys = _ffn(tile_expert, n_used, xs, None, l1_moe_w_gate.astype(BF16),
              l1_moe_w_up.astype(BF16), l1_moe_w_down.astype(BF16), tme, ff // 4)
    out = _combine(dest3, ys, x3, rec, tmd)
    return out.reshape(batch, seq, d)
```

```python
import functools

import jax
import jax.numpy as jnp
from jax import lax
from jax.experimental import pallas as pl
from jax.experimental.pallas import tpu as pltpu

F32 = jnp.float32
BF16 = jnp.bfloat16
I32 = jnp.int32

RMS_EPS = 1e-6
NEG_INF = -1e30
HEAD_DIM = 64
HEAD_SLAB = 128
MEM_HEADS = 4
MOBA_KV_HEADS = 4
MOBA_GROUP = 3
MOBA_BLOCK = 256
MOBA_TOP_BLOCKS = 3
S5_GROUP_DIM = 16
S5_STATE = 64
S5_GROUPS_PER_SLAB = HEAD_SLAB // S5_GROUP_DIM
S5_SLAB_STATES = S5_GROUPS_PER_SLAB * S5_STATE
N_EXPERTS = 8
ROUTER_LANES = 128
SUBLANES = 8
DMA_UNROLL = 8

VMEM_LIMIT_BYTES = 56 * 1024 * 1024

_NT = (((1,), (1,)), ((), ()))
_TN = (((0,), (0,)), ((), ()))


def _cparams(sem):
    return pltpu.CompilerParams(dimension_semantics=sem, vmem_limit_bytes=VMEM_LIMIT_BYTES)


def _rms(x, g):
    return x * lax.rsqrt(jnp.mean(x * x, axis=-1, keepdims=True) + RMS_EPS) * g


def _slab_norm_t(qt, n_heads, scale):
    outs = []
    for h in range(n_heads):
        s = qt[HEAD_SLAB * h:HEAD_SLAB * (h + 1), :]
        ms = jnp.sum(s * s, axis=0, keepdims=True) * (1.0 / HEAD_DIM)
        outs.append(s * (lax.rsqrt(ms + RMS_EPS) * scale))
    return outs


def _inproj0_kernel(x_ref, g_ref, wm_ref, wqt_ref, main_ref, qt_ref):
    hb = _rms(x_ref[...], g_ref[...]).astype(BF16)
    main_ref[...] = jnp.dot(hb, wm_ref[...], preferred_element_type=F32)
    qt = lax.dot_general(wqt_ref[...], hb, _NT, preferred_element_type=F32)
    for h, s in enumerate(_slab_norm_t(qt, MEM_HEADS, HEAD_DIM ** -0.5)):
        qt_ref[0, HEAD_SLAB * h:HEAD_SLAB * (h + 1), :] = s.astype(BF16)


def _inproj0(x2d, g, wm, wqt, batch, seq, tm):
    t, d = x2d.shape
    per_b = seq // tm
    nm = wm.shape[1]
    nq = wqt.shape[0]
    return pl.pallas_call(
        _inproj0_kernel,
        grid=(t // tm,),
        in_specs=[
            pl.BlockSpec((tm, d), lambda i: (i, 0)),
            pl.BlockSpec((1, d), lambda i: (0, 0)),
            pl.BlockSpec((d, nm), lambda i: (0, 0)),
            pl.BlockSpec((nq, d), lambda i: (0, 0)),
        ],
        out_specs=[
            pl.BlockSpec((tm, nm), lambda i: (i, 0)),
            pl.BlockSpec((1, nq, tm), lambda i: (i // per_b, 0, i % per_b)),
        ],
        out_shape=[
            jax.ShapeDtypeStruct((t, nm), F32),
            jax.ShapeDtypeStruct((batch, nq, seq), BF16),
        ],
        compiler_params=_cparams(("parallel",)),
        name="inproj0",
    )(x2d, g, wm, wqt)


def _mem_kv_kernel(mem_ref, g_ref, wk_ref, wvt_ref, ones_ref, kg_ref, k_ref, vt_ref):
    mb = _rms(mem_ref[0], g_ref[...]).astype(BF16)
    k = jnp.dot(mb, wk_ref[...], preferred_element_type=F32)
    ms = jnp.dot(k * k, ones_ref[...], preferred_element_type=F32,
                 precision=lax.Precision.HIGHEST) * (1.0 / HEAD_DIM)
    k_ref[0] = (k * lax.rsqrt(ms + RMS_EPS) * kg_ref[...]).astype(BF16)
    vt_ref[0] = lax.dot_general(wvt_ref[...], mb, _NT, preferred_element_type=F32).astype(BF16)


def _mem_kv(mem, g, wk, wvt, ones, kg):
    b, m, d = mem.shape
    w = wk.shape[1]
    return pl.pallas_call(
        _mem_kv_kernel,
        grid=(b,),
        in_specs=[
            pl.BlockSpec((1, m, d), lambda i: (i, 0, 0)),
            pl.BlockSpec((1, d), lambda i: (0, 0)),
            pl.BlockSpec((d, w), lambda i: (0, 0)),
            pl.BlockSpec((w, d), lambda i: (0, 0)),
            pl.BlockSpec((w, w), lambda i: (0, 0)),
            pl.BlockSpec((1, w), lambda i: (0, 0)),
        ],
        out_specs=[
            pl.BlockSpec((1, m, w), lambda i: (i, 0, 0)),
            pl.BlockSpec((1, w, m), lambda i: (i, 0, 0)),
        ],
        out_shape=[
            jax.ShapeDtypeStruct((b, m, w), BF16),
            jax.ShapeDtypeStruct((b, w, m), BF16),
        ],
        compiler_params=_cparams(("parallel",)),
        name="mem_kv",
    )(mem, g, wk, wvt, ones, kg)


def _mem_attn_kernel(qt_ref, k_ref, vt_ref, o_ref):
    for h in range(MEM_HEADS):
        pair = h // 2
        k2 = k_ref[0, :, HEAD_SLAB * pair:HEAD_SLAB * (pair + 1)]
        q = qt_ref[0, HEAD_SLAB * h:HEAD_SLAB * (h + 1), :]
        s = jnp.dot(k2, q, preferred_element_type=F32)
        m = jnp.max(s, axis=0, keepdims=True)
        p = jnp.exp(s - m)
        l = jnp.sum(p, axis=0, keepdims=True)
        v = vt_ref[0, HEAD_DIM * h:HEAD_DIM * (h + 1), :]
        o = jnp.dot(v, p.astype(BF16), preferred_element_type=F32)
        o_ref[0, HEAD_DIM * h:HEAD_DIM * (h + 1), :] = (o / l).astype(BF16)


def _mem_attn(qt, k, vt, tq):
    b, nq, s = qt.shape
    m, w = k.shape[1], k.shape[2]
    return pl.pallas_call(
        _mem_attn_kernel,
        grid=(b, s // tq),
        in_specs=[
            pl.BlockSpec((1, nq, tq), lambda i, j: (i, 0, j)),
            pl.BlockSpec((1, m, w), lambda i, j: (i, 0, 0)),
            pl.BlockSpec((1, w, m), lambda i, j: (i, 0, 0)),
        ],
        out_specs=pl.BlockSpec((1, w, tq), lambda i, j: (i, 0, j)),
        out_shape=jax.ShapeDtypeStruct((b, w, s), BF16),
        compiler_params=_cparams(("parallel", "parallel")),
        name="mem_attn",
    )(qt, k, vt)


def _s5_kernel(u_ref, bd_ref, cd_ref, ar_ref, ai_ref, d_ref, wglu_ref, bglu_ref, o_ref,
               buf_ref, st_ref, *, ts, n_slabs):
    half = S5_SLAB_STATES
    width = 2 * half

    @pl.when(pl.program_id(0) == 0)
    def _():
        st_ref[...] = jnp.zeros_like(st_ref)

    u = u_ref[...]
    ub = u.astype(BF16)
    for j in range(n_slabs):
        buf_ref[:, width * j:width * (j + 1)] = jnp.dot(
            ub[:, HEAD_SLAB * j:HEAD_SLAB * (j + 1)], bd_ref[j], preferred_element_type=F32)

    for j in range(n_slabs):
        re = slice(width * j, width * j + half)
        im = slice(width * j + half, width * (j + 1))
        ar = jnp.broadcast_to(ar_ref[j:j + 1, :], (SUBLANES, half))
        ai = jnp.broadcast_to(ai_ref[j:j + 1, :], (SUBLANES, half))

        def step(t, carry, re=re, im=im, ar=ar, ai=ai):
            xr, xi = carry
            rows = pl.ds(pl.multiple_of(t * SUBLANES, SUBLANES), SUBLANES)
            nxr = ar * xr - ai * xi + buf_ref[rows, re]
            nxi = ar * xi + ai * xr + buf_ref[rows, im]
            buf_ref[rows, re] = nxr
            buf_ref[rows, im] = nxi
            return nxr, nxi

        xr, xi = lax.fori_loop(0, ts, step, (st_ref[:, re], st_ref[:, im]), unroll=8)
        st_ref[:, re] = xr
        st_ref[:, im] = xi

    ys = []
    for j in range(n_slabs):
        ys.append(jnp.dot(buf_ref[:, width * j:width * (j + 1)].astype(BF16), cd_ref[j],
                          preferred_element_type=F32))
    y = jnp.concatenate(ys, axis=1) + d_ref[...] * u
    g = jax.nn.gelu(y)
    z = jnp.dot(g.astype(BF16), wglu_ref[...], preferred_element_type=F32) + bglu_ref[...]
    o_ref[...] = (g / (1.0 + jnp.exp(-z))).astype(BF16)


def _s5(u_tm, bd, cd, ar, ai, dskip, wglu, bglu, batch, ts):
    rows, c = u_tm.shape
    n_slabs = c // HEAD_SLAB
    width = 2 * S5_SLAB_STATES
    tile = ts * batch
    return pl.pallas_call(
        functools.partial(_s5_kernel, ts=ts, n_slabs=n_slabs),
        grid=(rows // tile,),
        in_specs=[
            pl.BlockSpec((tile, c), lambda i: (i, 0)),
            pl.BlockSpec((n_slabs, HEAD_SLAB, width), lambda i: (0, 0, 0)),
            pl.BlockSpec((n_slabs, width, HEAD_SLAB), lambda i: (0, 0, 0)),
            pl.BlockSpec((n_slabs, S5_SLAB_STATES), lambda i: (0, 0)),
            pl.BlockSpec((n_slabs, S5_SLAB_STATES), lambda i: (0, 0)),
            pl.BlockSpec((1, c), lambda i: (0, 0)),
            pl.BlockSpec((c, c), lambda i: (0, 0)),
            pl.BlockSpec((1, c), lambda i: (0, 0)),
        ],
        out_specs=pl.BlockSpec((tile, c), lambda i: (i, 0)),
        out_shape=jax.ShapeDtypeStruct((rows, c), BF16),
        scratch_shapes=[
            pltpu.VMEM((tile, n_slabs * width), F32),
            pltpu.VMEM((batch, n_slabs * width), F32),
        ],
        compiler_params=_cparams(("arbitrary",)),
        name="s5_mixer",
    )(u_tm, bd, cd, ar, ai, dskip, wglu, bglu)


def _outproj_body(a, a_transposed, bt_ref, wa_ref, wb_ref, x_ref, g_ref):
    dn = _TN if a_transposed else (((1,), (0,)), ((), ()))
    y = lax.dot_general(a, wa_ref[...], dn, preferred_element_type=F32)
    y = y + lax.dot_general(bt_ref[0], wb_ref[...], _TN, preferred_element_type=F32)
    xn = x_ref[...] + y
    return xn, _rms(xn, g_ref[...])


def _outproj0_kernel(a_ref, bt_ref, wa_ref, wb_ref, x_ref, g_ref, xo_ref, h_ref):
    xn, h = _outproj_body(a_ref[...], False, bt_ref, wa_ref, wb_ref, x_ref, g_ref)
    xo_ref[...] = xn
    h_ref[...] = h.astype(BF16)


def _outproj0(a, bt, wa, wb, x2d, g, seq, tm):
    t, d = x2d.shape
    per_b = seq // tm
    ka, kb = a.shape[1], bt.shape[1]
    return pl.pallas_call(
        _outproj0_kernel,
        grid=(t // tm,),
        in_specs=[
            pl.BlockSpec((tm, ka), lambda i: (i, 0)),
            pl.BlockSpec((1, kb, tm), lambda i: (i // per_b, 0, i % per_b)),
            pl.BlockSpec((ka, d), lambda i: (0, 0)),
            pl.BlockSpec((kb, d), lambda i: (0, 0)),
            pl.BlockSpec((tm, d), lambda i: (i, 0)),
            pl.BlockSpec((1, d), lambda i: (0, 0)),
        ],
        out_specs=[
            pl.BlockSpec((tm, d), lambda i: (i, 0)),
            pl.BlockSpec((tm, d), lambda i: (i, 0)),
        ],
        out_shape=[
            jax.ShapeDtypeStruct((t, d), F32),
            jax.ShapeDtypeStruct((t, d), BF16),
        ],
        compiler_params=_cparams(("parallel",)),
        name="outproj0",
    )(a, bt, wa, wb, x2d, g)


def _outproj1_kernel(at_ref, bt_ref, wa_ref, wb_ref, x_ref, g_ref, wr_ref,
                     xo_ref, h_ref, rec_ref, cnt_ref, run_ref, *, tm):
    i = pl.program_id(0)

    @pl.when(i == 0)
    def _():
        run_ref[...] = jnp.zeros_like(run_ref)

    xn, h = _outproj_body(at_ref[0], True, bt_ref, wa_ref, wb_ref, x_ref, g_ref)
    xo_ref[...] = xn
    h_ref[...] = h

    lane = lax.broadcasted_iota(I32, (tm, ROUTER_LANES), 1)
    logits = jnp.dot(h, wr_ref[...], preferred_element_type=F32, precision=lax.Precision.HIGHEST)
    logits = jnp.where(lane < N_EXPERTS, logits, -jnp.inf)
    m1 = jnp.max(logits, axis=-1, keepdims=True)
    e1 = jnp.min(jnp.where(logits == m1, lane, ROUTER_LANES), axis=-1, keepdims=True)
    rest = jnp.where(lane == e1, -jnp.inf, logits)
    m2 = jnp.max(rest, axis=-1, keepdims=True)
    e2 = jnp.min(jnp.where(rest == m2, lane, ROUTER_LANES), axis=-1, keepdims=True)
    w2 = jnp.exp(m2 - m1)
    g1 = 1.0 / (1.0 + w2)
    g2 = w2 / (1.0 + w2)

    hot = jnp.where((lane == e1) | (lane == e2), 1.0, 0.0)
    r_io = lax.broadcasted_iota(I32, (tm, tm), 0)
    c_io = lax.broadcasted_iota(I32, (tm, tm), 1)
    tril = jnp.where(c_io < r_io, 1.0, 0.0).astype(BF16)
    before = jnp.dot(tril, hot.astype(BF16), preferred_element_type=F32) + run_ref[...]
    r1 = jnp.sum(jnp.where(lane == e1, before, 0.0), axis=-1, keepdims=True)
    r2 = jnp.sum(jnp.where(lane == e2, before, 0.0), axis=-1, keepdims=True)
    run = run_ref[...] + jnp.sum(hot, axis=0, keepdims=True)
    run_ref[...] = run
    cnt_ref[...] = jnp.broadcast_to(run, cnt_ref.shape)

    rec = jnp.where(lane == 0, e1.astype(F32), 0.0)
    rec = jnp.where(lane == 1, e2.astype(F32), rec)
    rec = jnp.where(lane == 2, r1, rec)
    rec = jnp.where(lane == 3, r2, rec)
    rec = jnp.where(lane == 4, g1, rec)
    rec = jnp.where(lane == 5, g2, rec)
    rec_ref[...] = rec


def _outproj1(at, bt, wa, wb, x2d, g, wr, seq, tm):
    t, d = x2d.shape
    per_b = seq // tm
    ka, kb = at.shape[1], bt.shape[1]
    return pl.pallas_call(
        functools.partial(_outproj1_kernel, tm=tm),
        grid=(t // tm,),
        in_specs=[
            pl.BlockSpec((1, ka, tm), lambda i: (i // per_b, 0, i % per_b)),
            pl.BlockSpec((1, kb, tm), lambda i: (i // per_b, 0, i % per_b)),
            pl.BlockSpec((ka, d), lambda i: (0, 0)),
            pl.BlockSpec((kb, d), lambda i: (0, 0)),
            pl.BlockSpec((tm, d), lambda i: (i, 0)),
            pl.BlockSpec((1, d), lambda i: (0, 0)),
            pl.BlockSpec((d, ROUTER_LANES), lambda i: (0, 0)),
        ],
        out_specs=[
            pl.BlockSpec((tm, d), lambda i: (i, 0)),
            pl.BlockSpec((tm, d), lambda i: (i, 0)),
            pl.BlockSpec((tm, ROUTER_LANES), lambda i: (i, 0)),
            pl.BlockSpec((SUBLANES, ROUTER_LANES), lambda i: (0, 0)),
        ],
        out_shape=[
            jax.ShapeDtypeStruct((t, d), F32),
            jax.ShapeDtypeStruct((t, d), F32),
            jax.ShapeDtypeStruct((t, ROUTER_LANES), F32),
            jax.ShapeDtypeStruct((SUBLANES, ROUTER_LANES), F32),
        ],
        scratch_shapes=[pltpu.VMEM((1, ROUTER_LANES), F32)],
        compiler_params=_cparams(("arbitrary",)),
        name="outproj1_router",
    )(at, bt, wa, wb, x2d, g, wr)


def _ffn_kernel(te_ref, nu_ref, *refs, has_res, cast_in):
    if has_res:
        xs_ref, res_ref, wg_ref, wu_ref, wd_ref, o_ref, acc_ref, xb_ref = refs
    else:
        xs_ref, wg_ref, wu_ref, wd_ref, o_ref, acc_ref, xb_ref = refs
        res_ref = None
    i = pl.program_id(0)
    f = pl.program_id(1)
    nf = pl.num_programs(1)

    @pl.when(i < nu_ref[0])
    def _():
        @pl.when(f == 0)
        def _():
            acc_ref[...] = jnp.zeros_like(acc_ref)
            xb_ref[...] = xs_ref[...].astype(BF16) if cast_in else xs_ref[...]

        xb = xb_ref[...]
        gt = jnp.dot(xb, wg_ref[0], preferred_element_type=F32)
        up = jnp.dot(xb, wu_ref[0], preferred_element_type=F32)
        mid = (gt / (1.0 + jnp.exp(-gt))) * up
        acc_ref[...] += jnp.dot(mid.astype(BF16), wd_ref[0], preferred_element_type=F32)

        @pl.when(f == nf - 1)
        def _():
            o_ref[...] = acc_ref[...] + res_ref[...] if has_res else acc_ref[...]

    @pl.when((i >= nu_ref[0]) & (f == nf - 1))
    def _():
        o_ref[...] = jnp.zeros_like(o_ref)


def _ffn(tile_expert, n_used, xs, res, wg, wu, wd, tm, tf):
    rows, d = xs.shape
    ff = wg.shape[2]
    nf = ff // tf
    has_res = res is not None
    cast_in = xs.dtype != BF16

    def row_map(i, f, te, nu):
        return (jnp.minimum(i, nu[0] - 1), 0)

    def f_of(i, f, nu):
        return jnp.where(i < nu[0], f, nf - 1)

    def wcol_map(i, f, te, nu):
        return (te[jnp.minimum(i, nu[0] - 1)], 0, f_of(i, f, nu))

    def wrow_map(i, f, te, nu):
        return (te[jnp.minimum(i, nu[0] - 1)], f_of(i, f, nu), 0)

    in_specs = [pl.BlockSpec((tm, d), row_map)]
    args = [xs]
    if has_res:
        in_specs.append(pl.BlockSpec((tm, d), row_map))
        args.append(res)
    in_specs += [
        pl.BlockSpec((1, d, tf), wcol_map),
        pl.BlockSpec((1, d, tf), wcol_map),
        pl.BlockSpec((1, tf, d), wrow_map),
    ]
    args += [wg, wu, wd]
    grid_spec = pltpu.PrefetchScalarGridSpec(
        num_scalar_prefetch=2,
        grid=(rows // tm, nf),
        in_specs=in_specs,
        out_specs=pl.BlockSpec((tm, d), lambda i, f, te, nu: (i, 0)),
        scratch_shapes=[pltpu.VMEM((tm, d), F32), pltpu.VMEM((tm, d), BF16)],
    )
    return pl.pallas_call(
        functools.partial(_ffn_kernel, has_res=has_res, cast_in=cast_in),
        grid_spec=grid_spec,
        out_shape=jax.ShapeDtypeStruct((rows, d), F32),
        compiler_params=_cparams(("arbitrary", "arbitrary")),
        name="ffn_res" if has_res else "ffn_moe",
    )(tile_expert, n_used, *args)


def _proj1_kernel(x_ref, g1_ref, g2_ref, wqt_ref, wk_ref, wvt_ref, ones_ref, kg_ref, qg_ref,
                  qt_ref, qmt_ref, k_ref, km_ref, vt_ref, *, tm, n_q):
    x = x_ref[...]
    h1 = _rms(x, g1_ref[...]).astype(BF16)
    h2 = _rms(x, g2_ref[...]).astype(BF16)
    qt = lax.dot_general(wqt_ref[...], h1, _NT, preferred_element_type=F32)
    slabs = _slab_norm_t(qt, n_q + MEM_HEADS, HEAD_DIM ** -0.5)
    for h in range(n_q):
        qt_ref[0, HEAD_SLAB * h:HEAD_SLAB * (h + 1), :] = slabs[h].astype(BF16)
    for h in range(MEM_HEADS):
        qmt_ref[0, HEAD_SLAB * h:HEAD_SLAB * (h + 1), :] = slabs[n_q + h].astype(BF16)

    k = jnp.dot(h2, wk_ref[...], preferred_element_type=F32)
    ms = jnp.dot(k * k, ones_ref[...], preferred_element_type=F32,
                 precision=lax.Precision.HIGHEST) * (1.0 / HEAD_DIM)
    kn = k * lax.rsqrt(ms + RMS_EPS) * kg_ref[...]
    kq = kn * qg_ref[...]
    k_ref[...] = kq.astype(BF16)
    nblk = tm // MOBA_BLOCK
    for j in range(nblk):
        km_ref[j] = jnp.mean(kn[MOBA_BLOCK * j:MOBA_BLOCK * (j + 1), :], axis=0,
                             keepdims=True) * qg_ref[...]
    vt_ref[0] = lax.dot_general(wvt_ref[...], h2, _NT, preferred_element_type=F32).astype(BF16)


def _proj1(x2d, g1, g2, wqt, wk, wvt, ones, kg, qg, batch, seq, tm):
    t, d = x2d.shape
    per_b = seq // tm
    nq_rows = wqt.shape[0]
    n_q = nq_rows // HEAD_SLAB - MEM_HEADS
    kw = wk.shape[1]
    nblk = tm // MOBA_BLOCK
    return pl.pallas_call(
        functools.partial(_proj1_kernel, tm=tm, n_q=n_q),
        grid=(t // tm,),
        in_specs=[
            pl.BlockSpec((tm, d), lambda i: (i, 0)),
            pl.BlockSpec((1, d), lambda i: (0, 0)),
            pl.BlockSpec((1, d), lambda i: (0, 0)),
            pl.BlockSpec((nq_rows, d), lambda i: (0, 0)),
            pl.BlockSpec((d, kw), lambda i: (0, 0)),
            pl.BlockSpec((kw, d), lambda i: (0, 0)),
            pl.BlockSpec((kw, kw), lambda i: (0, 0)),
            pl.BlockSpec((1, kw), lambda i: (0, 0)),
            pl.BlockSpec((1, kw), lambda i: (0, 0)),
        ],
        out_specs=[
            pl.BlockSpec((1, n_q * HEAD_SLAB, tm), lambda i: (i // per_b, 0, i % per_b)),
            pl.BlockSpec((1, MEM_HEADS * HEAD_SLAB, tm), lambda i: (i // per_b, 0, i % per_b)),
            pl.BlockSpec((tm, kw), lambda i: (i, 0)),
            pl.BlockSpec((nblk, 1, kw), lambda i: (i, 0, 0)),
            pl.BlockSpec((1, kw, tm), lambda i: (i // per_b, 0, i % per_b)),
        ],
        out_shape=[
            jax.ShapeDtypeStruct((batch, n_q * HEAD_SLAB, seq), BF16),
            jax.ShapeDtypeStruct((batch, MEM_HEADS * HEAD_SLAB, seq), BF16),
            jax.ShapeDtypeStruct((t, kw), BF16),
            jax.ShapeDtypeStruct((t // MOBA_BLOCK, 1, kw), F32),
            jax.ShapeDtypeStruct((batch, kw, seq), BF16),
        ],
        compiler_params=_cparams(("parallel",)),
        name="proj1",
    )(x2d, g1, g2, wqt, wk, wvt, ones, kg, qg)


def _moba_kernel(qt_ref, k_ref, vt_ref, km_ref, o_ref):
    qb = pl.program_id(2)
    nb = km_ref.shape[1]
    bq = MOBA_BLOCK
    nq = MOBA_GROUP * bq
    q3 = jnp.concatenate(
        [qt_ref[0, HEAD_SLAB * g:HEAD_SLAB * (g + 1), :] for g in range(MOBA_GROUP)], axis=1)

    blk = lax.broadcasted_iota(I32, (nb, nq), 0)
    gate = jnp.dot(km_ref[0], q3.astype(F32), preferred_element_type=F32,
                   precision=lax.Precision.HIGHEST)
    gate = jnp.where(blk < qb, gate, NEG_INF)
    rank = jnp.zeros((nb, nq), F32)
    for m in range(nb):
        gm = gate[m:m + 1, :]
        rank = rank + jnp.where(gm > gate, 1.0, jnp.where((gm == gate) & (blk > m), 1.0, 0.0))
    sel = jnp.where((blk < qb) & (rank < MOBA_TOP_BLOCKS), 1.0, 0.0)

    key_pos = lax.broadcasted_iota(I32, (bq, nq), 0)
    qry_pos = lax.broadcasted_iota(I32, (bq, nq), 1) & (bq - 1)
    causal = key_pos <= qry_pos

    for n_blocks in range(1, nb + 1):
        @pl.when(qb == n_blocks - 1)
        def _(n_blocks=n_blocks):
            scores = []
            for n in range(n_blocks):
                s = jnp.dot(k_ref[0, bq * n:bq * (n + 1), :], q3, preferred_element_type=F32)
                keep = causal if n == n_blocks - 1 else sel[n:n + 1, :] > 0.0
                scores.append(jnp.where(keep, s, NEG_INF))
            mx = jnp.max(scores[0], axis=0, keepdims=True)
            for s in scores[1:]:
                mx = jnp.maximum(mx, jnp.max(s, axis=0, keepdims=True))
            den = jnp.zeros((1, nq), F32)
            acc = jnp.zeros((HEAD_DIM, nq), F32)
            for n, s in enumerate(scores):
                p = jnp.exp(s - mx)
                den = den + jnp.sum(p, axis=0, keepdims=True)
                acc = acc + jnp.dot(vt_ref[0, :, bq * n:bq * (n + 1)], p.astype(BF16),
                                    preferred_element_type=F32)
            out = acc / den
            for g in range(MOBA_GROUP):
                o_ref[0, HEAD_DIM * g:HEAD_DIM * (g + 1), :] = (
                    out[:, bq * g:bq * (g + 1)].astype(BF16))


def _moba(qt, k3, vt, km3):
    b, nq_rows, s = qt.shape
    nb = s // MOBA_BLOCK
    rows_out = MOBA_GROUP * HEAD_DIM
    return pl.pallas_call(
        _moba_kernel,
        grid=(b, MOBA_KV_HEADS, nb),
        in_specs=[
            pl.BlockSpec((1, MOBA_GROUP * HEAD_SLAB, MOBA_BLOCK), lambda i, j, q: (i, j, q)),
            pl.BlockSpec((1, s, HEAD_SLAB), lambda i, j, q: (i, 0, j // 2)),
            pl.BlockSpec((1, HEAD_DIM, s), lambda i, j, q: (i, j, 0)),
            pl.BlockSpec((1, nb, HEAD_SLAB), lambda i, j, q: (i, 0, j // 2)),
        ],
        out_specs=pl.BlockSpec((1, rows_out, MOBA_BLOCK), lambda i, j, q: (i, j, q)),
        out_shape=jax.ShapeDtypeStruct((b, MOBA_KV_HEADS * rows_out, s), BF16),
        compiler_params=_cparams(("parallel", "parallel", "arbitrary")),
        name="moba_attn",
    )(qt, k3, vt, km3)


def _row_copy(src_ref, src_row, dst_ref, dst_row, sem):
    return pltpu.make_async_copy(src_ref.at[pl.ds(src_row, 1)], dst_ref.at[pl.ds(dst_row, 1)], sem)


def _for_each_row(n_rows, fn):
    def body(j8, carry):
        for u in range(DMA_UNROLL):
            fn(j8 * DMA_UNROLL + u, u % 2)
        return carry

    lax.fori_loop(0, n_rows // DMA_UNROLL, body, 0)


def _dispatch_kernel(dest_ref, h_ref, init_ref, xs_ref, sem, *, tm):
    del init_ref

    def copy(j, slot):
        return _row_copy(h_ref, j, xs_ref, dest_ref[0, 0, slot * tm + j], sem)

    for slot in range(2):
        _for_each_row(tm, lambda j, lane, slot=slot: copy(j, slot).start(priority=lane))
    for slot in range(2):
        _for_each_row(tm, lambda j, lane, slot=slot: copy(j, slot).wait())


def _dispatch(dest3, h, xs_init, tm):
    t, d = h.shape
    return pl.pallas_call(
        functools.partial(_dispatch_kernel, tm=tm),
        grid=(t // tm,),
        in_specs=[
            pl.BlockSpec((1, 1, 2 * tm), lambda i: (i, 0, 0), memory_space=pltpu.SMEM),
            pl.BlockSpec((tm, d), lambda i: (i, 0)),
            pl.BlockSpec(memory_space=pl.ANY),
        ],
        out_specs=pl.BlockSpec(memory_space=pl.ANY),
        out_shape=jax.ShapeDtypeStruct(xs_init.shape, xs_init.dtype),
        scratch_shapes=[pltpu.SemaphoreType.DMA(())],
        input_output_aliases={2: 0},
        compiler_params=_cparams(("arbitrary",)),
        name="moe_dispatch",
    )(dest3, h, xs_init)


def _combine_kernel(dest_ref, next_ref, ys_ref, x_ref, rec_ref, o_ref, buf_ref, sem, *, tm):
    i = pl.program_id(0)
    cur = lax.rem(i, 2)

    def copy(idx_ref, buf_slot, j):
        return _row_copy(ys_ref, idx_ref[0, 0, j], buf_ref.at[buf_slot], j, sem.at[buf_slot])

    @pl.when(i == 0)
    def _():
        _for_each_row(2 * tm, lambda j, lane: copy(dest_ref, 0, j).start(priority=lane))

    @pl.when(i + 1 < pl.num_programs(0))
    def _():
        _for_each_row(2 * tm, lambda j, lane: copy(next_ref, 1 - cur, j).start(priority=lane))

    _for_each_row(2 * tm, lambda j, lane: copy(dest_ref, cur, j).wait())
    rec = rec_ref[...]
    g1 = rec[:, 4:5]
    g2 = rec[:, 5:6]
    o_ref[...] = x_ref[...] + (g1 * buf_ref[cur, 0:tm, :] + g2 * buf_ref[cur, tm:2 * tm, :])


def _combine(dest3, ys, x2d, rec, tm):
    t, d = x2d.shape
    n = t // tm
    return pl.pallas_call(
        functools.partial(_combine_kernel, tm=tm),
        grid=(n,),
        in_specs=[
            pl.BlockSpec((1, 1, 2 * tm), lambda i: (i, 0, 0), memory_space=pltpu.SMEM),
            pl.BlockSpec((1, 1, 2 * tm), lambda i: (jnp.minimum(i + 1, n - 1), 0, 0),
                         memory_space=pltpu.SMEM),
            pl.BlockSpec(memory_space=pl.ANY),
            pl.BlockSpec((tm, d), lambda i: (i, 0)),
            pl.BlockSpec((tm, ROUTER_LANES), lambda i: (i, 0)),
        ],
        out_specs=pl.BlockSpec((tm, d), lambda i: (i, 0)),
        out_shape=jax.ShapeDtypeStruct((t, d), F32),
        scratch_shapes=[pltpu.VMEM((2, 2 * tm, d), F32), pltpu.SemaphoreType.DMA((2,))],
        compiler_params=_cparams(("arbitrary",)),
        name="moe_combine",
    )(dest3, dest3, ys, x2d, rec)


def _pad_heads_t(w_cols, offsets):
    d = w_cols.shape[0]
    nh = w_cols.shape[1] // HEAD_DIM
    wt = w_cols.T.reshape(nh, HEAD_DIM, d)
    hi = jnp.asarray(offsets, I32).reshape(nh, 1, 1) > 0
    z = jnp.zeros_like(wt)
    slab = jnp.concatenate([jnp.where(hi, z, wt), jnp.where(hi, wt, z)], axis=1)
    return slab.reshape(nh * HEAD_SLAB, d)


def _head_block_ones(width):
    idx = jnp.arange(width) // HEAD_DIM
    return (idx[:, None] == idx[None, :]).astype(F32)


def _tile_gain(g, n_heads):
    return jnp.tile(g.astype(F32), n_heads).reshape(1, n_heads * HEAD_DIM)


def _s5_params(lam_re, lam_im, log_dt, b_re, b_im, c_re, c_im):
    f32 = F32
    g, p = lam_re.shape
    n = b_re.shape[2]
    lam = lax.complex(lam_re.astype(f32), lam_im.astype(f32))
    dt = jnp.exp(log_dt.astype(f32))[:, None]
    lam_bar = jnp.exp(lam * dt)
    b_bar = ((lam_bar - 1.0) / lam)[..., None] * lax.complex(b_re.astype(f32), b_im.astype(f32))
    ns = g // S5_GROUPS_PER_SLAB
    eye = jnp.eye(S5_GROUPS_PER_SLAB, dtype=f32)

    def in_map(part):
        blk = part.reshape(ns, S5_GROUPS_PER_SLAB, p, n).transpose(0, 1, 3, 2)
        return jnp.einsum('sgnp,gh->sgnhp', blk, eye).reshape(
            ns, S5_GROUPS_PER_SLAB * n, S5_GROUPS_PER_SLAB * p)

    def out_map(part):
        blk = part.reshape(ns, S5_GROUPS_PER_SLAB, n, p).transpose(0, 1, 3, 2)
        return jnp.einsum('sgpn,gh->sgphn', blk, eye).reshape(
            ns, S5_GROUPS_PER_SLAB * p, S5_GROUPS_PER_SLAB * n)

    bd = jnp.concatenate([in_map(jnp.real(b_bar)), in_map(jnp.imag(b_bar))], axis=2)
    cd = jnp.concatenate([out_map(c_re.astype(f32)), out_map(-c_im.astype(f32))], axis=1)
    ar = jnp.real(lam_bar).reshape(ns, S5_GROUPS_PER_SLAB * p)
    ai = jnp.imag(lam_bar).reshape(ns, S5_GROUPS_PER_SLAB * p)
    return bd.astype(BF16), cd.astype(BF16), ar, ai


def kernel(x, mem, l0_mix_norm, l0_w_in, l0_s5_lam_re, l0_s5_lam_im, l0_s5_log_dt, l0_s5_b_re, l0_s5_b_im, l0_s5_c_re, l0_s5_c_im, l0_s5_d, l0_s5_w_glu, l0_s5_b_glu, l0_mem_norm, l0_w_mem_k, l0_w_mem_v, l0_mem_q_gain, l0_mem_k_gain, l0_w_out, l0_ffn_norm, l0_ffn_w_gate, l0_ffn_w_up, l0_ffn_w_down, kv_norm, kv_w_k, kv_w_v, kv_k_gain, l1_mix_norm, l1_w_in, l1_moba_q_gain, l1_mem_norm, l1_w_mem_k, l1_w_mem_v, l1_mem_q_gain, l1_mem_k_gain, l1_w_out, l1_ffn_norm, l1_moe_router, l1_moe_w_gate, l1_moe_w_up, l1_moe_w_down):
    batch, seq, d = x.shape
    t = batch * seq
    main_w = l0_s5_w_glu.shape[0]
    mem_w = l0_w_mem_k.shape[1]
    n_q = main_w // HEAD_DIM
    tm = 512
    row = lambda v: v.astype(F32).reshape(1, -1)

    mem_off = [HEAD_DIM * (h % 2) for h in range(MEM_HEADS)]
    moba_off = [HEAD_DIM * ((h // MOBA_GROUP) % 2) for h in range(n_q)]
    ones_kv = _head_block_ones(mem_w)
    x2d = x.reshape(t, d)

    main_in, qm0_t = _inproj0(
        x2d, row(l0_mix_norm), l0_w_in[:, :main_w].astype(BF16),
        _pad_heads_t(l0_w_in[:, main_w:], mem_off).astype(BF16), batch, seq, tm)
    k0, v0_t = _mem_kv(
        mem, row(l0_mem_norm), l0_w_mem_k.astype(BF16), l0_w_mem_v.T.astype(BF16), ones_kv,
        _tile_gain(l0_mem_k_gain, MEM_HEADS) * _tile_gain(l0_mem_q_gain, MEM_HEADS))
    mem0_t = _mem_attn(qm0_t, k0, v0_t, tm)

    bd, cd, ar, ai = _s5_params(l0_s5_lam_re, l0_s5_lam_im, l0_s5_log_dt, l0_s5_b_re, l0_s5_b_im,
                                l0_s5_c_re, l0_s5_c_im)
    u_tm = main_in.reshape(batch, seq, main_w).transpose(1, 0, 2).reshape(t, main_w)
    s5_tm = _s5(u_tm, bd, cd, ar, ai, row(l0_s5_d), l0_s5_w_glu.astype(BF16), row(l0_s5_b_glu),
                batch, 64)
    s5_out = s5_tm.reshape(seq, batch, main_w).transpose(1, 0, 2).reshape(t, main_w)

    x1, h1 = _outproj0(s5_out, mem0_t, l0_w_out[:main_w].astype(BF16),
                       l0_w_out[main_w:].astype(BF16), x2d, row(l0_ffn_norm), seq, tm)
    n_tiles = t // tm
    x2 = _ffn(jnp.zeros((n_tiles,), I32), jnp.full((1,), n_tiles, I32), h1, x1,
              l0_ffn_w_gate.astype(BF16)[None], l0_ffn_w_up.astype(BF16)[None],
              l0_ffn_w_down.astype(BF16)[None], tm, l0_ffn_w_gate.shape[1] // 2)

    wq1_t = jnp.concatenate([_pad_heads_t(l1_w_in[:, :main_w], moba_off),
                             _pad_heads_t(l1_w_in[:, main_w:], mem_off)], axis=0).astype(BF16)
    q_t, qm1_t, kq, km, v_t = _proj1(
        x2, row(l1_mix_norm), row(kv_norm), wq1_t, kv_w_k.astype(BF16), kv_w_v.T.astype(BF16),
        ones_kv, _tile_gain(kv_k_gain, MOBA_KV_HEADS), _tile_gain(l1_moba_q_gain, MOBA_KV_HEADS),
        batch, seq, tm)
    nb = seq // MOBA_BLOCK
    moba_t = _moba(q_t, kq.reshape(batch, seq, -1), v_t, km.reshape(batch, nb, -1))
    k1, v1_t = _mem_kv(
        mem, row(l1_mem_norm), l1_w_mem_k.astype(BF16), l1_w_mem_v.T.astype(BF16), ones_kv,
        _tile_gain(l1_mem_k_gain, MEM_HEADS) * _tile_gain(l1_mem_q_gain, MEM_HEADS))
    mem1_t = _mem_attn(qm1_t, k1, v1_t, tm)

    wr = jnp.zeros((d, ROUTER_LANES), F32).at[:, :N_EXPERTS].set(l1_moe_router.astype(F32))
    x3, h3, rec, cnt = _outproj1(moba_t, mem1_t, l1_w_out[:main_w].astype(BF16),
                                 l1_w_out[main_w:].astype(BF16), x2, row(l1_ffn_norm), wr, seq, tm)

    tme = 512
    counts = cnt[0, :N_EXPERTS].astype(I32)
    padded = ((counts + tme - 1) // tme) * tme
    ends = jnp.cumsum(padded)
    starts = ends - padded
    max_tiles = (2 * t) // tme + N_EXPERTS
    tile_expert = jnp.minimum(
        jnp.searchsorted(ends // tme, jnp.arange(max_tiles, dtype=I32), side='right'),
        N_EXPERTS - 1).astype(I32)
    n_used = (ends[-1] // tme).astype(I32).reshape(1)
    e1 = rec[:, 0].astype(I32)
    e2 = rec[:, 1].astype(I32)
    d1 = starts[e1] + rec[:, 2].astype(I32)
    d2 = starts[e2] + rec[:, 3].astype(I32)
    def tile_dest(tile):
        return jnp.concatenate([d1.reshape(t // tile, 1, tile), d2.reshape(t // tile, 1, tile)],
                               axis=2)

    tmd, tmc = 1024, 512
    xs = _dispatch(tile_dest(tmd), h3, jnp.zeros((max_tiles * tme, d), F32), tmd)
    ff = l1_moe_w_gate.shape[2]
    ys = _ffn(tile_expert, n_used, xs, None, l1_moe_w_gate.astype(BF16),
              l1_moe_w_up.astype(BF16), l1_moe_w_down.astype(BF16), tme, ff // 4)
    out = _combine(tile_dest(tmc), ys, x3, rec, tmc)
    return out.reshape(batch, seq, d)
```

```python
import functools

import jax
import jax.numpy as jnp
from jax import lax
from jax.experimental import pallas as pl
from jax.experimental.pallas import tpu as pltpu

F32 = jnp.float32
BF16 = jnp.bfloat16
I32 = jnp.int32

RMS_EPS = 1e-6
NEG_INF = -1e30
HEAD_DIM = 64
HEAD_SLAB = 128
MEM_HEADS = 4
MOBA_KV_HEADS = 4
MOBA_GROUP = 3
MOBA_BLOCK = 256
MOBA_TOP_BLOCKS = 3
S5_GROUP_DIM = 16
S5_STATE = 64
S5_GROUPS_PER_SLAB = HEAD_SLAB // S5_GROUP_DIM
S5_SLAB_STATES = S5_GROUPS_PER_SLAB * S5_STATE
N_EXPERTS = 8
ROUTER_LANES = 128
ROUTER_ROWS = 16
SUBLANES = 8
DMA_UNROLL = 8
LOG2_E = 1.4426950408889634
V_ROWS = HEAD_DIM + 16

VMEM_LIMIT_BYTES = 56 * 1024 * 1024

_NT = (((1,), (1,)), ((), ()))
_TN = (((0,), (0,)), ((), ()))


def _cparams(sem):
    return pltpu.CompilerParams(dimension_semantics=sem, vmem_limit_bytes=VMEM_LIMIT_BYTES)


def _rms(x, g):
    return x * lax.rsqrt(jnp.mean(x * x, axis=-1, keepdims=True) + RMS_EPS) * g


def _slab_norm_t(qt, n_heads, scale):
    outs = []
    for h in range(n_heads):
        s = qt[HEAD_SLAB * h:HEAD_SLAB * (h + 1), :]
        ms = jnp.sum(s * s, axis=0, keepdims=True) * (1.0 / HEAD_DIM)
        outs.append(s * (lax.rsqrt(ms + RMS_EPS) * scale))
    return outs


def _inproj0_kernel(x_ref, g_ref, wm_ref, wqt_ref, main_ref, qt_ref):
    hb = _rms(x_ref[...], g_ref[...]).astype(BF16)
    main_ref[...] = jnp.dot(hb, wm_ref[...], preferred_element_type=F32)
    qt = lax.dot_general(wqt_ref[...], hb, _NT, preferred_element_type=F32)
    for h, s in enumerate(_slab_norm_t(qt, MEM_HEADS, HEAD_DIM ** -0.5)):
        qt_ref[0, HEAD_SLAB * h:HEAD_SLAB * (h + 1), :] = s.astype(BF16)


def _inproj0(x2d, g, wm, wqt, batch, seq, tm):
    t, d = x2d.shape
    per_b = seq // tm
    nm = wm.shape[1]
    nq = wqt.shape[0]
    return pl.pallas_call(
        _inproj0_kernel,
        grid=(t // tm,),
        in_specs=[
            pl.BlockSpec((tm, d), lambda i: (i, 0)),
            pl.BlockSpec((1, d), lambda i: (0, 0)),
            pl.BlockSpec((d, nm), lambda i: (0, 0)),
            pl.BlockSpec((nq, d), lambda i: (0, 0)),
        ],
        out_specs=[
            pl.BlockSpec((tm, nm), lambda i: (i, 0)),
            pl.BlockSpec((1, nq, tm), lambda i: (i // per_b, 0, i % per_b)),
        ],
        out_shape=[
            jax.ShapeDtypeStruct((t, nm), F32),
            jax.ShapeDtypeStruct((batch, nq, seq), BF16),
        ],
        compiler_params=_cparams(("parallel",)),
        name="inproj0",
    )(x2d, g, wm, wqt)


def _mem_kv_kernel(mem_ref, g_ref, wk_ref, wvt_ref, ones_ref, kg_ref, k_ref, vt_ref):
    mb = _rms(mem_ref[0], g_ref[...]).astype(BF16)
    k = jnp.dot(mb, wk_ref[...], preferred_element_type=F32)
    ms = jnp.dot(k * k, ones_ref[...], preferred_element_type=F32,
                 precision=lax.Precision.HIGHEST) * (1.0 / HEAD_DIM)
    k_ref[0] = (k * lax.rsqrt(ms + RMS_EPS) * kg_ref[...]).astype(BF16)
    vt_ref[0] = lax.dot_general(wvt_ref[...], mb, _NT, preferred_element_type=F32).astype(BF16)


def _mem_kv(mem, g, wk, wvt, ones, kg):
    b, m, d = mem.shape
    w = wk.shape[1]
    return pl.pallas_call(
        _mem_kv_kernel,
        grid=(b,),
        in_specs=[
            pl.BlockSpec((1, m, d), lambda i: (i, 0, 0)),
            pl.BlockSpec((1, d), lambda i: (0, 0)),
            pl.BlockSpec((d, w), lambda i: (0, 0)),
            pl.BlockSpec((w, d), lambda i: (0, 0)),
            pl.BlockSpec((w, w), lambda i: (0, 0)),
            pl.BlockSpec((1, w), lambda i: (0, 0)),
        ],
        out_specs=[
            pl.BlockSpec((1, m, w), lambda i: (i, 0, 0)),
            pl.BlockSpec((1, w, m), lambda i: (i, 0, 0)),
        ],
        out_shape=[
            jax.ShapeDtypeStruct((b, m, w), BF16),
            jax.ShapeDtypeStruct((b, w, m), BF16),
        ],
        compiler_params=_cparams(("parallel",)),
        name="mem_kv",
    )(mem, g, wk, wvt, ones, kg)


def _mem_attn_kernel(qt_ref, k_ref, vt_ref, o_ref):
    for h in range(MEM_HEADS):
        pair = h // 2
        k2 = k_ref[0, :, HEAD_SLAB * pair:HEAD_SLAB * (pair + 1)]
        q = qt_ref[0, HEAD_SLAB * h:HEAD_SLAB * (h + 1), :]
        s = jnp.dot(k2, q, preferred_element_type=F32)
        m = jnp.max(s, axis=0, keepdims=True)
        p = jnp.exp(s - m)
        l = jnp.sum(p, axis=0, keepdims=True)
        v = vt_ref[0, HEAD_DIM * h:HEAD_DIM * (h + 1), :]
        o = jnp.dot(v, p.astype(BF16), preferred_element_type=F32)
        o_ref[0, HEAD_DIM * h:HEAD_DIM * (h + 1), :] = (o / l).astype(BF16)


def _mem_attn(qt, k, vt, tq):
    b, nq, s = qt.shape
    m, w = k.shape[1], k.shape[2]
    return pl.pallas_call(
        _mem_attn_kernel,
        grid=(b, s // tq),
        in_specs=[
            pl.BlockSpec((1, nq, tq), lambda i, j: (i, 0, j)),
            pl.BlockSpec((1, m, w), lambda i, j: (i, 0, 0)),
            pl.BlockSpec((1, w, m), lambda i, j: (i, 0, 0)),
        ],
        out_specs=pl.BlockSpec((1, w, tq), lambda i, j: (i, 0, j)),
        out_shape=jax.ShapeDtypeStruct((b, w, s), BF16),
        compiler_params=_cparams(("parallel", "parallel")),
        name="mem_attn",
    )(qt, k, vt)


def _s5_kernel(u_ref, bd_ref, cd_ref, ar_ref, ai_ref, d_ref, wglu_ref, bglu_ref, o_ref,
               buf_ref, st_ref, *, ts, n_slabs):
    half = S5_SLAB_STATES
    width = 2 * half

    @pl.when(pl.program_id(0) == 0)
    def _():
        st_ref[...] = jnp.zeros_like(st_ref)

    u = u_ref[...]
    ub = u.astype(BF16)
    for j in range(n_slabs):
        buf_ref[:, width * j:width * (j + 1)] = jnp.dot(
            ub[:, HEAD_SLAB * j:HEAD_SLAB * (j + 1)], bd_ref[j], preferred_element_type=F32)

    for j in range(n_slabs):
        re = slice(width * j, width * j + half)
        im = slice(width * j + half, width * (j + 1))
        ar = jnp.broadcast_to(ar_ref[j:j + 1, :], (SUBLANES, half))
        ai = jnp.broadcast_to(ai_ref[j:j + 1, :], (SUBLANES, half))

        def step(t, carry, re=re, im=im, ar=ar, ai=ai):
            xr, xi = carry
            rows = pl.ds(pl.multiple_of(t * SUBLANES, SUBLANES), SUBLANES)
            nxr = ar * xr - ai * xi + buf_ref[rows, re]
            nxi = ar * xi + ai * xr + buf_ref[rows, im]
            buf_ref[rows, re] = nxr
            buf_ref[rows, im] = nxi
            return nxr, nxi

        xr, xi = lax.fori_loop(0, ts, step, (st_ref[:, re], st_ref[:, im]), unroll=8)
        st_ref[:, re] = xr
        st_ref[:, im] = xi

    ys = []
    for j in range(n_slabs):
        ys.append(jnp.dot(buf_ref[:, width * j:width * (j + 1)].astype(BF16), cd_ref[j],
                          preferred_element_type=F32))
    y = jnp.concatenate(ys, axis=1) + d_ref[...] * u
    g = jax.nn.gelu(y)
    z = jnp.dot(g.astype(BF16), wglu_ref[...], preferred_element_type=F32) + bglu_ref[...]
    o_ref[...] = (g / (1.0 + jnp.exp(-z))).astype(BF16)


def _s5(u_tm, bd, cd, ar, ai, dskip, wglu, bglu, batch, ts):
    rows, c = u_tm.shape
    n_slabs = c // HEAD_SLAB
    width = 2 * S5_SLAB_STATES
    tile = ts * batch
    return pl.pallas_call(
        functools.partial(_s5_kernel, ts=ts, n_slabs=n_slabs),
        grid=(rows // tile,),
        in_specs=[
            pl.BlockSpec((tile, c), lambda i: (i, 0)),
            pl.BlockSpec((n_slabs, HEAD_SLAB, width), lambda i: (0, 0, 0)),
            pl.BlockSpec((n_slabs, width, HEAD_SLAB), lambda i: (0, 0, 0)),
            pl.BlockSpec((n_slabs, S5_SLAB_STATES), lambda i: (0, 0)),
            pl.BlockSpec((n_slabs, S5_SLAB_STATES), lambda i: (0, 0)),
            pl.BlockSpec((1, c), lambda i: (0, 0)),
            pl.BlockSpec((c, c), lambda i: (0, 0)),
            pl.BlockSpec((1, c), lambda i: (0, 0)),
        ],
        out_specs=pl.BlockSpec((tile, c), lambda i: (i, 0)),
        out_shape=jax.ShapeDtypeStruct((rows, c), BF16),
        scratch_shapes=[
            pltpu.VMEM((tile, n_slabs * width), F32),
            pltpu.VMEM((batch, n_slabs * width), F32),
        ],
        compiler_params=_cparams(("arbitrary",)),
        name="s5_mixer",
    )(u_tm, bd, cd, ar, ai, dskip, wglu, bglu)


def _outproj_body(a, a_transposed, bt_ref, wa_ref, wb_ref, x_ref, g_ref):
    dn = _TN if a_transposed else (((1,), (0,)), ((), ()))
    y = lax.dot_general(a, wa_ref[...], dn, preferred_element_type=F32)
    y = y + lax.dot_general(bt_ref[0], wb_ref[...], _TN, preferred_element_type=F32)
    xn = x_ref[...] + y
    return xn, _rms(xn, g_ref[...])


def _outproj0_kernel(a_ref, bt_ref, wa_ref, wb_ref, x_ref, g_ref, xo_ref, h_ref):
    xn, h = _outproj_body(a_ref[...], False, bt_ref, wa_ref, wb_ref, x_ref, g_ref)
    xo_ref[...] = xn
    h_ref[...] = h.astype(BF16)


def _outproj0(a, bt, wa, wb, x2d, g, seq, tm):
    t, d = x2d.shape
    per_b = seq // tm
    ka, kb = a.shape[1], bt.shape[1]
    return pl.pallas_call(
        _outproj0_kernel,
        grid=(t // tm,),
        in_specs=[
            pl.BlockSpec((tm, ka), lambda i: (i, 0)),
            pl.BlockSpec((1, kb, tm), lambda i: (i // per_b, 0, i % per_b)),
            pl.BlockSpec((ka, d), lambda i: (0, 0)),
            pl.BlockSpec((kb, d), lambda i: (0, 0)),
            pl.BlockSpec((tm, d), lambda i: (i, 0)),
            pl.BlockSpec((1, d), lambda i: (0, 0)),
        ],
        out_specs=[
            pl.BlockSpec((tm, d), lambda i: (i, 0)),
            pl.BlockSpec((tm, d), lambda i: (i, 0)),
        ],
        out_shape=[
            jax.ShapeDtypeStruct((t, d), F32),
            jax.ShapeDtypeStruct((t, d), BF16),
        ],
        compiler_params=_cparams(("parallel",)),
        name="outproj0",
    )(a, bt, wa, wb, x2d, g)


def _outproj1_kernel(at_ref, bt_ref, wa_ref, wb_ref, x_ref, g_ref, wrh_ref, wrl_ref,
                     xo_ref, h_ref, rec_ref, cnt_ref, run_ref, *, tm):
    i = pl.program_id(0)

    @pl.when(i == 0)
    def _():
        run_ref[...] = jnp.zeros_like(run_ref)

    xn, h = _outproj_body(at_ref[0], True, bt_ref, wa_ref, wb_ref, x_ref, g_ref)
    xo_ref[...] = xn
    h_ref[...] = h

    h_hi = h.astype(BF16)
    h_lo = (h - h_hi.astype(F32)).astype(BF16)
    logits = (lax.dot_general(wrh_ref[...], h_hi, _NT, preferred_element_type=F32)
              + lax.dot_general(wrh_ref[...], h_lo, _NT, preferred_element_type=F32)
              + lax.dot_general(wrl_ref[...], h_hi, _NT, preferred_element_type=F32))
    ex = lax.broadcasted_iota(I32, (ROUTER_ROWS, tm), 0)
    logits = jnp.where(ex < N_EXPERTS, logits, -jnp.inf)
    m1 = jnp.max(logits, axis=0, keepdims=True)
    e1 = jnp.min(jnp.where(logits == m1, ex, ROUTER_ROWS), axis=0, keepdims=True)
    rest = jnp.where(ex == e1, -jnp.inf, logits)
    m2 = jnp.max(rest, axis=0, keepdims=True)
    e2 = jnp.min(jnp.where(rest == m2, ex, ROUTER_ROWS), axis=0, keepdims=True)
    w2 = jnp.exp(m2 - m1)
    g1 = 1.0 / (1.0 + w2)
    g2 = w2 / (1.0 + w2)

    hot = jnp.where((ex == e1) | (ex == e2), 1.0, 0.0)
    r_io = lax.broadcasted_iota(I32, (tm, tm), 0)
    c_io = lax.broadcasted_iota(I32, (tm, tm), 1)
    triu = jnp.where(r_io < c_io, 1.0, 0.0).astype(BF16)
    before = (jnp.dot(hot.astype(BF16), triu, preferred_element_type=F32)
              + run_ref[:, 0:1])
    r1 = jnp.sum(jnp.where(ex == e1, before, 0.0), axis=0, keepdims=True)
    r2 = jnp.sum(jnp.where(ex == e2, before, 0.0), axis=0, keepdims=True)
    run = run_ref[...] + jnp.sum(hot, axis=1, keepdims=True)
    run_ref[...] = run
    cnt_ref[...] = run

    slot = lax.broadcasted_iota(I32, (SUBLANES, tm), 0)
    rec = jnp.where(slot == 0, e1.astype(F32), 0.0)
    rec = jnp.where(slot == 1, e2.astype(F32), rec)
    rec = jnp.where(slot == 2, r1, rec)
    rec = jnp.where(slot == 3, r2, rec)
    rec = jnp.where(slot == 4, g1, rec)
    rec = jnp.where(slot == 5, g2, rec)
    rec_ref[...] = rec


def _outproj1(at, bt, wa, wb, x2d, g, wr_hi, wr_lo, seq, tm):
    t, d = x2d.shape
    per_b = seq // tm
    ka, kb = at.shape[1], bt.shape[1]
    return pl.pallas_call(
        functools.partial(_outproj1_kernel, tm=tm),
        grid=(t // tm,),
        in_specs=[
            pl.BlockSpec((1, ka, tm), lambda i: (i // per_b, 0, i % per_b)),
            pl.BlockSpec((1, kb, tm), lambda i: (i // per_b, 0, i % per_b)),
            pl.BlockSpec((ka, d), lambda i: (0, 0)),
            pl.BlockSpec((kb, d), lambda i: (0, 0)),
            pl.BlockSpec((tm, d), lambda i: (i, 0)),
            pl.BlockSpec((1, d), lambda i: (0, 0)),
            pl.BlockSpec((ROUTER_ROWS, d), lambda i: (0, 0)),
            pl.BlockSpec((ROUTER_ROWS, d), lambda i: (0, 0)),
        ],
        out_specs=[
            pl.BlockSpec((tm, d), lambda i: (i, 0)),
            pl.BlockSpec((tm, d), lambda i: (i, 0)),
            pl.BlockSpec((SUBLANES, tm), lambda i: (0, i)),
            pl.BlockSpec((ROUTER_ROWS, ROUTER_LANES), lambda i: (0, 0)),
        ],
        out_shape=[
            jax.ShapeDtypeStruct((t, d), F32),
            jax.ShapeDtypeStruct((t, d), F32),
            jax.ShapeDtypeStruct((SUBLANES, t), F32),
            jax.ShapeDtypeStruct((ROUTER_ROWS, ROUTER_LANES), F32),
        ],
        scratch_shapes=[pltpu.VMEM((ROUTER_ROWS, ROUTER_LANES), F32)],
        compiler_params=_cparams(("arbitrary",)),
        name="outproj1_router",
    )(at, bt, wa, wb, x2d, g, wr_hi, wr_lo)


def _ffn_kernel(te_ref, nu_ref, *refs, has_res, cast_in):
    if has_res:
        xs_ref, res_ref, wg_ref, wu_ref, wd_ref, o_ref, acc_ref, xb_ref = refs
    else:
        xs_ref, wg_ref, wu_ref, wd_ref, o_ref, acc_ref, xb_ref = refs
        res_ref = None
    i = pl.program_id(0)
    f = pl.program_id(1)
    nf = pl.num_programs(1)

    @pl.when(i < nu_ref[0])
    def _():
        @pl.when(f == 0)
        def _():
            acc_ref[...] = jnp.zeros_like(acc_ref)
            xb_ref[...] = xs_ref[...].astype(BF16) if cast_in else xs_ref[...]

        xb = xb_ref[...]
        gt = jnp.dot(xb, wg_ref[0], preferred_element_type=F32)
        up = jnp.dot(xb, wu_ref[0], preferred_element_type=F32)
        mid = (gt / (1.0 + jnp.exp(-gt))) * up
        acc_ref[...] += jnp.dot(mid.astype(BF16), wd_ref[0], preferred_element_type=F32)

        @pl.when(f == nf - 1)
        def _():
            o_ref[...] = acc_ref[...] + res_ref[...] if has_res else acc_ref[...]

    @pl.when((i >= nu_ref[0]) & (f == nf - 1))
    def _():
        o_ref[...] = jnp.zeros_like(o_ref)


def _ffn(tile_expert, n_used, xs, res, wg, wu, wd, tm, tf):
    rows, d = xs.shape
    ff = wg.shape[2]
    nf = ff // tf
    has_res = res is not None
    cast_in = xs.dtype != BF16

    def row_map(i, f, te, nu):
        return (jnp.minimum(i, nu[0] - 1), 0)

    def f_of(i, f, nu):
        return jnp.where(i < nu[0], f, nf - 1)

    def wcol_map(i, f, te, nu):
        return (te[jnp.minimum(i, nu[0] - 1)], 0, f_of(i, f, nu))

    def wrow_map(i, f, te, nu):
        return (te[jnp.minimum(i, nu[0] - 1)], f_of(i, f, nu), 0)

    in_specs = [pl.BlockSpec((tm, d), row_map)]
    args = [xs]
    if has_res:
        in_specs.append(pl.BlockSpec((tm, d), row_map))
        args.append(res)
    in_specs += [
        pl.BlockSpec((1, d, tf), wcol_map),
        pl.BlockSpec((1, d, tf), wcol_map),
        pl.BlockSpec((1, tf, d), wrow_map),
    ]
    args += [wg, wu, wd]
    grid_spec = pltpu.PrefetchScalarGridSpec(
        num_scalar_prefetch=2,
        grid=(rows // tm, nf),
        in_specs=in_specs,
        out_specs=pl.BlockSpec((tm, d), lambda i, f, te, nu: (i, 0)),
        scratch_shapes=[pltpu.VMEM((tm, d), F32), pltpu.VMEM((tm, d), BF16)],
    )
    return pl.pallas_call(
        functools.partial(_ffn_kernel, has_res=has_res, cast_in=cast_in),
        grid_spec=grid_spec,
        out_shape=jax.ShapeDtypeStruct((rows, d), F32),
        compiler_params=_cparams(("arbitrary", "arbitrary")),
        name="ffn_res" if has_res else "ffn_moe",
    )(tile_expert, n_used, *args)


def _proj1_kernel(x_ref, g1_ref, g2_ref, wqt_ref, wk_ref, wvt_ref, kg_ref, qg_ref,
                  qt_ref, qmt_ref, k_ref, km_ref, vt_ref, *, tm, n_q, per_b):
    x = x_ref[...]
    h1 = _rms(x, g1_ref[...]).astype(BF16)
    h2 = _rms(x, g2_ref[...]).astype(BF16)
    qt = lax.dot_general(wqt_ref[...], h1, _NT, preferred_element_type=F32)

    for h in range(n_q):
        s = qt[HEAD_DIM * h:HEAD_DIM * (h + 1), :]
        ms = jnp.mean(s * s, axis=0, keepdims=True)
        qt_ref[0, HEAD_DIM * h:HEAD_DIM * (h + 1), :] = (
            s * (lax.rsqrt(ms + RMS_EPS) * (HEAD_DIM ** -0.5 * LOG2_E))).astype(BF16)
    qm = qt[HEAD_DIM * n_q:, :]
    for h, s in enumerate(_slab_norm_t(qm, MEM_HEADS, HEAD_DIM ** -0.5)):
        qmt_ref[0, HEAD_SLAB * h:HEAD_SLAB * (h + 1), :] = s.astype(BF16)

    k = jnp.dot(h2, wk_ref[...], preferred_element_type=F32)
    nblk = tm // MOBA_BLOCK
    first_blk = lax.rem(pl.program_id(0), per_b) * nblk
    lane = lax.broadcasted_iota(I32, (MOBA_BLOCK, HEAD_SLAB), 1)
    for kv in range(MOBA_KV_HEADS):
        cols = slice(HEAD_SLAB * kv, HEAD_SLAB * (kv + 1))
        ks = k[:, cols]
        ms = jnp.sum(ks * ks, axis=-1, keepdims=True) * (1.0 / HEAD_DIM)
        kn = ks * lax.rsqrt(ms + RMS_EPS) * kg_ref[:, cols]
        kq = kn * qg_ref[:, cols]
        for j in range(nblk):
            rows = slice(MOBA_BLOCK * j, MOBA_BLOCK * (j + 1))
            km_ref[j, :, cols] = jnp.mean(kn[rows], axis=0, keepdims=True) * qg_ref[:, cols]
            k_ref[rows, cols] = jnp.where(lane == HEAD_DIM + first_blk + j, 1.0,
                                          kq[rows]).astype(BF16)

    vt = lax.dot_general(wvt_ref[...], h2, _NT, preferred_element_type=F32)
    ones_row = jnp.where(lax.broadcasted_iota(I32, (V_ROWS - HEAD_DIM, tm), 0) == 0, 1.0, 0.0)
    for kv in range(MOBA_KV_HEADS):
        vt_ref[0, V_ROWS * kv:V_ROWS * kv + HEAD_DIM, :] = (
            vt[HEAD_DIM * kv:HEAD_DIM * (kv + 1), :].astype(BF16))
        vt_ref[0, V_ROWS * kv + HEAD_DIM:V_ROWS * (kv + 1), :] = ones_row.astype(BF16)


def _proj1(x2d, g1, g2, wqt, wk, wvt, kg, qg, batch, seq, tm):
    t, d = x2d.shape
    per_b = seq // tm
    nq_rows = wqt.shape[0]
    n_q = (nq_rows - MEM_HEADS * HEAD_SLAB) // HEAD_DIM
    kw = wk.shape[1]
    vw = wvt.shape[0]
    v_rows = MOBA_KV_HEADS * V_ROWS
    nblk = tm // MOBA_BLOCK
    return pl.pallas_call(
        functools.partial(_proj1_kernel, tm=tm, n_q=n_q, per_b=per_b),
        grid=(t // tm,),
        in_specs=[
            pl.BlockSpec((tm, d), lambda i: (i, 0)),
            pl.BlockSpec((1, d), lambda i: (0, 0)),
            pl.BlockSpec((1, d), lambda i: (0, 0)),
            pl.BlockSpec((nq_rows, d), lambda i: (0, 0)),
            pl.BlockSpec((d, kw), lambda i: (0, 0)),
            pl.BlockSpec((vw, d), lambda i: (0, 0)),
            pl.BlockSpec((1, kw), lambda i: (0, 0)),
            pl.BlockSpec((1, kw), lambda i: (0, 0)),
        ],
        out_specs=[
            pl.BlockSpec((1, n_q * HEAD_DIM, tm), lambda i: (i // per_b, 0, i % per_b)),
            pl.BlockSpec((1, MEM_HEADS * HEAD_SLAB, tm), lambda i: (i // per_b, 0, i % per_b)),
            pl.BlockSpec((tm, kw), lambda i: (i, 0)),
            pl.BlockSpec((nblk, 1, kw), lambda i: (i, 0, 0)),
            pl.BlockSpec((1, v_rows, tm), lambda i: (i // per_b, 0, i % per_b)),
        ],
        out_shape=[
            jax.ShapeDtypeStruct((batch, n_q * HEAD_DIM, seq), BF16),
            jax.ShapeDtypeStruct((batch, MEM_HEADS * HEAD_SLAB, seq), BF16),
            jax.ShapeDtypeStruct((t, kw), BF16),
            jax.ShapeDtypeStruct((t // MOBA_BLOCK, 1, kw), F32),
            jax.ShapeDtypeStruct((batch, v_rows, seq), BF16),
        ],
        compiler_params=_cparams(("parallel",)),
        name="proj1",
    )(x2d, g1, g2, wqt, wk, wvt, kg, qg)


def _moba_kernel(qt_ref, k_ref, vt_ref, km_ref, o_ref, s_ref):
    qb = pl.program_id(2)
    nb = km_ref.shape[1]
    bq = MOBA_BLOCK
    nq = MOBA_GROUP * bq
    q64 = jnp.concatenate(
        [qt_ref[0, HEAD_DIM * g:HEAD_DIM * (g + 1), :] for g in range(MOBA_GROUP)], axis=1)

    blk = lax.broadcasted_iota(I32, (nb, nq), 0)
    km = km_ref[0, :, 0:HEAD_DIM]
    km_hi = km.astype(BF16)
    km_lo = (km - km_hi.astype(F32)).astype(BF16)
    gate = (jnp.dot(km_hi, q64, preferred_element_type=F32)
            + jnp.dot(km_lo, q64, preferred_element_type=F32))
    gate = jnp.where(blk < qb, gate, NEG_INF)
    rank = jnp.zeros((nb, nq), F32)
    for m in range(nb):
        gm = gate[m:m + 1, :]
        rank = rank + jnp.where(gm > gate, 1.0, jnp.where((gm == gate) & (blk > m), 1.0, 0.0))
    dropped = (blk < qb) & (rank >= MOBA_TOP_BLOCKS)

    bias = jnp.where(dropped, NEG_INF, 0.0)
    pad_rows = HEAD_SLAB - HEAD_DIM - 2 * nb
    q3 = jnp.concatenate(
        [q64, jnp.concatenate([bias, jnp.zeros((nb, nq), F32)], axis=0).astype(BF16),
         jnp.zeros((pad_rows, nq), BF16)], axis=0)

    key_pos = lax.broadcasted_iota(I32, (bq, nq), 0)
    qry_pos = lax.broadcasted_iota(I32, (bq, nq), 1) & (bq - 1)
    causal = key_pos <= qry_pos

    for n_blocks in range(1, nb + 1):
        @pl.when(qb == n_blocks - 1)
        def _(n_blocks=n_blocks):
            mx = None
            for n in range(n_blocks):
                s = jnp.dot(k_ref[0, bq * n:bq * (n + 1), :], q3, preferred_element_type=F32)
                if n == n_blocks - 1:
                    s = jnp.where(causal, s, NEG_INF)
                s_ref[n] = s
                cm = jnp.max(s, axis=0, keepdims=True)
                mx = cm if mx is None else jnp.maximum(mx, cm)
            acc = jnp.zeros((V_ROWS, nq), F32)
            for n in range(n_blocks):
                p = jnp.exp2(s_ref[n] - mx).astype(BF16)
                acc = acc + jnp.dot(vt_ref[0, :, bq * n:bq * (n + 1)], p,
                                    preferred_element_type=F32)
            out = acc[0:HEAD_DIM, :] / acc[HEAD_DIM:HEAD_DIM + 1, :]
            for g in range(MOBA_GROUP):
                o_ref[0, HEAD_DIM * g:HEAD_DIM * (g + 1), :] = (
                    out[:, bq * g:bq * (g + 1)].astype(BF16))


def _moba(qt, k3, vt, km3):
    b, nq_rows, s = qt.shape
    nb = s // MOBA_BLOCK
    rows = MOBA_GROUP * HEAD_DIM
    return pl.pallas_call(
        _moba_kernel,
        grid=(b, MOBA_KV_HEADS, nb),
        in_specs=[
            pl.BlockSpec((1, rows, MOBA_BLOCK), lambda i, j, q: (i, j, q)),
            pl.BlockSpec((1, s, HEAD_SLAB), lambda i, j, q: (i, 0, j)),
            pl.BlockSpec((1, V_ROWS, s), lambda i, j, q: (i, j, 0)),
            pl.BlockSpec((1, nb, HEAD_SLAB), lambda i, j, q: (i, 0, j)),
        ],
        out_specs=pl.BlockSpec((1, rows, MOBA_BLOCK), lambda i, j, q: (i, j, q)),
        out_shape=jax.ShapeDtypeStruct((b, MOBA_KV_HEADS * rows, s), BF16),
        scratch_shapes=[pltpu.VMEM((nb, MOBA_BLOCK, MOBA_GROUP * MOBA_BLOCK), F32)],
        compiler_params=_cparams(("parallel", "parallel", "arbitrary")),
        name="moba_attn",
    )(qt, k3, vt, km3)


def _row_copy(src_ref, src_row, dst_ref, dst_row, sem):
    return pltpu.make_async_copy(src_ref.at[pl.ds(src_row, 1)], dst_ref.at[pl.ds(dst_row, 1)], sem)


def _for_each_row(n_rows, fn):
    def body(j8, carry):
        for u in range(DMA_UNROLL):
            fn(j8 * DMA_UNROLL + u, u % 2)
        return carry

    lax.fori_loop(0, n_rows // DMA_UNROLL, body, 0)


def _dispatch_kernel(dest_ref, h_ref, init_ref, xs_ref, sem, *, tm):
    del init_ref

    def copy(j, slot):
        return _row_copy(h_ref, j, xs_ref, dest_ref[0, 0, slot * tm + j], sem)

    for slot in range(2):
        _for_each_row(tm, lambda j, lane, slot=slot: copy(j, slot).start(priority=lane))
    for slot in range(2):
        _for_each_row(tm, lambda j, lane, slot=slot: copy(j, slot).wait())


def _dispatch(dest3, h, xs_init, tm):
    t, d = h.shape
    return pl.pallas_call(
        functools.partial(_dispatch_kernel, tm=tm),
        grid=(t // tm,),
        in_specs=[
            pl.BlockSpec((1, 1, 2 * tm), lambda i: (i, 0, 0), memory_space=pltpu.SMEM),
            pl.BlockSpec((tm, d), lambda i: (i, 0)),
            pl.BlockSpec(memory_space=pl.ANY),
        ],
        out_specs=pl.BlockSpec(memory_space=pl.ANY),
        out_shape=jax.ShapeDtypeStruct(xs_init.shape, xs_init.dtype),
        scratch_shapes=[pltpu.SemaphoreType.DMA(())],
        input_output_aliases={2: 0},
        compiler_params=_cparams(("arbitrary",)),
        name="moe_dispatch",
    )(dest3, h, xs_init)


def _combine_kernel(dest_ref, next_ref, ys_ref, x_ref, gate_ref, o_ref, buf_ref, sem, *, tm):
    i = pl.program_id(0)
    cur = lax.rem(i, 2)

    def copy(idx_ref, buf_slot, j):
        return _row_copy(ys_ref, idx_ref[0, 0, j], buf_ref.at[buf_slot], j, sem.at[buf_slot])

    @pl.when(i == 0)
    def _():
        _for_each_row(2 * tm, lambda j, lane: copy(dest_ref, 0, j).start(priority=lane))

    @pl.when(i + 1 < pl.num_programs(0))
    def _():
        _for_each_row(2 * tm, lambda j, lane: copy(next_ref, 1 - cur, j).start(priority=lane))

    _for_each_row(2 * tm, lambda j, lane: copy(dest_ref, cur, j).wait())
    g1 = gate_ref[:, 0:1]
    g2 = gate_ref[:, 1:2]
    o_ref[...] = x_ref[...] + (g1 * buf_ref[cur, 0:tm, :] + g2 * buf_ref[cur, tm:2 * tm, :])


def _combine(dest3, ys, x2d, gates, tm):
    t, d = x2d.shape
    n = t // tm
    return pl.pallas_call(
        functools.partial(_combine_kernel, tm=tm),
        grid=(n,),
        in_specs=[
            pl.BlockSpec((1, 1, 2 * tm), lambda i: (i, 0, 0), memory_space=pltpu.SMEM),
            pl.BlockSpec((1, 1, 2 * tm), lambda i: (jnp.minimum(i + 1, n - 1), 0, 0),
                         memory_space=pltpu.SMEM),
            pl.BlockSpec(memory_space=pl.ANY),
            pl.BlockSpec((tm, d), lambda i: (i, 0)),
            pl.BlockSpec((tm, gates.shape[1]), lambda i: (i, 0)),
        ],
        out_specs=pl.BlockSpec((tm, d), lambda i: (i, 0)),
        out_shape=jax.ShapeDtypeStruct((t, d), F32),
        scratch_shapes=[pltpu.VMEM((2, 2 * tm, d), F32), pltpu.SemaphoreType.DMA((2,))],
        compiler_params=_cparams(("arbitrary",)),
        name="moe_combine",
    )(dest3, dest3, ys, x2d, gates)


def _pad_heads_t(w_cols, offsets):
    d = w_cols.shape[0]
    nh = w_cols.shape[1] // HEAD_DIM
    wt = w_cols.T.reshape(nh, HEAD_DIM, d)
    hi = jnp.asarray(offsets, I32).reshape(nh, 1, 1) > 0
    z = jnp.zeros_like(wt)
    slab = jnp.concatenate([jnp.where(hi, z, wt), jnp.where(hi, wt, z)], axis=1)
    return slab.reshape(nh * HEAD_SLAB, d)


def _pad_head_cols(w):
    rows = w.shape[0]
    nh = w.shape[1] // HEAD_DIM
    w3 = w.reshape(rows, nh, HEAD_DIM)
    return jnp.concatenate([w3, jnp.zeros_like(w3)], axis=2).reshape(rows, nh * HEAD_SLAB)


def _head_block_ones(width):
    idx = jnp.arange(width) // HEAD_DIM
    return (idx[:, None] == idx[None, :]).astype(F32)


def _tile_gain(g, n_heads):
    return jnp.tile(g.astype(F32), n_heads).reshape(1, n_heads * HEAD_DIM)


def _s5_params(lam_re, lam_im, log_dt, b_re, b_im, c_re, c_im):
    f32 = F32
    g, p = lam_re.shape
    n = b_re.shape[2]
    lam = lax.complex(lam_re.astype(f32), lam_im.astype(f32))
    dt = jnp.exp(log_dt.astype(f32))[:, None]
    lam_bar = jnp.exp(lam * dt)
    b_bar = ((lam_bar - 1.0) / lam)[..., None] * lax.complex(b_re.astype(f32), b_im.astype(f32))
    ns = g // S5_GROUPS_PER_SLAB
    eye = jnp.eye(S5_GROUPS_PER_SLAB, dtype=f32)

    def in_map(part):
        blk = part.reshape(ns, S5_GROUPS_PER_SLAB, p, n).transpose(0, 1, 3, 2)
        return jnp.einsum('sgnp,gh->sgnhp', blk, eye).reshape(
            ns, S5_GROUPS_PER_SLAB * n, S5_GROUPS_PER_SLAB * p)

    def out_map(part):
        blk = part.reshape(ns, S5_GROUPS_PER_SLAB, n, p).transpose(0, 1, 3, 2)
        return jnp.einsum('sgpn,gh->sgphn', blk, eye).reshape(
            ns, S5_GROUPS_PER_SLAB * p, S5_GROUPS_PER_SLAB * n)

    bd = jnp.concatenate([in_map(jnp.real(b_bar)), in_map(jnp.imag(b_bar))], axis=2)
    cd = jnp.concatenate([out_map(c_re.astype(f32)), out_map(-c_im.astype(f32))], axis=1)
    ar = jnp.real(lam_bar).reshape(ns, S5_GROUPS_PER_SLAB * p)
    ai = jnp.imag(lam_bar).reshape(ns, S5_GROUPS_PER_SLAB * p)
    return bd.astype(BF16), cd.astype(BF16), ar, ai


def kernel(x, mem, l0_mix_norm, l0_w_in, l0_s5_lam_re, l0_s5_lam_im, l0_s5_log_dt, l0_s5_b_re, l0_s5_b_im, l0_s5_c_re, l0_s5_c_im, l0_s5_d, l0_s5_w_glu, l0_s5_b_glu, l0_mem_norm, l0_w_mem_k, l0_w_mem_v, l0_mem_q_gain, l0_mem_k_gain, l0_w_out, l0_ffn_norm, l0_ffn_w_gate, l0_ffn_w_up, l0_ffn_w_down, kv_norm, kv_w_k, kv_w_v, kv_k_gain, l1_mix_norm, l1_w_in, l1_moba_q_gain, l1_mem_norm, l1_w_mem_k, l1_w_mem_v, l1_mem_q_gain, l1_mem_k_gain, l1_w_out, l1_ffn_norm, l1_moe_router, l1_moe_w_gate, l1_moe_w_up, l1_moe_w_down):
    batch, seq, d = x.shape
    t = batch * seq
    main_w = l0_s5_w_glu.shape[0]
    mem_w = l0_w_mem_k.shape[1]
    n_q = main_w // HEAD_DIM
    tm = 512
    row = lambda v: v.astype(F32).reshape(1, -1)

    mem_off = [HEAD_DIM * (h % 2) for h in range(MEM_HEADS)]
    ones_kv = _head_block_ones(mem_w)
    x2d = x.reshape(t, d)

    main_in, qm0_t = _inproj0(
        x2d, row(l0_mix_norm), l0_w_in[:, :main_w].astype(BF16),
        _pad_heads_t(l0_w_in[:, main_w:], mem_off).astype(BF16), batch, seq, tm)
    k0, v0_t = _mem_kv(
        mem, row(l0_mem_norm), l0_w_mem_k.astype(BF16), l0_w_mem_v.T.astype(BF16), ones_kv,
        _tile_gain(l0_mem_k_gain, MEM_HEADS) * _tile_gain(l0_mem_q_gain, MEM_HEADS))
    mem0_t = _mem_attn(qm0_t, k0, v0_t, tm)

    bd, cd, ar, ai = _s5_params(l0_s5_lam_re, l0_s5_lam_im, l0_s5_log_dt, l0_s5_b_re, l0_s5_b_im,
                                l0_s5_c_re, l0_s5_c_im)
    u_tm = main_in.reshape(batch, seq, main_w).transpose(1, 0, 2).reshape(t, main_w)
    s5_tm = _s5(u_tm, bd, cd, ar, ai, row(l0_s5_d), l0_s5_w_glu.astype(BF16), row(l0_s5_b_glu),
                batch, 64)
    s5_out = s5_tm.reshape(seq, batch, main_w).transpose(1, 0, 2).reshape(t, main_w)

    x1, h1 = _outproj0(s5_out, mem0_t, l0_w_out[:main_w].astype(BF16),
                       l0_w_out[main_w:].astype(BF16), x2d, row(l0_ffn_norm), seq, tm)
    n_tiles = t // tm
    x2 = _ffn(jnp.zeros((n_tiles,), I32), jnp.full((1,), n_tiles, I32), h1, x1,
              l0_ffn_w_gate.astype(BF16)[None], l0_ffn_w_up.astype(BF16)[None],
              l0_ffn_w_down.astype(BF16)[None], tm, l0_ffn_w_gate.shape[1] // 2)

    wq1_t = jnp.concatenate([l1_w_in[:, :main_w].T,
                             _pad_heads_t(l1_w_in[:, main_w:], mem_off)], axis=0).astype(BF16)
    q_t, qm1_t, kq, km, v_t = _proj1(
        x2, row(l1_mix_norm), row(kv_norm), wq1_t, _pad_head_cols(kv_w_k).astype(BF16),
        kv_w_v.T.astype(BF16), _pad_head_cols(_tile_gain(kv_k_gain, MOBA_KV_HEADS)),
        _pad_head_cols(_tile_gain(l1_moba_q_gain, MOBA_KV_HEADS)), batch, seq, tm)
    nb = seq // MOBA_BLOCK
    moba_t = _moba(q_t, kq.reshape(batch, seq, -1), v_t, km.reshape(batch, nb, -1))
    k1, v1_t = _mem_kv(
        mem, row(l1_mem_norm), l1_w_mem_k.astype(BF16), l1_w_mem_v.T.astype(BF16), ones_kv,
        _tile_gain(l1_mem_k_gain, MEM_HEADS) * _tile_gain(l1_mem_q_gain, MEM_HEADS))
    mem1_t = _mem_attn(qm1_t, k1, v1_t, tm)

    wr_t = jnp.zeros((ROUTER_ROWS, d), F32).at[:N_EXPERTS].set(l1_moe_router.astype(F32).T)
    wr_hi = wr_t.astype(BF16)
    wr_lo = (wr_t - wr_hi.astype(F32)).astype(BF16)
    x3, h3, rec, cnt = _outproj1(moba_t, mem1_t, l1_w_out[:main_w].astype(BF16),
                                 l1_w_out[main_w:].astype(BF16), x2, row(l1_ffn_norm),
                                 wr_hi, wr_lo, seq, tm)

    tme = 512
    counts = cnt[:N_EXPERTS, 0].astype(I32)
    padded = ((counts + tme - 1) // tme) * tme
    ends = jnp.cumsum(padded)
    starts = ends - padded
    max_tiles = (2 * t) // tme + N_EXPERTS
    tile_expert = jnp.minimum(
        jnp.searchsorted(ends // tme, jnp.arange(max_tiles, dtype=I32), side='right'),
        N_EXPERTS - 1).astype(I32)
    n_used = (ends[-1] // tme).astype(I32).reshape(1)
    e1 = rec[0].astype(I32)
    e2 = rec[1].astype(I32)
    d1 = starts[e1] + rec[2].astype(I32)
    d2 = starts[e2] + rec[3].astype(I32)
    gates = rec[4:6].T
    def tile_dest(tile):
        return jnp.concatenate([d1.reshape(t // tile, 1, tile), d2.reshape(t // tile, 1, tile)],
                               axis=2)

    tmd, tmc = 1024, 512
    xs = _dispatch(tile_dest(tmd), h3, jnp.zeros((max_tiles * tme, d), F32), tmd)
    ff = l1_moe_w_gate.shape[2]
    ys = _ffn(tile_expert, n_used, xs, None, l1_moe_w_gate.astype(BF16),
              l1_moe_w_up.astype(BF16), l1_moe_w_down.astype(BF16), tme, ff // 4)
    out = _combine(tile_dest(tmc), ys, x3, gates, tmc)
    return out.reshape(batch, seq, d)
```

```python
import functools

import jax
import jax.numpy as jnp
from jax import lax
from jax.experimental import pallas as pl
from jax.experimental.pallas import tpu as pltpu

F32 = jnp.float32
BF16 = jnp.bfloat16
I32 = jnp.int32

RMS_EPS = 1e-6
NEG_INF = -1e30
HEAD_DIM = 64
HEAD_SLAB = 128
MEM_HEADS = 4
MOBA_KV_HEADS = 4
MOBA_GROUP = 3
MOBA_BLOCK = 256
MOBA_TOP_BLOCKS = 3
S5_GROUP_DIM = 16
S5_STATE = 64
S5_GROUPS_PER_SLAB = HEAD_SLAB // S5_GROUP_DIM
S5_SLAB_STATES = S5_GROUPS_PER_SLAB * S5_STATE
N_EXPERTS = 8
ROUTER_LANES = 128
ROUTER_ROWS = 16
SUBLANES = 8
LANES = 128
ROW_TILE = SUBLANES
DMA_UNROLL = 8
LOG2_E = 1.4426950408889634
V_ROWS = HEAD_DIM + 16

VMEM_LIMIT_BYTES = 56 * 1024 * 1024

_NT = (((1,), (1,)), ((), ()))
_TN = (((0,), (0,)), ((), ()))


def _cparams(sem):
    return pltpu.CompilerParams(dimension_semantics=sem, vmem_limit_bytes=VMEM_LIMIT_BYTES)


def _rms(x, g):
    return x * lax.rsqrt(jnp.mean(x * x, axis=-1, keepdims=True) + RMS_EPS) * g


def _slab_norm_t(qt, n_heads, scale):
    outs = []
    for h in range(n_heads):
        s = qt[HEAD_SLAB * h:HEAD_SLAB * (h + 1), :]
        ms = jnp.sum(s * s, axis=0, keepdims=True) * (1.0 / HEAD_DIM)
        outs.append(s * (lax.rsqrt(ms + RMS_EPS) * scale))
    return outs


def _inproj0_kernel(x_ref, g_ref, wm_ref, wqt_ref, main_ref, qt_ref):
    hb = _rms(x_ref[...], g_ref[...]).astype(BF16)
    main_ref[...] = jnp.dot(hb, wm_ref[...], preferred_element_type=F32)
    qt = lax.dot_general(wqt_ref[...], hb, _NT, preferred_element_type=F32)
    for h, s in enumerate(_slab_norm_t(qt, MEM_HEADS, HEAD_DIM ** -0.5)):
        qt_ref[0, HEAD_SLAB * h:HEAD_SLAB * (h + 1), :] = s.astype(BF16)


def _inproj0(x2d, g, wm, wqt, batch, seq, tm):
    t, d = x2d.shape
    per_b = seq // tm
    nm = wm.shape[1]
    nq = wqt.shape[0]
    return pl.pallas_call(
        _inproj0_kernel,
        grid=(t // tm,),
        in_specs=[
            pl.BlockSpec((tm, d), lambda i: (i, 0)),
            pl.BlockSpec((1, d), lambda i: (0, 0)),
            pl.BlockSpec((d, nm), lambda i: (0, 0)),
            pl.BlockSpec((nq, d), lambda i: (0, 0)),
        ],
        out_specs=[
            pl.BlockSpec((tm, nm), lambda i: (i, 0)),
            pl.BlockSpec((1, nq, tm), lambda i: (i // per_b, 0, i % per_b)),
        ],
        out_shape=[
            jax.ShapeDtypeStruct((t, nm), F32),
            jax.ShapeDtypeStruct((batch, nq, seq), BF16),
        ],
        compiler_params=_cparams(("parallel",)),
        name="inproj0",
    )(x2d, g, wm, wqt)


def _mem_kv_kernel(mem_ref, g_ref, wk_ref, wvt_ref, ones_ref, kg_ref, k_ref, vt_ref):
    mb = _rms(mem_ref[0], g_ref[...]).astype(BF16)
    k = jnp.dot(mb, wk_ref[...], preferred_element_type=F32)
    ms = jnp.dot(k * k, ones_ref[...], preferred_element_type=F32,
                 precision=lax.Precision.HIGHEST) * (1.0 / HEAD_DIM)
    k_ref[0] = (k * lax.rsqrt(ms + RMS_EPS) * kg_ref[...]).astype(BF16)
    vt_ref[0] = lax.dot_general(wvt_ref[...], mb, _NT, preferred_element_type=F32).astype(BF16)


def _mem_kv(mem, g, wk, wvt, ones, kg):
    b, m, d = mem.shape
    w = wk.shape[1]
    return pl.pallas_call(
        _mem_kv_kernel,
        grid=(b,),
        in_specs=[
            pl.BlockSpec((1, m, d), lambda i: (i, 0, 0)),
            pl.BlockSpec((1, d), lambda i: (0, 0)),
            pl.BlockSpec((d, w), lambda i: (0, 0)),
            pl.BlockSpec((w, d), lambda i: (0, 0)),
            pl.BlockSpec((w, w), lambda i: (0, 0)),
            pl.BlockSpec((1, w), lambda i: (0, 0)),
        ],
        out_specs=[
            pl.BlockSpec((1, m, w), lambda i: (i, 0, 0)),
            pl.BlockSpec((1, w, m), lambda i: (i, 0, 0)),
        ],
        out_shape=[
            jax.ShapeDtypeStruct((b, m, w), BF16),
            jax.ShapeDtypeStruct((b, w, m), BF16),
        ],
        compiler_params=_cparams(("parallel",)),
        name="mem_kv",
    )(mem, g, wk, wvt, ones, kg)


def _mem_attn_kernel(qt_ref, k_ref, vt_ref, o_ref):
    for h in range(MEM_HEADS):
        pair = h // 2
        k2 = k_ref[0, :, HEAD_SLAB * pair:HEAD_SLAB * (pair + 1)]
        q = qt_ref[0, HEAD_SLAB * h:HEAD_SLAB * (h + 1), :]
        s = jnp.dot(k2, q, preferred_element_type=F32)
        m = jnp.max(s, axis=0, keepdims=True)
        p = jnp.exp(s - m)
        l = jnp.sum(p, axis=0, keepdims=True)
        v = vt_ref[0, HEAD_DIM * h:HEAD_DIM * (h + 1), :]
        o = jnp.dot(v, p.astype(BF16), preferred_element_type=F32)
        o_ref[0, HEAD_DIM * h:HEAD_DIM * (h + 1), :] = (o / l).astype(BF16)


def _mem_attn(qt, k, vt, tq):
    b, nq, s = qt.shape
    m, w = k.shape[1], k.shape[2]
    return pl.pallas_call(
        _mem_attn_kernel,
        grid=(b, s // tq),
        in_specs=[
            pl.BlockSpec((1, nq, tq), lambda i, j: (i, 0, j)),
            pl.BlockSpec((1, m, w), lambda i, j: (i, 0, 0)),
            pl.BlockSpec((1, w, m), lambda i, j: (i, 0, 0)),
        ],
        out_specs=pl.BlockSpec((1, w, tq), lambda i, j: (i, 0, j)),
        out_shape=jax.ShapeDtypeStruct((b, w, s), BF16),
        compiler_params=_cparams(("parallel", "parallel")),
        name="mem_attn",
    )(qt, k, vt)


def _s5_kernel(u_ref, bd_ref, cd_ref, ar_ref, ai_ref, d_ref, wglu_ref, bglu_ref, o_ref,
               buf_ref, st_ref, *, ts, n_slabs):
    half = S5_SLAB_STATES
    width = 2 * half

    @pl.when(pl.program_id(0) == 0)
    def _():
        st_ref[...] = jnp.zeros_like(st_ref)

    u = u_ref[...]
    ub = u.astype(BF16)
    for j in range(n_slabs):
        buf_ref[:, width * j:width * (j + 1)] = jnp.dot(
            ub[:, HEAD_SLAB * j:HEAD_SLAB * (j + 1)], bd_ref[j], preferred_element_type=F32)

    for j in range(n_slabs):
        re = slice(width * j, width * j + half)
        im = slice(width * j + half, width * (j + 1))
        ar = jnp.broadcast_to(ar_ref[j:j + 1, :], (SUBLANES, half))
        ai = jnp.broadcast_to(ai_ref[j:j + 1, :], (SUBLANES, half))

        def step(t, carry, re=re, im=im, ar=ar, ai=ai):
            xr, xi = carry
            rows = pl.ds(pl.multiple_of(t * SUBLANES, SUBLANES), SUBLANES)
            nxr = ar * xr - ai * xi + buf_ref[rows, re]
            nxi = ar * xi + ai * xr + buf_ref[rows, im]
            buf_ref[rows, re] = nxr
            buf_ref[rows, im] = nxi
            return nxr, nxi

        xr, xi = lax.fori_loop(0, ts, step, (st_ref[:, re], st_ref[:, im]), unroll=8)
        st_ref[:, re] = xr
        st_ref[:, im] = xi

    ys = []
    for j in range(n_slabs):
        ys.append(jnp.dot(buf_ref[:, width * j:width * (j + 1)].astype(BF16), cd_ref[j],
                          preferred_element_type=F32))
    y = jnp.concatenate(ys, axis=1) + d_ref[...] * u
    g = jax.nn.gelu(y)
    z = jnp.dot(g.astype(BF16), wglu_ref[...], preferred_element_type=F32) + bglu_ref[...]
    o_ref[...] = (g / (1.0 + jnp.exp(-z))).astype(BF16)


def _s5(u_tm, bd, cd, ar, ai, dskip, wglu, bglu, batch, ts):
    rows, c = u_tm.shape
    n_slabs = c // HEAD_SLAB
    width = 2 * S5_SLAB_STATES
    tile = ts * batch
    return pl.pallas_call(
        functools.partial(_s5_kernel, ts=ts, n_slabs=n_slabs),
        grid=(rows // tile,),
        in_specs=[
            pl.BlockSpec((tile, c), lambda i: (i, 0)),
            pl.BlockSpec((n_slabs, HEAD_SLAB, width), lambda i: (0, 0, 0)),
            pl.BlockSpec((n_slabs, width, HEAD_SLAB), lambda i: (0, 0, 0)),
            pl.BlockSpec((n_slabs, S5_SLAB_STATES), lambda i: (0, 0)),
            pl.BlockSpec((n_slabs, S5_SLAB_STATES), lambda i: (0, 0)),
            pl.BlockSpec((1, c), lambda i: (0, 0)),
            pl.BlockSpec((c, c), lambda i: (0, 0)),
            pl.BlockSpec((1, c), lambda i: (0, 0)),
        ],
        out_specs=pl.BlockSpec((tile, c), lambda i: (i, 0)),
        out_shape=jax.ShapeDtypeStruct((rows, c), BF16),
        scratch_shapes=[
            pltpu.VMEM((tile, n_slabs * width), F32),
            pltpu.VMEM((batch, n_slabs * width), F32),
        ],
        compiler_params=_cparams(("arbitrary",)),
        name="s5_mixer",
    )(u_tm, bd, cd, ar, ai, dskip, wglu, bglu)


def _outproj_body(a, a_transposed, bt_ref, wa_ref, wb_ref, x_ref, g_ref):
    dn = _TN if a_transposed else (((1,), (0,)), ((), ()))
    y = lax.dot_general(a, wa_ref[...], dn, preferred_element_type=F32)
    y = y + lax.dot_general(bt_ref[0], wb_ref[...], _TN, preferred_element_type=F32)
    xn = x_ref[...] + y
    return xn, _rms(xn, g_ref[...])


def _outproj0_kernel(a_ref, bt_ref, wa_ref, wb_ref, x_ref, g_ref, xo_ref, h_ref):
    xn, h = _outproj_body(a_ref[...], False, bt_ref, wa_ref, wb_ref, x_ref, g_ref)
    xo_ref[...] = xn
    h_ref[...] = h.astype(BF16)


def _outproj0(a, bt, wa, wb, x2d, g, seq, tm):
    t, d = x2d.shape
    per_b = seq // tm
    ka, kb = a.shape[1], bt.shape[1]
    return pl.pallas_call(
        _outproj0_kernel,
        grid=(t // tm,),
        in_specs=[
            pl.BlockSpec((tm, ka), lambda i: (i, 0)),
            pl.BlockSpec((1, kb, tm), lambda i: (i // per_b, 0, i % per_b)),
            pl.BlockSpec((ka, d), lambda i: (0, 0)),
            pl.BlockSpec((kb, d), lambda i: (0, 0)),
            pl.BlockSpec((tm, d), lambda i: (i, 0)),
            pl.BlockSpec((1, d), lambda i: (0, 0)),
        ],
        out_specs=[
            pl.BlockSpec((tm, d), lambda i: (i, 0)),
            pl.BlockSpec((tm, d), lambda i: (i, 0)),
        ],
        out_shape=[
            jax.ShapeDtypeStruct((t, d), F32),
            jax.ShapeDtypeStruct((t, d), BF16),
        ],
        compiler_params=_cparams(("parallel",)),
        name="outproj0",
    )(a, bt, wa, wb, x2d, g)


def _outproj1_kernel(at_ref, bt_ref, wa_ref, wb_ref, x_ref, g_ref, wrh_ref, wrl_ref,
                     xo_ref, h_ref, rec_ref, cnt_ref, run_ref, *, tm):
    i = pl.program_id(0)

    @pl.when(i == 0)
    def _():
        run_ref[...] = jnp.zeros_like(run_ref)

    xn, h = _outproj_body(at_ref[0], True, bt_ref, wa_ref, wb_ref, x_ref, g_ref)
    xo_ref[...] = xn
    _to_row_tiles(h_ref, h)

    h_hi = h.astype(BF16)
    h_lo = (h - h_hi.astype(F32)).astype(BF16)
    logits = (lax.dot_general(wrh_ref[...], h_hi, _NT, preferred_element_type=F32)
              + lax.dot_general(wrh_ref[...], h_lo, _NT, preferred_element_type=F32)
              + lax.dot_general(wrl_ref[...], h_hi, _NT, preferred_element_type=F32))
    ex = lax.broadcasted_iota(I32, (ROUTER_ROWS, tm), 0)
    logits = jnp.where(ex < N_EXPERTS, logits, -jnp.inf)
    m1 = jnp.max(logits, axis=0, keepdims=True)
    e1 = jnp.min(jnp.where(logits == m1, ex, ROUTER_ROWS), axis=0, keepdims=True)
    rest = jnp.where(ex == e1, -jnp.inf, logits)
    m2 = jnp.max(rest, axis=0, keepdims=True)
    e2 = jnp.min(jnp.where(rest == m2, ex, ROUTER_ROWS), axis=0, keepdims=True)
    w2 = jnp.exp(m2 - m1)
    g1 = 1.0 / (1.0 + w2)
    g2 = w2 / (1.0 + w2)

    hot = jnp.where((ex == e1) | (ex == e2), 1.0, 0.0)
    r_io = lax.broadcasted_iota(I32, (tm, tm), 0)
    c_io = lax.broadcasted_iota(I32, (tm, tm), 1)
    triu = jnp.where(r_io < c_io, 1.0, 0.0).astype(BF16)
    before = (jnp.dot(hot.astype(BF16), triu, preferred_element_type=F32)
              + run_ref[:, 0:1])
    r1 = jnp.sum(jnp.where(ex == e1, before, 0.0), axis=0, keepdims=True)
    r2 = jnp.sum(jnp.where(ex == e2, before, 0.0), axis=0, keepdims=True)
    run = run_ref[...] + jnp.sum(hot, axis=1, keepdims=True)
    run_ref[...] = run
    cnt_ref[...] = run

    slot = lax.broadcasted_iota(I32, (SUBLANES, tm), 0)
    rec = jnp.where(slot == 0, e1.astype(F32), 0.0)
    rec = jnp.where(slot == 1, e2.astype(F32), rec)
    rec = jnp.where(slot == 2, r1, rec)
    rec = jnp.where(slot == 3, r2, rec)
    rec = jnp.where(slot == 4, g1, rec)
    rec = jnp.where(slot == 5, g2, rec)
    rec_ref[...] = rec


def _outproj1(at, bt, wa, wb, x2d, g, wr_hi, wr_lo, seq, tm):
    t, d = x2d.shape
    per_b = seq // tm
    ka, kb = at.shape[1], bt.shape[1]
    return pl.pallas_call(
        functools.partial(_outproj1_kernel, tm=tm),
        grid=(t // tm,),
        in_specs=[
            pl.BlockSpec((1, ka, tm), lambda i: (i // per_b, 0, i % per_b)),
            pl.BlockSpec((1, kb, tm), lambda i: (i // per_b, 0, i % per_b)),
            pl.BlockSpec((ka, d), lambda i: (0, 0)),
            pl.BlockSpec((kb, d), lambda i: (0, 0)),
            pl.BlockSpec((tm, d), lambda i: (i, 0)),
            pl.BlockSpec((1, d), lambda i: (0, 0)),
            pl.BlockSpec((ROUTER_ROWS, d), lambda i: (0, 0)),
            pl.BlockSpec((ROUTER_ROWS, d), lambda i: (0, 0)),
        ],
        out_specs=[
            pl.BlockSpec((tm, d), lambda i: (i, 0)),
            pl.BlockSpec((tm * ROW_TILE, LANES), lambda i: (i, 0)),
            pl.BlockSpec((SUBLANES, tm), lambda i: (0, i)),
            pl.BlockSpec((ROUTER_ROWS, ROUTER_LANES), lambda i: (0, 0)),
        ],
        out_shape=[
            jax.ShapeDtypeStruct((t, d), F32),
            jax.ShapeDtypeStruct((t * ROW_TILE, LANES), F32),
            jax.ShapeDtypeStruct((SUBLANES, t), F32),
            jax.ShapeDtypeStruct((ROUTER_ROWS, ROUTER_LANES), F32),
        ],
        scratch_shapes=[pltpu.VMEM((ROUTER_ROWS, ROUTER_LANES), F32)],
        compiler_params=_cparams(("arbitrary",)),
        name="outproj1_router",
    )(at, bt, wa, wb, x2d, g, wr_hi, wr_lo)


def _ffn_kernel(h_ref, res_ref, wg_ref, wu_ref, wd_ref, o_ref, acc_ref):
    f = pl.program_id(1)

    @pl.when(f == 0)
    def _():
        acc_ref[...] = jnp.zeros_like(acc_ref)

    half = h_ref.shape[0] // 2
    for rows in (slice(0, half), slice(half, 2 * half)):
        hb = h_ref[rows, :]
        gt = jnp.dot(hb, wg_ref[...], preferred_element_type=F32)
        up = jnp.dot(hb, wu_ref[...], preferred_element_type=F32)
        mid = (gt / (1.0 + jnp.exp(-gt))) * up
        acc_ref[rows, :] += jnp.dot(mid.astype(BF16), wd_ref[...], preferred_element_type=F32)

    @pl.when(f == pl.num_programs(1) - 1)
    def _():
        o_ref[...] = acc_ref[...] + res_ref[...]


def _ffn(h, res, wg, wu, wd, tm, tf):
    rows, d = h.shape
    ff = wg.shape[1]
    return pl.pallas_call(
        _ffn_kernel,
        grid=(rows // tm, ff // tf),
        in_specs=[
            pl.BlockSpec((tm, d), lambda i, f: (i, 0)),
            pl.BlockSpec((tm, d), lambda i, f: (i, 0)),
            pl.BlockSpec((d, tf), lambda i, f: (0, f)),
            pl.BlockSpec((d, tf), lambda i, f: (0, f)),
            pl.BlockSpec((tf, d), lambda i, f: (f, 0)),
        ],
        out_specs=pl.BlockSpec((tm, d), lambda i, f: (i, 0)),
        out_shape=jax.ShapeDtypeStruct((rows, d), F32),
        scratch_shapes=[pltpu.VMEM((tm, d), F32)],
        compiler_params=_cparams(("parallel", "arbitrary")),
        name="ffn_dense",
    )(h, res, wg, wu, wd)


def _from_row_tiles(ref, base, n_rows, n_feat):
    parts = [ref[pl.ds(base + k, n_rows, stride=ROW_TILE), :] for k in range(n_feat // LANES)]
    return jnp.concatenate(parts, axis=1)


def _to_row_tiles(ref, value):
    for k in range(value.shape[1] // LANES):
        ref[pl.ds(k, value.shape[0], stride=ROW_TILE), :] = value[:, LANES * k:LANES * (k + 1)]


def _moe_ffn_kernel(te_ref, nv_ref, nu_ref, xs_ref, wg_ref, wu_ref, wd_ref, o_ref,
                    acc_ref, xb_ref, *, tm):
    del te_ref
    i = pl.program_id(0)
    f = pl.program_id(1)
    nf = pl.num_programs(1)
    half = tm // 2
    d = xb_ref.shape[1]

    def partial_out(rows):
        xb = xb_ref[rows, :]
        gt = jnp.dot(xb, wg_ref[0].astype(BF16), preferred_element_type=F32)
        up = jnp.dot(xb, wu_ref[0].astype(BF16), preferred_element_type=F32)
        mid = (gt / (1.0 + jnp.exp(-gt))) * up
        acc_ref[rows, :] += jnp.dot(mid.astype(BF16), wd_ref[0].astype(BF16),
                                    preferred_element_type=F32)

    @pl.when(i < nu_ref[0])
    def _():
        @pl.when(f == 0)
        def _():
            acc_ref[...] = jnp.zeros_like(acc_ref)
            xb_ref[...] = _from_row_tiles(xs_ref, 0, tm, d).astype(BF16)

        @pl.when(nv_ref[i] > half)
        def _():
            partial_out(slice(0, half))
            partial_out(slice(half, tm))

        @pl.when(nv_ref[i] <= half)
        def _():
            partial_out(slice(0, half))

        @pl.when(f == nf - 1)
        def _():
            _to_row_tiles(o_ref, acc_ref[...])

    @pl.when((i >= nu_ref[0]) & (f == nf - 1))
    def _():
        o_ref[...] = jnp.zeros_like(o_ref)


def _moe_ffn(tile_expert, n_valid, n_used, xs_rt, wg, wu, wd, tm, tf):
    d = wg.shape[1]
    ff = wg.shape[2]
    nf = ff // tf
    n_tiles = xs_rt.shape[0] // (tm * ROW_TILE)

    def live(i, nu):
        return jnp.minimum(i, jnp.maximum(nu[0] - 1, 0))

    def f_of(i, f, nu):
        return jnp.where(i < nu[0], f, nf - 1)

    def wcol_map(i, f, te, nv, nu):
        return (te[live(i, nu)], 0, f_of(i, f, nu))

    def wrow_map(i, f, te, nv, nu):
        return (te[live(i, nu)], f_of(i, f, nu), 0)

    grid_spec = pltpu.PrefetchScalarGridSpec(
        num_scalar_prefetch=3,
        grid=(n_tiles, nf),
        in_specs=[
            pl.BlockSpec((tm * ROW_TILE, LANES), lambda i, f, te, nv, nu: (live(i, nu), 0)),
            pl.BlockSpec((1, d, tf), wcol_map),
            pl.BlockSpec((1, d, tf), wcol_map),
            pl.BlockSpec((1, tf, d), wrow_map),
        ],
        out_specs=pl.BlockSpec((tm * ROW_TILE, LANES), lambda i, f, te, nv, nu: (i, 0)),
        scratch_shapes=[pltpu.VMEM((tm, d), F32), pltpu.VMEM((tm, d), BF16)],
    )
    return pl.pallas_call(
        functools.partial(_moe_ffn_kernel, tm=tm),
        grid_spec=grid_spec,
        out_shape=jax.ShapeDtypeStruct(xs_rt.shape, F32),
        compiler_params=_cparams(("arbitrary", "arbitrary")),
        name="ffn_moe",
    )(tile_expert, n_valid, n_used, xs_rt, wg, wu, wd)


def _proj1_kernel(x_ref, g1_ref, g2_ref, wqt_ref, wk_ref, wvt_ref, kg_ref, qg_ref,
                  qt_ref, qmt_ref, k_ref, km_ref, vt_ref, *, tm, n_q, per_b):
    x = x_ref[...]
    h1 = _rms(x, g1_ref[...]).astype(BF16)
    h2 = _rms(x, g2_ref[...]).astype(BF16)
    qt = lax.dot_general(wqt_ref[...], h1, _NT, preferred_element_type=F32)

    for h in range(n_q):
        s = qt[HEAD_DIM * h:HEAD_DIM * (h + 1), :]
        ms = jnp.mean(s * s, axis=0, keepdims=True)
        qt_ref[0, HEAD_DIM * h:HEAD_DIM * (h + 1), :] = (
            s * (lax.rsqrt(ms + RMS_EPS) * (HEAD_DIM ** -0.5 * LOG2_E))).astype(BF16)
    qm = qt[HEAD_DIM * n_q:, :]
    for h, s in enumerate(_slab_norm_t(qm, MEM_HEADS, HEAD_DIM ** -0.5)):
        qmt_ref[0, HEAD_SLAB * h:HEAD_SLAB * (h + 1), :] = s.astype(BF16)

    k = jnp.dot(h2, wk_ref[...], preferred_element_type=F32)
    nblk = tm // MOBA_BLOCK
    first_blk = lax.rem(pl.program_id(0), per_b) * nblk
    lane = lax.broadcasted_iota(I32, (MOBA_BLOCK, HEAD_SLAB), 1)
    for kv in range(MOBA_KV_HEADS):
        cols = slice(HEAD_SLAB * kv, HEAD_SLAB * (kv + 1))
        ks = k[:, cols]
        ms = jnp.sum(ks * ks, axis=-1, keepdims=True) * (1.0 / HEAD_DIM)
        kn = ks * lax.rsqrt(ms + RMS_EPS) * kg_ref[:, cols]
        kq = kn * qg_ref[:, cols]
        for j in range(nblk):
            rows = slice(MOBA_BLOCK * j, MOBA_BLOCK * (j + 1))
            km_ref[j, :, cols] = jnp.mean(kn[rows], axis=0, keepdims=True) * qg_ref[:, cols]
            k_ref[rows, cols] = jnp.where(lane == HEAD_DIM + first_blk + j, 1.0,
                                          kq[rows]).astype(BF16)

    vt = lax.dot_general(wvt_ref[...], h2, _NT, preferred_element_type=F32)
    ones_row = jnp.where(lax.broadcasted_iota(I32, (V_ROWS - HEAD_DIM, tm), 0) == 0, 1.0, 0.0)
    for kv in range(MOBA_KV_HEADS):
        vt_ref[0, V_ROWS * kv:V_ROWS * kv + HEAD_DIM, :] = (
            vt[HEAD_DIM * kv:HEAD_DIM * (kv + 1), :].astype(BF16))
        vt_ref[0, V_ROWS * kv + HEAD_DIM:V_ROWS * (kv + 1), :] = ones_row.astype(BF16)


def _proj1(x2d, g1, g2, wqt, wk, wvt, kg, qg, batch, seq, tm):
    t, d = x2d.shape
    per_b = seq // tm
    nq_rows = wqt.shape[0]
    n_q = (nq_rows - MEM_HEADS * HEAD_SLAB) // HEAD_DIM
    kw = wk.shape[1]
    vw = wvt.shape[0]
    v_rows = MOBA_KV_HEADS * V_ROWS
    nblk = tm // MOBA_BLOCK
    return pl.pallas_call(
        functools.partial(_proj1_kernel, tm=tm, n_q=n_q, per_b=per_b),
        grid=(t // tm,),
        in_specs=[
            pl.BlockSpec((tm, d), lambda i: (i, 0)),
            pl.BlockSpec((1, d), lambda i: (0, 0)),
            pl.BlockSpec((1, d), lambda i: (0, 0)),
            pl.BlockSpec((nq_rows, d), lambda i: (0, 0)),
            pl.BlockSpec((d, kw), lambda i: (0, 0)),
            pl.BlockSpec((vw, d), lambda i: (0, 0)),
            pl.BlockSpec((1, kw), lambda i: (0, 0)),
            pl.BlockSpec((1, kw), lambda i: (0, 0)),
        ],
        out_specs=[
            pl.BlockSpec((1, n_q * HEAD_DIM, tm), lambda i: (i // per_b, 0, i % per_b)),
            pl.BlockSpec((1, MEM_HEADS * HEAD_SLAB, tm), lambda i: (i // per_b, 0, i % per_b)),
            pl.BlockSpec((tm, kw), lambda i: (i, 0)),
            pl.BlockSpec((nblk, 1, kw), lambda i: (i, 0, 0)),
            pl.BlockSpec((1, v_rows, tm), lambda i: (i // per_b, 0, i % per_b)),
        ],
        out_shape=[
            jax.ShapeDtypeStruct((batch, n_q * HEAD_DIM, seq), BF16),
            jax.ShapeDtypeStruct((batch, MEM_HEADS * HEAD_SLAB, seq), BF16),
            jax.ShapeDtypeStruct((t, kw), BF16),
            jax.ShapeDtypeStruct((t // MOBA_BLOCK, 1, kw), F32),
            jax.ShapeDtypeStruct((batch, v_rows, seq), BF16),
        ],
        compiler_params=_cparams(("parallel",)),
        name="proj1",
    )(x2d, g1, g2, wqt, wk, wvt, kg, qg)


def _moba_kernel(qt_ref, k_ref, vt_ref, km_ref, o_ref, s_ref):
    qb = pl.program_id(2)
    nb = km_ref.shape[1]
    bq = MOBA_BLOCK
    nq = MOBA_GROUP * bq
    q64 = jnp.concatenate(
        [qt_ref[0, HEAD_DIM * g:HEAD_DIM * (g + 1), :] for g in range(MOBA_GROUP)], axis=1)

    blk = lax.broadcasted_iota(I32, (nb, nq), 0)
    km = km_ref[0, :, 0:HEAD_DIM]
    km_hi = km.astype(BF16)
    km_lo = (km - km_hi.astype(F32)).astype(BF16)
    gate = (jnp.dot(km_hi, q64, preferred_element_type=F32)
            + jnp.dot(km_lo, q64, preferred_element_type=F32))
    gate = jnp.where(blk < qb, gate, NEG_INF)
    rank = jnp.zeros((nb, nq), F32)
    for m in range(nb):
        gm = gate[m:m + 1, :]
        rank = rank + jnp.where(gm > gate, 1.0, jnp.where((gm == gate) & (blk > m), 1.0, 0.0))
    dropped = (blk < qb) & (rank >= MOBA_TOP_BLOCKS)

    bias = jnp.where(dropped, NEG_INF, 0.0)
    pad_rows = HEAD_SLAB - HEAD_DIM - 2 * nb
    q3 = jnp.concatenate(
        [q64, jnp.concatenate([bias, jnp.zeros((nb, nq), F32)], axis=0).astype(BF16),
         jnp.zeros((pad_rows, nq), BF16)], axis=0)

    key_pos = lax.broadcasted_iota(I32, (bq, nq), 0)
    qry_pos = lax.broadcasted_iota(I32, (bq, nq), 1) & (bq - 1)
    causal = key_pos <= qry_pos

    for n_blocks in range(1, nb + 1):
        @pl.when(qb == n_blocks - 1)
        def _(n_blocks=n_blocks):
            mx = None
            for n in range(n_blocks):
                s = jnp.dot(k_ref[0, bq * n:bq * (n + 1), :], q3, preferred_element_type=F32)
                if n == n_blocks - 1:
                    s = jnp.where(causal, s, NEG_INF)
                s_ref[n] = s
                cm = jnp.max(s, axis=0, keepdims=True)
                mx = cm if mx is None else jnp.maximum(mx, cm)
            acc = jnp.zeros((V_ROWS, nq), F32)
            for n in range(n_blocks):
                p = jnp.exp2(s_ref[n] - mx).astype(BF16)
                acc = acc + jnp.dot(vt_ref[0, :, bq * n:bq * (n + 1)], p,
                                    preferred_element_type=F32)
            out = acc[0:HEAD_DIM, :] / acc[HEAD_DIM:HEAD_DIM + 1, :]
            for g in range(MOBA_GROUP):
                o_ref[0, HEAD_DIM * g:HEAD_DIM * (g + 1), :] = (
                    out[:, bq * g:bq * (g + 1)].astype(BF16))


def _moba(qt, k3, vt, km3):
    b, nq_rows, s = qt.shape
    nb = s // MOBA_BLOCK
    rows = MOBA_GROUP * HEAD_DIM
    return pl.pallas_call(
        _moba_kernel,
        grid=(b, MOBA_KV_HEADS, nb),
        in_specs=[
            pl.BlockSpec((1, rows, MOBA_BLOCK), lambda i, j, q: (i, j, q)),
            pl.BlockSpec((1, s, HEAD_SLAB), lambda i, j, q: (i, 0, j)),
            pl.BlockSpec((1, V_ROWS, s), lambda i, j, q: (i, j, 0)),
            pl.BlockSpec((1, nb, HEAD_SLAB), lambda i, j, q: (i, 0, j)),
        ],
        out_specs=pl.BlockSpec((1, rows, MOBA_BLOCK), lambda i, j, q: (i, j, q)),
        out_shape=jax.ShapeDtypeStruct((b, MOBA_KV_HEADS * rows, s), BF16),
        scratch_shapes=[pltpu.VMEM((nb, MOBA_BLOCK, MOBA_GROUP * MOBA_BLOCK), F32)],
        compiler_params=_cparams(("parallel", "parallel", "arbitrary")),
        name="moba_attn",
    )(qt, k3, vt, km3)


def _row_copy(src_ref, src_row, dst_ref, dst_row, sem):
    def tile(ref, r):
        return ref.at[pl.ds(pl.multiple_of(r * ROW_TILE, ROW_TILE), ROW_TILE)]

    return pltpu.make_async_copy(tile(src_ref, src_row), tile(dst_ref, dst_row), sem)


def _for_each_row(n_rows, fn):
    def body(j8, carry):
        for u in range(DMA_UNROLL):
            fn(j8 * DMA_UNROLL + u, u % 2)
        return carry

    lax.fori_loop(0, n_rows // DMA_UNROLL, body, 0)


def _dispatch_kernel(dest_ref, h_ref, init_ref, xs_ref, sem, *, tm):
    del init_ref

    def copy(j, slot):
        return _row_copy(h_ref, j, xs_ref, dest_ref[0, 0, slot * tm + j], sem)

    for slot in range(2):
        _for_each_row(tm, lambda j, lane, slot=slot: copy(j, slot).start(priority=lane))
    for slot in range(2):
        _for_each_row(tm, lambda j, lane, slot=slot: copy(j, slot).wait())


def _dispatch(dest3, h_rt, xs_init, tm):
    return pl.pallas_call(
        functools.partial(_dispatch_kernel, tm=tm),
        grid=(h_rt.shape[0] // (tm * ROW_TILE),),
        in_specs=[
            pl.BlockSpec((1, 1, 2 * tm), lambda i: (i, 0, 0), memory_space=pltpu.SMEM),
            pl.BlockSpec((tm * ROW_TILE, LANES), lambda i: (i, 0)),
            pl.BlockSpec(memory_space=pl.ANY),
        ],
        out_specs=pl.BlockSpec(memory_space=pl.ANY),
        out_shape=jax.ShapeDtypeStruct(xs_init.shape, xs_init.dtype),
        scratch_shapes=[pltpu.SemaphoreType.DMA(())],
        input_output_aliases={2: 0},
        compiler_params=_cparams(("arbitrary",)),
        name="moe_dispatch",
    )(dest3, h_rt, xs_init)


def _combine_kernel(dest_ref, next_ref, ys_ref, x_ref, gate_ref, o_ref, buf_ref, sem, *, tm):
    i = pl.program_id(0)
    cur = lax.rem(i, 2)

    def copy(idx_ref, buf_slot, j):
        return _row_copy(ys_ref, idx_ref[0, 0, j], buf_ref.at[buf_slot], j, sem.at[buf_slot])

    @pl.when(i == 0)
    def _():
        _for_each_row(2 * tm, lambda j, lane: copy(dest_ref, 0, j).start(priority=lane))

    @pl.when(i + 1 < pl.num_programs(0))
    def _():
        _for_each_row(2 * tm, lambda j, lane: copy(next_ref, 1 - cur, j).start(priority=lane))

    _for_each_row(2 * tm, lambda j, lane: copy(dest_ref, cur, j).wait())
    g1 = gate_ref[:, 0:1]
    g2 = gate_ref[:, 1:2]
    d = x_ref.shape[1]
    for slot in range(2):
        @pl.when(cur == slot)
        def _(slot=slot):
            rows = buf_ref.at[slot]
            y1 = _from_row_tiles(rows, 0, tm, d)
            y2 = _from_row_tiles(rows, tm * ROW_TILE, tm, d)
            o_ref[...] = x_ref[...] + (g1 * y1 + g2 * y2)


def _combine(dest3, ys, x2d, gates, tm):
    t, d = x2d.shape
    n = t // tm
    return pl.pallas_call(
        functools.partial(_combine_kernel, tm=tm),
        grid=(n,),
        in_specs=[
            pl.BlockSpec((1, 1, 2 * tm), lambda i: (i, 0, 0), memory_space=pltpu.SMEM),
            pl.BlockSpec((1, 1, 2 * tm), lambda i: (jnp.minimum(i + 1, n - 1), 0, 0),
                         memory_space=pltpu.SMEM),
            pl.BlockSpec(memory_space=pl.ANY),
            pl.BlockSpec((tm, d), lambda i: (i, 0)),
            pl.BlockSpec((tm, gates.shape[1]), lambda i: (i, 0)),
        ],
        out_specs=pl.BlockSpec((tm, d), lambda i: (i, 0)),
        out_shape=jax.ShapeDtypeStruct((t, d), F32),
        scratch_shapes=[pltpu.VMEM((2, 2 * tm * ROW_TILE, LANES), F32),
                        pltpu.SemaphoreType.DMA((2,))],
        compiler_params=_cparams(("arbitrary",)),
        name="moe_combine",
    )(dest3, dest3, ys, x2d, gates)


def _pad_heads_t(w_cols, offsets):
    d = w_cols.shape[0]
    nh = w_cols.shape[1] // HEAD_DIM
    wt = w_cols.T.reshape(nh, HEAD_DIM, d)
    hi = jnp.asarray(offsets, I32).reshape(nh, 1, 1) > 0
    z = jnp.zeros_like(wt)
    slab = jnp.concatenate([jnp.where(hi, z, wt), jnp.where(hi, wt, z)], axis=1)
    return slab.reshape(nh * HEAD_SLAB, d)


def _pad_head_cols(w):
    rows = w.shape[0]
    nh = w.shape[1] // HEAD_DIM
    w3 = w.reshape(rows, nh, HEAD_DIM)
    return jnp.concatenate([w3, jnp.zeros_like(w3)], axis=2).reshape(rows, nh * HEAD_SLAB)


def _head_block_ones(width):
    idx = jnp.arange(width) // HEAD_DIM
    return (idx[:, None] == idx[None, :]).astype(F32)


def _tile_gain(g, n_heads):
    return jnp.tile(g.astype(F32), n_heads).reshape(1, n_heads * HEAD_DIM)


def _s5_params(lam_re, lam_im, log_dt, b_re, b_im, c_re, c_im):
    f32 = F32
    g, p = lam_re.shape
    n = b_re.shape[2]
    lam = lax.complex(lam_re.astype(f32), lam_im.astype(f32))
    dt = jnp.exp(log_dt.astype(f32))[:, None]
    lam_bar = jnp.exp(lam * dt)
    b_bar = ((lam_bar - 1.0) / lam)[..., None] * lax.complex(b_re.astype(f32), b_im.astype(f32))
    ns = g // S5_GROUPS_PER_SLAB
    eye = jnp.eye(S5_GROUPS_PER_SLAB, dtype=f32)

    def in_map(part):
        blk = part.reshape(ns, S5_GROUPS_PER_SLAB, p, n).transpose(0, 1, 3, 2)
        return jnp.einsum('sgnp,gh->sgnhp', blk, eye).reshape(
            ns, S5_GROUPS_PER_SLAB * n, S5_GROUPS_PER_SLAB * p)

    def out_map(part):
        blk = part.reshape(ns, S5_GROUPS_PER_SLAB, n, p).transpose(0, 1, 3, 2)
        return jnp.einsum('sgpn,gh->sgphn', blk, eye).reshape(
            ns, S5_GROUPS_PER_SLAB * p, S5_GROUPS_PER_SLAB * n)

    bd = jnp.concatenate([in_map(jnp.real(b_bar)), in_map(jnp.imag(b_bar))], axis=2)
    cd = jnp.concatenate([out_map(c_re.astype(f32)), out_map(-c_im.astype(f32))], axis=1)
    ar = jnp.real(lam_bar).reshape(ns, S5_GROUPS_PER_SLAB * p)
    ai = jnp.imag(lam_bar).reshape(ns, S5_GROUPS_PER_SLAB * p)
    return bd.astype(BF16), cd.astype(BF16), ar, ai


def kernel(x, mem, l0_mix_norm, l0_w_in, l0_s5_lam_re, l0_s5_lam_im, l0_s5_log_dt, l0_s5_b_re, l0_s5_b_im, l0_s5_c_re, l0_s5_c_im, l0_s5_d, l0_s5_w_glu, l0_s5_b_glu, l0_mem_norm, l0_w_mem_k, l0_w_mem_v, l0_mem_q_gain, l0_mem_k_gain, l0_w_out, l0_ffn_norm, l0_ffn_w_gate, l0_ffn_w_up, l0_ffn_w_down, kv_norm, kv_w_k, kv_w_v, kv_k_gain, l1_mix_norm, l1_w_in, l1_moba_q_gain, l1_mem_norm, l1_w_mem_k, l1_w_mem_v, l1_mem_q_gain, l1_mem_k_gain, l1_w_out, l1_ffn_norm, l1_moe_router, l1_moe_w_gate, l1_moe_w_up, l1_moe_w_down):
    batch, seq, d = x.shape
    t = batch * seq
    main_w = l0_s5_w_glu.shape[0]
    mem_w = l0_w_mem_k.shape[1]
    n_q = main_w // HEAD_DIM
    tm = 512
    row = lambda v: v.astype(F32).reshape(1, -1)

    mem_off = [HEAD_DIM * (h % 2) for h in range(MEM_HEADS)]
    ones_kv = _head_block_ones(mem_w)
    x2d = x.reshape(t, d)

    main_in, qm0_t = _inproj0(
        x2d, row(l0_mix_norm), l0_w_in[:, :main_w].astype(BF16),
        _pad_heads_t(l0_w_in[:, main_w:], mem_off).astype(BF16), batch, seq, tm)
    k0, v0_t = _mem_kv(
        mem, row(l0_mem_norm), l0_w_mem_k.astype(BF16), l0_w_mem_v.T.astype(BF16), ones_kv,
        _tile_gain(l0_mem_k_gain, MEM_HEADS) * _tile_gain(l0_mem_q_gain, MEM_HEADS))
    mem0_t = _mem_attn(qm0_t, k0, v0_t, tm)

    bd, cd, ar, ai = _s5_params(l0_s5_lam_re, l0_s5_lam_im, l0_s5_log_dt, l0_s5_b_re, l0_s5_b_im,
                                l0_s5_c_re, l0_s5_c_im)
    u_tm = main_in.reshape(batch, seq, main_w).transpose(1, 0, 2).reshape(t, main_w)
    s5_tm = _s5(u_tm, bd, cd, ar, ai, row(l0_s5_d), l0_s5_w_glu.astype(BF16), row(l0_s5_b_glu),
                batch, 64)
    s5_out = s5_tm.reshape(seq, batch, main_w).transpose(1, 0, 2).reshape(t, main_w)

    x1, h1 = _outproj0(s5_out, mem0_t, l0_w_out[:main_w].astype(BF16),
                       l0_w_out[main_w:].astype(BF16), x2d, row(l0_ffn_norm), seq, tm)
    x2 = _ffn(h1, x1, l0_ffn_w_gate.astype(BF16), l0_ffn_w_up.astype(BF16),
              l0_ffn_w_down.astype(BF16), tm, l0_ffn_w_gate.shape[1] // 2)

    wq1_t = jnp.concatenate([l1_w_in[:, :main_w].T,
                             _pad_heads_t(l1_w_in[:, main_w:], mem_off)], axis=0).astype(BF16)
    q_t, qm1_t, kq, km, v_t = _proj1(
        x2, row(l1_mix_norm), row(kv_norm), wq1_t, _pad_head_cols(kv_w_k).astype(BF16),
        kv_w_v.T.astype(BF16), _pad_head_cols(_tile_gain(kv_k_gain, MOBA_KV_HEADS)),
        _pad_head_cols(_tile_gain(l1_moba_q_gain, MOBA_KV_HEADS)), batch, seq, tm)
    nb = seq // MOBA_BLOCK
    moba_t = _moba(q_t, kq.reshape(batch, seq, -1), v_t, km.reshape(batch, nb, -1))
    k1, v1_t = _mem_kv(
        mem, row(l1_mem_norm), l1_w_mem_k.astype(BF16), l1_w_mem_v.T.astype(BF16), ones_kv,
        _tile_gain(l1_mem_k_gain, MEM_HEADS) * _tile_gain(l1_mem_q_gain, MEM_HEADS))
    mem1_t = _mem_attn(qm1_t, k1, v1_t, tm)

    wr_t = jnp.zeros((ROUTER_ROWS, d), F32).at[:N_EXPERTS].set(l1_moe_router.astype(F32).T)
    wr_hi = wr_t.astype(BF16)
    wr_lo = (wr_t - wr_hi.astype(F32)).astype(BF16)
    x3, h3, rec, cnt = _outproj1(moba_t, mem1_t, l1_w_out[:main_w].astype(BF16),
                                 l1_w_out[main_w:].astype(BF16), x2, row(l1_ffn_norm),
                                 wr_hi, wr_lo, seq, tm)

    assert d == ROW_TILE * LANES
    tme = 1024
    counts = cnt[:N_EXPERTS, 0].astype(I32)
    padded = ((counts + tme - 1) // tme) * tme
    ends = jnp.cumsum(padded)
    starts = ends - padded
    max_tiles = (2 * t) // tme + N_EXPERTS
    tile_ids = jnp.arange(max_tiles, dtype=I32)
    tile_expert = jnp.minimum(jnp.searchsorted(ends // tme, tile_ids, side='right'),
                              N_EXPERTS - 1).astype(I32)
    n_valid = jnp.clip(counts[tile_expert] - (tile_ids * tme - starts[tile_expert]), 0, tme)
    n_used = (ends[-1] // tme).astype(I32).reshape(1)
    e1 = rec[0].astype(I32)
    e2 = rec[1].astype(I32)
    d1 = starts[e1] + rec[2].astype(I32)
    d2 = starts[e2] + rec[3].astype(I32)
    gates = rec[4:6].T
    def tile_dest(tile):
        return jnp.concatenate([d1.reshape(t // tile, 1, tile), d2.reshape(t // tile, 1, tile)],
                               axis=2)

    tmd, tmc = 1024, 512
    xs = _dispatch(tile_dest(tmd), h3, jnp.zeros((max_tiles * tme * ROW_TILE, LANES), F32), tmd)
    ys = _moe_ffn(tile_expert, n_valid.astype(I32), n_used, xs, l1_moe_w_gate, l1_moe_w_up,
                  l1_moe_w_down, tme, 512)
    out = _combine(tile_dest(tmc), ys, x3, gates, tmc)
    return out.reshape(batch, seq, d)
```

```python
import functools

import jax
import jax.numpy as jnp
from jax import lax
from jax.experimental import pallas as pl
from jax.experimental.pallas import tpu as pltpu

F32 = jnp.float32
BF16 = jnp.bfloat16
I32 = jnp.int32

RMS_EPS = 1e-6
NEG_INF = -1e30
HEAD_DIM = 64
HEAD_SLAB = 128
MEM_HEADS = 4
MOBA_KV_HEADS = 4
MOBA_GROUP = 3
MOBA_BLOCK = 256
MOBA_TOP_BLOCKS = 3
MOBA_KV_PER_STEP = 2
S5_GROUP_DIM = 16
S5_STATE = 64
S5_GROUPS_PER_SLAB = HEAD_SLAB // S5_GROUP_DIM
S5_SLAB_STATES = S5_GROUPS_PER_SLAB * S5_STATE
N_EXPERTS = 8
ROUTER_LANES = 128
ROUTER_ROWS = 16
SUBLANES = 8
LANES = 128
ROW_TILE = SUBLANES
DMA_UNROLL = 8
LOG2_E = 1.4426950408889634
V_ROWS = HEAD_DIM + 16

VMEM_LIMIT_BYTES = 56 * 1024 * 1024

_NT = (((1,), (1,)), ((), ()))
_TN = (((0,), (0,)), ((), ()))


def _cparams(sem):
    return pltpu.CompilerParams(dimension_semantics=sem, vmem_limit_bytes=VMEM_LIMIT_BYTES)


def _rms(x, g):
    return x * lax.rsqrt(jnp.mean(x * x, axis=-1, keepdims=True) + RMS_EPS) * g


def _slab_norm_t(qt, n_heads, scale):
    outs = []
    for h in range(n_heads):
        s = qt[HEAD_SLAB * h:HEAD_SLAB * (h + 1), :]
        ms = jnp.sum(s * s, axis=0, keepdims=True) * (1.0 / HEAD_DIM)
        outs.append(s * (lax.rsqrt(ms + RMS_EPS) * scale))
    return outs


def _inproj0_kernel(x_ref, g_ref, wm_ref, wqt_ref, main_ref, qt_ref):
    hb = _rms(x_ref[...], g_ref[...]).astype(BF16)
    main_ref[...] = jnp.dot(hb, wm_ref[...], preferred_element_type=F32)
    qt = lax.dot_general(wqt_ref[...], hb, _NT, preferred_element_type=F32)
    for h, s in enumerate(_slab_norm_t(qt, MEM_HEADS, HEAD_DIM ** -0.5)):
        qt_ref[0, HEAD_SLAB * h:HEAD_SLAB * (h + 1), :] = s.astype(BF16)


def _inproj0(x2d, g, wm, wqt, batch, seq, tm):
    t, d = x2d.shape
    per_b = seq // tm
    nm = wm.shape[1]
    nq = wqt.shape[0]
    return pl.pallas_call(
        _inproj0_kernel,
        grid=(t // tm,),
        in_specs=[
            pl.BlockSpec((tm, d), lambda i: (i, 0)),
            pl.BlockSpec((1, d), lambda i: (0, 0)),
            pl.BlockSpec((d, nm), lambda i: (0, 0)),
            pl.BlockSpec((nq, d), lambda i: (0, 0)),
        ],
        out_specs=[
            pl.BlockSpec((tm, nm), lambda i: (i, 0)),
            pl.BlockSpec((1, nq, tm), lambda i: (i // per_b, 0, i % per_b)),
        ],
        out_shape=[
            jax.ShapeDtypeStruct((t, nm), F32),
            jax.ShapeDtypeStruct((batch, nq, seq), BF16),
        ],
        compiler_params=_cparams(("parallel",)),
        name="inproj0",
    )(x2d, g, wm, wqt)


def _mem_kv_kernel(mem_ref, g_ref, wk_ref, wvt_ref, ones_ref, kg_ref, k_ref, vt_ref):
    mb = _rms(mem_ref[0], g_ref[...]).astype(BF16)
    k = jnp.dot(mb, wk_ref[...], preferred_element_type=F32)
    ms = jnp.dot(k * k, ones_ref[...], preferred_element_type=F32,
                 precision=lax.Precision.HIGHEST) * (1.0 / HEAD_DIM)
    k_ref[0] = (k * lax.rsqrt(ms + RMS_EPS) * kg_ref[...]).astype(BF16)
    vt_ref[0] = lax.dot_general(wvt_ref[...], mb, _NT, preferred_element_type=F32).astype(BF16)


def _mem_kv(mem, g, wk, wvt, ones, kg):
    b, m, d = mem.shape
    w = wk.shape[1]
    return pl.pallas_call(
        _mem_kv_kernel,
        grid=(b,),
        in_specs=[
            pl.BlockSpec((1, m, d), lambda i: (i, 0, 0)),
            pl.BlockSpec((1, d), lambda i: (0, 0)),
            pl.BlockSpec((d, w), lambda i: (0, 0)),
            pl.BlockSpec((w, d), lambda i: (0, 0)),
            pl.BlockSpec((w, w), lambda i: (0, 0)),
            pl.BlockSpec((1, w), lambda i: (0, 0)),
        ],
        out_specs=[
            pl.BlockSpec((1, m, w), lambda i: (i, 0, 0)),
            pl.BlockSpec((1, w, m), lambda i: (i, 0, 0)),
        ],
        out_shape=[
            jax.ShapeDtypeStruct((b, m, w), BF16),
            jax.ShapeDtypeStruct((b, w, m), BF16),
        ],
        compiler_params=_cparams(("parallel",)),
        name="mem_kv",
    )(mem, g, wk, wvt, ones, kg)


def _mem_attn_kernel(qt_ref, k_ref, vt_ref, o_ref):
    for h in range(MEM_HEADS):
        pair = h // 2
        k2 = k_ref[0, :, HEAD_SLAB * pair:HEAD_SLAB * (pair + 1)]
        q = qt_ref[0, HEAD_SLAB * h:HEAD_SLAB * (h + 1), :]
        s = jnp.dot(k2, q, preferred_element_type=F32)
        m = jnp.max(s, axis=0, keepdims=True)
        p = jnp.exp(s - m)
        l = jnp.sum(p, axis=0, keepdims=True)
        v = vt_ref[0, HEAD_DIM * h:HEAD_DIM * (h + 1), :]
        o = jnp.dot(v, p.astype(BF16), preferred_element_type=F32)
        o_ref[0, HEAD_DIM * h:HEAD_DIM * (h + 1), :] = (o / l).astype(BF16)


def _mem_attn(qt, k, vt, tq):
    b, nq, s = qt.shape
    m, w = k.shape[1], k.shape[2]
    return pl.pallas_call(
        _mem_attn_kernel,
        grid=(b, s // tq),
        in_specs=[
            pl.BlockSpec((1, nq, tq), lambda i, j: (i, 0, j)),
            pl.BlockSpec((1, m, w), lambda i, j: (i, 0, 0)),
            pl.BlockSpec((1, w, m), lambda i, j: (i, 0, 0)),
        ],
        out_specs=pl.BlockSpec((1, w, tq), lambda i, j: (i, 0, j)),
        out_shape=jax.ShapeDtypeStruct((b, w, s), BF16),
        compiler_params=_cparams(("parallel", "parallel")),
        name="mem_attn",
    )(qt, k, vt)


def _s5_kernel(u_ref, bd_ref, cd_ref, ar_ref, ai_ref, d_ref, wglu_ref, bglu_ref, o_ref,
               buf_ref, st_ref, ut_ref, ot_ref, *, ts, n_slabs):
    half = S5_SLAB_STATES
    width = 2 * half
    batch = u_ref.shape[0]

    @pl.when(pl.program_id(0) == 0)
    def _():
        st_ref[...] = jnp.zeros_like(st_ref)

    for b in range(batch):
        for j in range(n_slabs):
            ut_ref[j, pl.ds(b, ts, stride=batch), :] = u_ref[b, :, HEAD_SLAB * j:HEAD_SLAB * (j + 1)]

    for j in range(n_slabs):
        buf_ref[:, width * j:width * (j + 1)] = jnp.dot(
            ut_ref[j].astype(BF16), bd_ref[j], preferred_element_type=F32)

    for j in range(n_slabs):
        re = slice(width * j, width * j + half)
        im = slice(width * j + half, width * (j + 1))
        ar = jnp.broadcast_to(ar_ref[j:j + 1, :], (SUBLANES, half))
        ai = jnp.broadcast_to(ai_ref[j:j + 1, :], (SUBLANES, half))

        def step(t, carry, re=re, im=im, ar=ar, ai=ai):
            xr, xi = carry
            rows = pl.ds(pl.multiple_of(t * SUBLANES, SUBLANES), SUBLANES)
            nxr = ar * xr - ai * xi + buf_ref[rows, re]
            nxi = ar * xi + ai * xr + buf_ref[rows, im]
            buf_ref[rows, re] = nxr
            buf_ref[rows, im] = nxi
            return nxr, nxi

        xr, xi = lax.fori_loop(0, ts, step, (st_ref[:, re], st_ref[:, im]), unroll=8)
        st_ref[:, re] = xr
        st_ref[:, im] = xi

    ys = []
    for j in range(n_slabs):
        cols = slice(HEAD_SLAB * j, HEAD_SLAB * (j + 1))
        ys.append(jnp.dot(buf_ref[:, width * j:width * (j + 1)].astype(BF16), cd_ref[j],
                          preferred_element_type=F32) + d_ref[:, cols] * ut_ref[j])
    g = jax.nn.gelu(jnp.concatenate(ys, axis=1))
    z = jnp.dot(g.astype(BF16), wglu_ref[...], preferred_element_type=F32) + bglu_ref[...]
    out = g / (1.0 + jnp.exp(-z))
    for j in range(n_slabs):
        ot_ref[j] = out[:, HEAD_SLAB * j:HEAD_SLAB * (j + 1)]
    for b in range(batch):
        for j in range(n_slabs):
            o_ref[b, :, HEAD_SLAB * j:HEAD_SLAB * (j + 1)] = (
                ot_ref[j, pl.ds(b, ts, stride=batch), :].astype(BF16))


def _s5(u, bd, cd, ar, ai, dskip, wglu, bglu, ts):
    batch, seq, c = u.shape
    n_slabs = c // HEAD_SLAB
    width = 2 * S5_SLAB_STATES
    tile = ts * batch
    return pl.pallas_call(
        functools.partial(_s5_kernel, ts=ts, n_slabs=n_slabs),
        grid=(seq // ts,),
        in_specs=[
            pl.BlockSpec((batch, ts, c), lambda i: (0, i, 0)),
            pl.BlockSpec((n_slabs, HEAD_SLAB, width), lambda i: (0, 0, 0)),
            pl.BlockSpec((n_slabs, width, HEAD_SLAB), lambda i: (0, 0, 0)),
            pl.BlockSpec((n_slabs, S5_SLAB_STATES), lambda i: (0, 0)),
            pl.BlockSpec((n_slabs, S5_SLAB_STATES), lambda i: (0, 0)),
            pl.BlockSpec((1, c), lambda i: (0, 0)),
            pl.BlockSpec((c, c), lambda i: (0, 0)),
            pl.BlockSpec((1, c), lambda i: (0, 0)),
        ],
        out_specs=pl.BlockSpec((batch, ts, c), lambda i: (0, i, 0)),
        out_shape=jax.ShapeDtypeStruct((batch, seq, c), BF16),
        scratch_shapes=[
            pltpu.VMEM((tile, n_slabs * width), F32),
            pltpu.VMEM((batch, n_slabs * width), F32),
            pltpu.VMEM((n_slabs, tile, HEAD_SLAB), F32),
            pltpu.VMEM((n_slabs, tile, HEAD_SLAB), F32),
        ],
        compiler_params=_cparams(("arbitrary",)),
        name="s5_mixer",
    )(u, bd, cd, ar, ai, dskip, wglu, bglu)


def _outproj_body(a, a_transposed, bt_ref, wa_ref, wb_ref, x_ref, g_ref):
    dn = _TN if a_transposed else (((1,), (0,)), ((), ()))
    y = lax.dot_general(a, wa_ref[...], dn, preferred_element_type=F32)
    y = y + lax.dot_general(bt_ref[0], wb_ref[...], _TN, preferred_element_type=F32)
    xn = x_ref[...] + y
    return xn, _rms(xn, g_ref[...])


def _outproj0_kernel(a_ref, bt_ref, wa_ref, wb_ref, x_ref, g_ref, xo_ref, h_ref):
    xn, h = _outproj_body(a_ref[...], False, bt_ref, wa_ref, wb_ref, x_ref, g_ref)
    xo_ref[...] = xn
    h_ref[...] = h.astype(BF16)


def _outproj0(a, bt, wa, wb, x2d, g, seq, tm):
    t, d = x2d.shape
    per_b = seq // tm
    ka, kb = a.shape[1], bt.shape[1]
    return pl.pallas_call(
        _outproj0_kernel,
        grid=(t // tm,),
        in_specs=[
            pl.BlockSpec((tm, ka), lambda i: (i, 0)),
            pl.BlockSpec((1, kb, tm), lambda i: (i // per_b, 0, i % per_b)),
            pl.BlockSpec((ka, d), lambda i: (0, 0)),
            pl.BlockSpec((kb, d), lambda i: (0, 0)),
            pl.BlockSpec((tm, d), lambda i: (i, 0)),
            pl.BlockSpec((1, d), lambda i: (0, 0)),
        ],
        out_specs=[
            pl.BlockSpec((tm, d), lambda i: (i, 0)),
            pl.BlockSpec((tm, d), lambda i: (i, 0)),
        ],
        out_shape=[
            jax.ShapeDtypeStruct((t, d), F32),
            jax.ShapeDtypeStruct((t, d), BF16),
        ],
        compiler_params=_cparams(("parallel",)),
        name="outproj0",
    )(a, bt, wa, wb, x2d, g)


def _outproj1_kernel(at_ref, bt_ref, wa_ref, wb_ref, x_ref, g_ref, wrh_ref, wrl_ref,
                     xo_ref, h_ref, rec_ref, cnt_ref, run_ref, *, tm):
    i = pl.program_id(0)

    @pl.when(i == 0)
    def _():
        run_ref[...] = jnp.zeros_like(run_ref)

    xn, h = _outproj_body(at_ref[0], True, bt_ref, wa_ref, wb_ref, x_ref, g_ref)
    xo_ref[...] = xn
    _to_row_tiles(h_ref, h)

    h_hi = h.astype(BF16)
    h_lo = (h - h_hi.astype(F32)).astype(BF16)
    logits = (lax.dot_general(wrh_ref[...], h_hi, _NT, preferred_element_type=F32)
              + lax.dot_general(wrh_ref[...], h_lo, _NT, preferred_element_type=F32)
              + lax.dot_general(wrl_ref[...], h_hi, _NT, preferred_element_type=F32))
    ex = lax.broadcasted_iota(I32, (ROUTER_ROWS, tm), 0)
    logits = jnp.where(ex < N_EXPERTS, logits, -jnp.inf)
    m1 = jnp.max(logits, axis=0, keepdims=True)
    e1 = jnp.min(jnp.where(logits == m1, ex, ROUTER_ROWS), axis=0, keepdims=True)
    rest = jnp.where(ex == e1, -jnp.inf, logits)
    m2 = jnp.max(rest, axis=0, keepdims=True)
    e2 = jnp.min(jnp.where(rest == m2, ex, ROUTER_ROWS), axis=0, keepdims=True)
    w2 = jnp.exp(m2 - m1)
    g1 = 1.0 / (1.0 + w2)
    g2 = w2 / (1.0 + w2)

    hot = jnp.where((ex == e1) | (ex == e2), 1.0, 0.0)
    r_io = lax.broadcasted_iota(I32, (tm, tm), 0)
    c_io = lax.broadcasted_iota(I32, (tm, tm), 1)
    triu = jnp.where(r_io < c_io, 1.0, 0.0).astype(BF16)
    before = (jnp.dot(hot.astype(BF16), triu, preferred_element_type=F32)
              + run_ref[:, 0:1])
    r1 = jnp.sum(jnp.where(ex == e1, before, 0.0), axis=0, keepdims=True)
    r2 = jnp.sum(jnp.where(ex == e2, before, 0.0), axis=0, keepdims=True)
    run = run_ref[...] + jnp.sum(hot, axis=1, keepdims=True)
    run_ref[...] = run
    cnt_ref[...] = run

    slot = lax.broadcasted_iota(I32, (SUBLANES, tm), 0)
    rec = jnp.where(slot == 0, e1.astype(F32), 0.0)
    rec = jnp.where(slot == 1, e2.astype(F32), rec)
    rec = jnp.where(slot == 2, r1, rec)
    rec = jnp.where(slot == 3, r2, rec)
    rec = jnp.where(slot == 4, g1, rec)
    rec = jnp.where(slot == 5, g2, rec)
    rec_ref[...] = rec


def _outproj1(at, bt, wa, wb, x2d, g, wr_hi, wr_lo, seq, tm):
    t, d = x2d.shape
    per_b = seq // tm
    ka, kb = at.shape[1], bt.shape[1]
    return pl.pallas_call(
        functools.partial(_outproj1_kernel, tm=tm),
        grid=(t // tm,),
        in_specs=[
            pl.BlockSpec((1, ka, tm), lambda i: (i // per_b, 0, i % per_b)),
            pl.BlockSpec((1, kb, tm), lambda i: (i // per_b, 0, i % per_b)),
            pl.BlockSpec((ka, d), lambda i: (0, 0)),
            pl.BlockSpec((kb, d), lambda i: (0, 0)),
            pl.BlockSpec((tm, d), lambda i: (i, 0)),
            pl.BlockSpec((1, d), lambda i: (0, 0)),
            pl.BlockSpec((ROUTER_ROWS, d), lambda i: (0, 0)),
            pl.BlockSpec((ROUTER_ROWS, d), lambda i: (0, 0)),
        ],
        out_specs=[
            pl.BlockSpec((tm, d), lambda i: (i, 0)),
            pl.BlockSpec((tm * ROW_TILE, LANES), lambda i: (i, 0)),
            pl.BlockSpec((SUBLANES, tm), lambda i: (0, i)),
            pl.BlockSpec((ROUTER_ROWS, ROUTER_LANES), lambda i: (0, 0)),
        ],
        out_shape=[
            jax.ShapeDtypeStruct((t, d), F32),
            jax.ShapeDtypeStruct((t * ROW_TILE, LANES), F32),
            jax.ShapeDtypeStruct((SUBLANES, t), F32),
            jax.ShapeDtypeStruct((ROUTER_ROWS, ROUTER_LANES), F32),
        ],
        scratch_shapes=[pltpu.VMEM((ROUTER_ROWS, ROUTER_LANES), F32)],
        compiler_params=_cparams(("arbitrary",)),
        name="outproj1_router",
    )(at, bt, wa, wb, x2d, g, wr_hi, wr_lo)


def _ffn_kernel(h_ref, res_ref, wg_ref, wu_ref, wd_ref, o_ref, acc_ref):
    f = pl.program_id(1)

    @pl.when(f == 0)
    def _():
        acc_ref[...] = jnp.zeros_like(acc_ref)

    half = h_ref.shape[0] // 2
    for rows in (slice(0, half), slice(half, 2 * half)):
        hb = h_ref[rows, :]
        gt = jnp.dot(hb, wg_ref[...], preferred_element_type=F32)
        up = jnp.dot(hb, wu_ref[...], preferred_element_type=F32)
        mid = (gt / (1.0 + jnp.exp(-gt))) * up
        acc_ref[rows, :] += jnp.dot(mid.astype(BF16), wd_ref[...], preferred_element_type=F32)

    @pl.when(f == pl.num_programs(1) - 1)
    def _():
        o_ref[...] = acc_ref[...] + res_ref[...]


def _ffn(h, res, wg, wu, wd, tm, tf):
    rows, d = h.shape
    ff = wg.shape[1]
    return pl.pallas_call(
        _ffn_kernel,
        grid=(rows // tm, ff // tf),
        in_specs=[
            pl.BlockSpec((tm, d), lambda i, f: (i, 0)),
            pl.BlockSpec((tm, d), lambda i, f: (i, 0)),
            pl.BlockSpec((d, tf), lambda i, f: (0, f)),
            pl.BlockSpec((d, tf), lambda i, f: (0, f)),
            pl.BlockSpec((tf, d), lambda i, f: (f, 0)),
        ],
        out_specs=pl.BlockSpec((tm, d), lambda i, f: (i, 0)),
        out_shape=jax.ShapeDtypeStruct((rows, d), F32),
        scratch_shapes=[pltpu.VMEM((tm, d), F32)],
        compiler_params=_cparams(("parallel", "arbitrary")),
        name="ffn_dense",
    )(h, res, wg, wu, wd)


def _from_row_tiles(ref, base, n_rows, n_feat):
    parts = [ref[pl.ds(base + k, n_rows, stride=ROW_TILE), :] for k in range(n_feat // LANES)]
    return jnp.concatenate(parts, axis=1)


def _to_row_tiles(ref, value):
    for k in range(value.shape[1] // LANES):
        ref[pl.ds(k, value.shape[0], stride=ROW_TILE), :] = value[:, LANES * k:LANES * (k + 1)]


def _moe_ffn_kernel(te_ref, nv_ref, nu_ref, xs_ref, wg_ref, wu_ref, wd_ref, o_ref,
                    acc_ref, xb_ref, *, tm):
    del te_ref
    i = pl.program_id(0)
    f = pl.program_id(1)
    nf = pl.num_programs(1)
    half = tm // 2
    d = xb_ref.shape[1]

    def partial_out(rows):
        xb = xb_ref[rows, :]
        gt = jnp.dot(xb, wg_ref[0].astype(BF16), preferred_element_type=F32)
        up = jnp.dot(xb, wu_ref[0].astype(BF16), preferred_element_type=F32)
        mid = (gt / (1.0 + jnp.exp(-gt))) * up
        acc_ref[rows, :] += jnp.dot(mid.astype(BF16), wd_ref[0].astype(BF16),
                                    preferred_element_type=F32)

    @pl.when(i < nu_ref[0])
    def _():
        @pl.when(f == 0)
        def _():
            acc_ref[...] = jnp.zeros_like(acc_ref)
            xb_ref[...] = _from_row_tiles(xs_ref, 0, tm, d).astype(BF16)

        @pl.when(nv_ref[i] > half)
        def _():
            partial_out(slice(0, half))
            partial_out(slice(half, tm))

        @pl.when(nv_ref[i] <= half)
        def _():
            partial_out(slice(0, half))

        @pl.when(f == nf - 1)
        def _():
            _to_row_tiles(o_ref, acc_ref[...])

    @pl.when((i >= nu_ref[0]) & (f == nf - 1))
    def _():
        o_ref[...] = jnp.zeros_like(o_ref)


def _moe_ffn(tile_expert, n_valid, n_used, xs_rt, wg, wu, wd, tm, tf):
    d = wg.shape[1]
    ff = wg.shape[2]
    nf = ff // tf
    n_tiles = xs_rt.shape[0] // (tm * ROW_TILE)

    def live(i, nu):
        return jnp.minimum(i, jnp.maximum(nu[0] - 1, 0))

    def f_of(i, f, nu):
        return jnp.where(i < nu[0], f, nf - 1)

    def wcol_map(i, f, te, nv, nu):
        return (te[live(i, nu)], 0, f_of(i, f, nu))

    def wrow_map(i, f, te, nv, nu):
        return (te[live(i, nu)], f_of(i, f, nu), 0)

    grid_spec = pltpu.PrefetchScalarGridSpec(
        num_scalar_prefetch=3,
        grid=(n_tiles, nf),
        in_specs=[
            pl.BlockSpec((tm * ROW_TILE, LANES), lambda i, f, te, nv, nu: (live(i, nu), 0)),
            pl.BlockSpec((1, d, tf), wcol_map),
            pl.BlockSpec((1, d, tf), wcol_map),
            pl.BlockSpec((1, tf, d), wrow_map),
        ],
        out_specs=pl.BlockSpec((tm * ROW_TILE, LANES), lambda i, f, te, nv, nu: (i, 0)),
        scratch_shapes=[pltpu.VMEM((tm, d), F32), pltpu.VMEM((tm, d), BF16)],
    )
    return pl.pallas_call(
        functools.partial(_moe_ffn_kernel, tm=tm),
        grid_spec=grid_spec,
        out_shape=jax.ShapeDtypeStruct(xs_rt.shape, F32),
        compiler_params=_cparams(("arbitrary", "arbitrary")),
        name="ffn_moe",
    )(tile_expert, n_valid, n_used, xs_rt, wg, wu, wd)


def _proj1_kernel(x_ref, g1_ref, g2_ref, wqt_ref, wk_ref, wvt_ref, kg_ref, qg_ref,
                  qt_ref, qmt_ref, k_ref, km_ref, vt_ref, *, tm, n_q, per_b):
    x = x_ref[...]
    h1 = _rms(x, g1_ref[...]).astype(BF16)
    h2 = _rms(x, g2_ref[...]).astype(BF16)
    qt = lax.dot_general(wqt_ref[...], h1, _NT, preferred_element_type=F32)

    for h in range(n_q):
        s = qt[HEAD_DIM * h:HEAD_DIM * (h + 1), :]
        ms = jnp.mean(s * s, axis=0, keepdims=True)
        qt_ref[0, HEAD_DIM * h:HEAD_DIM * (h + 1), :] = (
            s * (lax.rsqrt(ms + RMS_EPS) * (HEAD_DIM ** -0.5 * LOG2_E))).astype(BF16)
    qm = qt[HEAD_DIM * n_q:, :]
    for h, s in enumerate(_slab_norm_t(qm, MEM_HEADS, HEAD_DIM ** -0.5)):
        qmt_ref[0, HEAD_SLAB * h:HEAD_SLAB * (h + 1), :] = s.astype(BF16)

    k = jnp.dot(h2, wk_ref[...], preferred_element_type=F32)
    nblk = tm // MOBA_BLOCK
    first_blk = lax.rem(pl.program_id(0), per_b) * nblk
    lane = lax.broadcasted_iota(I32, (MOBA_BLOCK, HEAD_SLAB), 1)
    for kv in range(MOBA_KV_HEADS):
        cols = slice(HEAD_SLAB * kv, HEAD_SLAB * (kv + 1))
        ks = k[:, cols]
        ms = jnp.sum(ks * ks, axis=-1, keepdims=True) * (1.0 / HEAD_DIM)
        kn = ks * lax.rsqrt(ms + RMS_EPS) * kg_ref[:, cols]
        kq = kn * qg_ref[:, cols]
        for j in range(nblk):
            rows = slice(MOBA_BLOCK * j, MOBA_BLOCK * (j + 1))
            km_ref[j, :, cols] = jnp.mean(kn[rows], axis=0, keepdims=True) * qg_ref[:, cols]
            k_ref[rows, cols] = jnp.where(lane == HEAD_DIM + first_blk + j, 1.0,
                                          kq[rows]).astype(BF16)

    vt = lax.dot_general(wvt_ref[...], h2, _NT, preferred_element_type=F32)
    ones_row = jnp.where(lax.broadcasted_iota(I32, (V_ROWS - HEAD_DIM, tm), 0) == 0, 1.0, 0.0)
    for kv in range(MOBA_KV_HEADS):
        vt_ref[0, V_ROWS * kv:V_ROWS * kv + HEAD_DIM, :] = (
            vt[HEAD_DIM * kv:HEAD_DIM * (kv + 1), :].astype(BF16))
        vt_ref[0, V_ROWS * kv + HEAD_DIM:V_ROWS * (kv + 1), :] = ones_row.astype(BF16)


def _proj1(x2d, g1, g2, wqt, wk, wvt, kg, qg, batch, seq, tm):
    t, d = x2d.shape
    per_b = seq // tm
    nq_rows = wqt.shape[0]
    n_q = (nq_rows - MEM_HEADS * HEAD_SLAB) // HEAD_DIM
    kw = wk.shape[1]
    vw = wvt.shape[0]
    v_rows = MOBA_KV_HEADS * V_ROWS
    nblk = tm // MOBA_BLOCK
    return pl.pallas_call(
        functools.partial(_proj1_kernel, tm=tm, n_q=n_q, per_b=per_b),
        grid=(t // tm,),
        in_specs=[
            pl.BlockSpec((tm, d), lambda i: (i, 0)),
            pl.BlockSpec((1, d), lambda i: (0, 0)),
            pl.BlockSpec((1, d), lambda i: (0, 0)),
            pl.BlockSpec((nq_rows, d), lambda i: (0, 0)),
            pl.BlockSpec((d, kw), lambda i: (0, 0)),
            pl.BlockSpec((vw, d), lambda i: (0, 0)),
            pl.BlockSpec((1, kw), lambda i: (0, 0)),
            pl.BlockSpec((1, kw), lambda i: (0, 0)),
        ],
        out_specs=[
            pl.BlockSpec((1, n_q * HEAD_DIM, tm), lambda i: (i // per_b, 0, i % per_b)),
            pl.BlockSpec((1, MEM_HEADS * HEAD_SLAB, tm), lambda i: (i // per_b, 0, i % per_b)),
            pl.BlockSpec((tm, kw), lambda i: (i, 0)),
            pl.BlockSpec((nblk, 1, kw), lambda i: (i, 0, 0)),
            pl.BlockSpec((1, v_rows, tm), lambda i: (i // per_b, 0, i % per_b)),
        ],
        out_shape=[
            jax.ShapeDtypeStruct((batch, n_q * HEAD_DIM, seq), BF16),
            jax.ShapeDtypeStruct((batch, MEM_HEADS * HEAD_SLAB, seq), BF16),
            jax.ShapeDtypeStruct((t, kw), BF16),
            jax.ShapeDtypeStruct((t // MOBA_BLOCK, 1, kw), F32),
            jax.ShapeDtypeStruct((batch, v_rows, seq), BF16),
        ],
        compiler_params=_cparams(("parallel",)),
        name="proj1",
    )(x2d, g1, g2, wqt, wk, wvt, kg, qg)


def _moba_kernel(qt_ref, k_ref, vt_ref, km_ref, o_ref, s_ref):
    qb = pl.program_id(2)
    nb = km_ref.shape[1]
    bq = MOBA_BLOCK
    nq = MOBA_GROUP * bq
    q_rows = MOBA_GROUP * HEAD_DIM
    blk = lax.broadcasted_iota(I32, (nb, nq), 0)

    def gated_queries(kv):
        q64 = jnp.concatenate(
            [qt_ref[0, q_rows * kv + HEAD_DIM * g:q_rows * kv + HEAD_DIM * (g + 1), :]
             for g in range(MOBA_GROUP)], axis=1)
        km = km_ref[0, :, HEAD_SLAB * kv:HEAD_SLAB * kv + HEAD_DIM]
        km_hi = km.astype(BF16)
        km_lo = (km - km_hi.astype(F32)).astype(BF16)
        gate = (jnp.dot(km_hi, q64, preferred_element_type=F32)
                + jnp.dot(km_lo, q64, preferred_element_type=F32))
        gate = jnp.where(blk < qb, gate, NEG_INF)
        rank = jnp.zeros((nb, nq), F32)
        for m in range(nb):
            gm = gate[m:m + 1, :]
            rank = rank + jnp.where(gm > gate, 1.0,
                                    jnp.where((gm == gate) & (blk > m), 1.0, 0.0))
        dropped = (blk < qb) & (rank >= MOBA_TOP_BLOCKS)
        bias = jnp.where(dropped, NEG_INF, 0.0)
        pad_rows = HEAD_SLAB - HEAD_DIM - 2 * nb
        return jnp.concatenate(
            [q64, jnp.concatenate([bias, jnp.zeros((nb, nq), F32)], axis=0).astype(BF16),
             jnp.zeros((pad_rows, nq), BF16)], axis=0)

    q3 = [gated_queries(kv) for kv in range(MOBA_KV_PER_STEP)]

    key_pos = lax.broadcasted_iota(I32, (bq, nq), 0)
    qry_pos = lax.broadcasted_iota(I32, (bq, nq), 1) & (bq - 1)
    causal = key_pos <= qry_pos

    for n_blocks in range(1, nb + 1):
        @pl.when(qb == n_blocks - 1)
        def _(n_blocks=n_blocks):
            mx = []
            for kv in range(MOBA_KV_PER_STEP):
                m = None
                for n in range(n_blocks):
                    s = jnp.dot(k_ref[0, bq * n:bq * (n + 1), HEAD_SLAB * kv:HEAD_SLAB * (kv + 1)],
                                q3[kv], preferred_element_type=F32)
                    if n == n_blocks - 1:
                        s = jnp.where(causal, s, NEG_INF)
                    s_ref[kv, n] = s
                    cm = jnp.max(s, axis=0, keepdims=True)
                    m = cm if m is None else jnp.maximum(m, cm)
                mx.append(m)
            for kv in range(MOBA_KV_PER_STEP):
                acc = jnp.zeros((V_ROWS, nq), F32)
                for n in range(n_blocks):
                    p = jnp.exp2(s_ref[kv, n] - mx[kv]).astype(BF16)
                    acc = acc + jnp.dot(vt_ref[0, V_ROWS * kv:V_ROWS * (kv + 1),
                                               bq * n:bq * (n + 1)], p,
                                        preferred_element_type=F32)
                out = acc[0:HEAD_DIM, :] / acc[HEAD_DIM:HEAD_DIM + 1, :]
                for g in range(MOBA_GROUP):
                    r0 = q_rows * kv + HEAD_DIM * g
                    o_ref[0, r0:r0 + HEAD_DIM, :] = out[:, bq * g:bq * (g + 1)].astype(BF16)


def _moba(qt, k3, vt, km3):
    b, nq_rows, s = qt.shape
    nb = s // MOBA_BLOCK
    per = MOBA_KV_PER_STEP
    rows = per * MOBA_GROUP * HEAD_DIM
    return pl.pallas_call(
        _moba_kernel,
        grid=(b, MOBA_KV_HEADS // per, nb),
        in_specs=[
            pl.BlockSpec((1, rows, MOBA_BLOCK), lambda i, j, q: (i, j, q)),
            pl.BlockSpec((1, s, per * HEAD_SLAB), lambda i, j, q: (i, 0, j)),
            pl.BlockSpec((1, per * V_ROWS, s), lambda i, j, q: (i, j, 0)),
            pl.BlockSpec((1, nb, per * HEAD_SLAB), lambda i, j, q: (i, 0, j)),
        ],
        out_specs=pl.BlockSpec((1, rows, MOBA_BLOCK), lambda i, j, q: (i, j, q)),
        out_shape=jax.ShapeDtypeStruct((b, nq_rows, s), BF16),
        scratch_shapes=[pltpu.VMEM((per, nb, MOBA_BLOCK, MOBA_GROUP * MOBA_BLOCK), F32)],
        compiler_params=_cparams(("parallel", "parallel", "arbitrary")),
        name="moba_attn",
    )(qt, k3, vt, km3)


def _row_copy(src_ref, src_row, dst_ref, dst_row, sem):
    def tile(ref, r):
        return ref.at[pl.ds(pl.multiple_of(r * ROW_TILE, ROW_TILE), ROW_TILE)]

    return pltpu.make_async_copy(tile(src_ref, src_row), tile(dst_ref, dst_row), sem)


def _for_each_row(n_rows, fn):
    def body(j8, carry):
        for u in range(DMA_UNROLL):
            fn(j8 * DMA_UNROLL + u, u % 2)
        return carry

    lax.fori_loop(0, n_rows // DMA_UNROLL, body, 0)


def _dispatch_kernel(dest_ref, h_ref, init_ref, xs_ref, sem, *, tm):
    del init_ref

    def copy(j, slot):
        return _row_copy(h_ref, j, xs_ref, dest_ref[0, 0, slot * tm + j], sem)

    for slot in range(2):
        _for_each_row(tm, lambda j, lane, slot=slot: copy(j, slot).start(priority=lane))
    for slot in range(2):
        _for_each_row(tm, lambda j, lane, slot=slot: copy(j, slot).wait())


def _dispatch(dest3, h_rt, xs_init, tm):
    return pl.pallas_call(
        functools.partial(_dispatch_kernel, tm=tm),
        grid=(h_rt.shape[0] // (tm * ROW_TILE),),
        in_specs=[
            pl.BlockSpec((1, 1, 2 * tm), lambda i: (i, 0, 0), memory_space=pltpu.SMEM),
            pl.BlockSpec((tm * ROW_TILE, LANES), lambda i: (i, 0)),
            pl.BlockSpec(memory_space=pl.ANY),
        ],
        out_specs=pl.BlockSpec(memory_space=pl.ANY),
        out_shape=jax.ShapeDtypeStruct(xs_init.shape, xs_init.dtype),
        scratch_shapes=[pltpu.SemaphoreType.DMA(())],
        input_output_aliases={2: 0},
        compiler_params=_cparams(("arbitrary",)),
        name="moe_dispatch",
    )(dest3, h_rt, xs_init)


def _combine_kernel(dest_ref, next_ref, ys_ref, x_ref, gate_ref, o_ref, buf_ref, sem, *, tm):
    i = pl.program_id(0)
    cur = lax.rem(i, 2)

    def copy(idx_ref, buf_slot, j):
        return _row_copy(ys_ref, idx_ref[0, 0, j], buf_ref.at[buf_slot], j, sem.at[buf_slot])

    @pl.when(i == 0)
    def _():
        _for_each_row(2 * tm, lambda j, lane: copy(dest_ref, 0, j).start(priority=lane))

    @pl.when(i + 1 < pl.num_programs(0))
    def _():
        _for_each_row(2 * tm, lambda j, lane: copy(next_ref, 1 - cur, j).start(priority=lane))

    _for_each_row(2 * tm, lambda j, lane: copy(dest_ref, cur, j).wait())
    g1 = gate_ref[:, 0:1]
    g2 = gate_ref[:, 1:2]
    d = x_ref.shape[1]
    for slot in range(2):
        @pl.when(cur == slot)
        def _(slot=slot):
            rows = buf_ref.at[slot]
            y1 = _from_row_tiles(rows, 0, tm, d)
            y2 = _from_row_tiles(rows, tm * ROW_TILE, tm, d)
            o_ref[...] = x_ref[...] + (g1 * y1 + g2 * y2)


def _combine(dest3, ys, x2d, gates, tm):
    t, d = x2d.shape
    n = t // tm
    return pl.pallas_call(
        functools.partial(_combine_kernel, tm=tm),
        grid=(n,),
        in_specs=[
            pl.BlockSpec((1, 1, 2 * tm), lambda i: (i, 0, 0), memory_space=pltpu.SMEM),
            pl.BlockSpec((1, 1, 2 * tm), lambda i: (jnp.minimum(i + 1, n - 1), 0, 0),
                         memory_space=pltpu.SMEM),
            pl.BlockSpec(memory_space=pl.ANY),
            pl.BlockSpec((tm, d), lambda i: (i, 0)),
            pl.BlockSpec((tm, gates.shape[1]), lambda i: (i, 0)),
        ],
        out_specs=pl.BlockSpec((tm, d), lambda i: (i, 0)),
        out_shape=jax.ShapeDtypeStruct((t, d), F32),
        scratch_shapes=[pltpu.VMEM((2, 2 * tm * ROW_TILE, LANES), F32),
                        pltpu.SemaphoreType.DMA((2,))],
        compiler_params=_cparams(("arbitrary",)),
        name="moe_combine",
    )(dest3, dest3, ys, x2d, gates)


def _pad_heads_t(w_cols, offsets):
    d = w_cols.shape[0]
    nh = w_cols.shape[1] // HEAD_DIM
    wt = w_cols.T.reshape(nh, HEAD_DIM, d)
    hi = jnp.asarray(offsets, I32).reshape(nh, 1, 1) > 0
    z = jnp.zeros_like(wt)
    slab = jnp.concatenate([jnp.where(hi, z, wt), jnp.where(hi, wt, z)], axis=1)
    return slab.reshape(nh * HEAD_SLAB, d)


def _pad_head_cols(w):
    rows = w.shape[0]
    nh = w.shape[1] // HEAD_DIM
    w3 = w.reshape(rows, nh, HEAD_DIM)
    return jnp.concatenate([w3, jnp.zeros_like(w3)], axis=2).reshape(rows, nh * HEAD_SLAB)


def _head_block_ones(width):
    idx = jnp.arange(width) // HEAD_DIM
    return (idx[:, None] == idx[None, :]).astype(F32)


def _tile_gain(g, n_heads):
    return jnp.tile(g.astype(F32), n_heads).reshape(1, n_heads * HEAD_DIM)


def _s5_params(lam_re, lam_im, log_dt, b_re, b_im, c_re, c_im):
    f32 = F32
    g, p = lam_re.shape
    n = b_re.shape[2]
    lam = lax.complex(lam_re.astype(f32), lam_im.astype(f32))
    dt = jnp.exp(log_dt.astype(f32))[:, None]
    lam_bar = jnp.exp(lam * dt)
    b_bar = ((lam_bar - 1.0) / lam)[..., None] * lax.complex(b_re.astype(f32), b_im.astype(f32))
    ns = g // S5_GROUPS_PER_SLAB
    eye = jnp.eye(S5_GROUPS_PER_SLAB, dtype=f32)

    def in_map(part):
        blk = part.reshape(ns, S5_GROUPS_PER_SLAB, p, n).transpose(0, 1, 3, 2)
        return jnp.einsum('sgnp,gh->sgnhp', blk, eye).reshape(
            ns, S5_GROUPS_PER_SLAB * n, S5_GROUPS_PER_SLAB * p)

    def out_map(part):
        blk = part.reshape(ns, S5_GROUPS_PER_SLAB, n, p).transpose(0, 1, 3, 2)
        return jnp.einsum('sgpn,gh->sgphn', blk, eye).reshape(
            ns, S5_GROUPS_PER_SLAB * p, S5_GROUPS_PER_SLAB * n)

    bd = jnp.concatenate([in_map(jnp.real(b_bar)), in_map(jnp.imag(b_bar))], axis=2)
    cd = jnp.concatenate([out_map(c_re.astype(f32)), out_map(-c_im.astype(f32))], axis=1)
    ar = jnp.real(lam_bar).reshape(ns, S5_GROUPS_PER_SLAB * p)
    ai = jnp.imag(lam_bar).reshape(ns, S5_GROUPS_PER_SLAB * p)
    return bd.astype(BF16), cd.astype(BF16), ar, ai


def kernel(x, mem, l0_mix_norm, l0_w_in, l0_s5_lam_re, l0_s5_lam_im, l0_s5_log_dt, l0_s5_b_re, l0_s5_b_im, l0_s5_c_re, l0_s5_c_im, l0_s5_d, l0_s5_w_glu, l0_s5_b_glu, l0_mem_norm, l0_w_mem_k, l0_w_mem_v, l0_mem_q_gain, l0_mem_k_gain, l0_w_out, l0_ffn_norm, l0_ffn_w_gate, l0_ffn_w_up, l0_ffn_w_down, kv_norm, kv_w_k, kv_w_v, kv_k_gain, l1_mix_norm, l1_w_in, l1_moba_q_gain, l1_mem_norm, l1_w_mem_k, l1_w_mem_v, l1_mem_q_gain, l1_mem_k_gain, l1_w_out, l1_ffn_norm, l1_moe_router, l1_moe_w_gate, l1_moe_w_up, l1_moe_w_down):
    batch, seq, d = x.shape
    t = batch * seq
    main_w = l0_s5_w_glu.shape[0]
    mem_w = l0_w_mem_k.shape[1]
    n_q = main_w // HEAD_DIM
    tm = 512
    row = lambda v: v.astype(F32).reshape(1, -1)

    mem_off = [HEAD_DIM * (h % 2) for h in range(MEM_HEADS)]
    ones_kv = _head_block_ones(mem_w)
    x2d = x.reshape(t, d)

    main_in, qm0_t = _inproj0(
        x2d, row(l0_mix_norm), l0_w_in[:, :main_w].astype(BF16),
        _pad_heads_t(l0_w_in[:, main_w:], mem_off).astype(BF16), batch, seq, tm)
    k0, v0_t = _mem_kv(
        mem, row(l0_mem_norm), l0_w_mem_k.astype(BF16), l0_w_mem_v.T.astype(BF16), ones_kv,
        _tile_gain(l0_mem_k_gain, MEM_HEADS) * _tile_gain(l0_mem_q_gain, MEM_HEADS))
    mem0_t = _mem_attn(qm0_t, k0, v0_t, tm)

    bd, cd, ar, ai = _s5_params(l0_s5_lam_re, l0_s5_lam_im, l0_s5_log_dt, l0_s5_b_re, l0_s5_b_im,
                                l0_s5_c_re, l0_s5_c_im)
    s5_out = _s5(main_in.reshape(batch, seq, main_w), bd, cd, ar, ai, row(l0_s5_d),
                 l0_s5_w_glu.astype(BF16), row(l0_s5_b_glu), 64).reshape(t, main_w)

    x1, h1 = _outproj0(s5_out, mem0_t, l0_w_out[:main_w].astype(BF16),
                       l0_w_out[main_w:].astype(BF16), x2d, row(l0_ffn_norm), seq, tm)
    x2 = _ffn(h1, x1, l0_ffn_w_gate.astype(BF16), l0_ffn_w_up.astype(BF16),
              l0_ffn_w_down.astype(BF16), tm, l0_ffn_w_gate.shape[1] // 2)

    wq1_t = jnp.concatenate([l1_w_in[:, :main_w].T,
                             _pad_heads_t(l1_w_in[:, main_w:], mem_off)], axis=0).astype(BF16)
    q_t, qm1_t, kq, km, v_t = _proj1(
        x2, row(l1_mix_norm), row(kv_norm), wq1_t, _pad_head_cols(kv_w_k).astype(BF16),
        kv_w_v.T.astype(BF16), _pad_head_cols(_tile_gain(kv_k_gain, MOBA_KV_HEADS)),
        _pad_head_cols(_tile_gain(l1_moba_q_gain, MOBA_KV_HEADS)), batch, seq, tm)
    nb = seq // MOBA_BLOCK
    moba_t = _moba(q_t, kq.reshape(batch, seq, -1), v_t, km.reshape(batch, nb, -1))
    k1, v1_t = _mem_kv(
        mem, row(l1_mem_norm), l1_w_mem_k.astype(BF16), l1_w_mem_v.T.astype(BF16), ones_kv,
        _tile_gain(l1_mem_k_gain, MEM_HEADS) * _tile_gain(l1_mem_q_gain, MEM_HEADS))
    mem1_t = _mem_attn(qm1_t, k1, v1_t, tm)

    wr_t = jnp.zeros((ROUTER_ROWS, d), F32).at[:N_EXPERTS].set(l1_moe_router.astype(F32).T)
    wr_hi = wr_t.astype(BF16)
    wr_lo = (wr_t - wr_hi.astype(F32)).astype(BF16)
    x3, h3, rec, cnt = _outproj1(moba_t, mem1_t, l1_w_out[:main_w].astype(BF16),
                                 l1_w_out[main_w:].astype(BF16), x2, row(l1_ffn_norm),
                                 wr_hi, wr_lo, seq, tm)

    assert d == ROW_TILE * LANES
    tme = 1024
    counts = cnt[:N_EXPERTS, 0].astype(I32)
    padded = ((counts + tme - 1) // tme) * tme
    ends = jnp.cumsum(padded)
    starts = ends - padded
    max_tiles = (2 * t) // tme + N_EXPERTS
    tile_ids = jnp.arange(max_tiles, dtype=I32)
    tile_expert = jnp.minimum(jnp.searchsorted(ends // tme, tile_ids, side='right'),
                              N_EXPERTS - 1).astype(I32)
    n_valid = jnp.clip(counts[tile_expert] - (tile_ids * tme - starts[tile_expert]), 0, tme)
    n_used = (ends[-1] // tme).astype(I32).reshape(1)
    e1 = rec[0].astype(I32)
    e2 = rec[1].astype(I32)
    d1 = starts[e1] + rec[2].astype(I32)
    d2 = starts[e2] + rec[3].astype(I32)
    gates = rec[4:6].T
    def tile_dest(tile):
        return jnp.concatenate([d1.reshape(t // tile, 1, tile), d2.reshape(t // tile, 1, tile)],
                               axis=2)

    tmd, tmc = 1024, 512
    xs = _dispatch(tile_dest(tmd), h3, jnp.zeros((max_tiles * tme * ROW_TILE, LANES), F32), tmd)
    ys = _moe_ffn(tile_expert, n_valid.astype(I32), n_used, xs, l1_moe_w_gate, l1_moe_w_up,
                  l1_moe_w_down, tme, 512)
    out = _combine(tile_dest(tmc), ys, x3, gates, tmc)
    return out.reshape(batch, seq, d)
```

```python
import functools

import jax
import jax.numpy as jnp
from jax import lax
from jax.experimental import pallas as pl
from jax.experimental.pallas import tpu as pltpu

F32 = jnp.float32
BF16 = jnp.bfloat16
I32 = jnp.int32

RMS_EPS = 1e-6
NEG_INF = -1e30
HEAD_DIM = 64
HEAD_SLAB = 128
MEM_HEADS = 4
MOBA_KV_HEADS = 4
MOBA_GROUP = 3
MOBA_BLOCK = 256
MOBA_TOP_BLOCKS = 3
MOBA_KV_PER_STEP = 2
S5_GROUP_DIM = 16
S5_STATE = 64
S5_GROUPS_PER_SLAB = HEAD_SLAB // S5_GROUP_DIM
S5_SLAB_STATES = S5_GROUPS_PER_SLAB * S5_STATE
N_EXPERTS = 8
ROUTER_LANES = 128
ROUTER_ROWS = 16
SUBLANES = 8
LANES = 128
ROW_TILE = SUBLANES
DMA_UNROLL = 8
LOG2_E = 1.4426950408889634
V_ROWS = HEAD_DIM + 16

VMEM_LIMIT_BYTES = 56 * 1024 * 1024

_NT = (((1,), (1,)), ((), ()))
_TN = (((0,), (0,)), ((), ()))


def _cparams(sem):
    return pltpu.CompilerParams(dimension_semantics=sem, vmem_limit_bytes=VMEM_LIMIT_BYTES)


def _rms(x, g):
    return x * lax.rsqrt(jnp.mean(x * x, axis=-1, keepdims=True) + RMS_EPS) * g


def _store_vt_with_ones(vt_ref, vt, n_heads):
    n = vt.shape[1]
    ones_row = jnp.where(lax.broadcasted_iota(I32, (V_ROWS - HEAD_DIM, n), 0) == 0, 1.0, 0.0)
    for h in range(n_heads):
        vt_ref[0, V_ROWS * h:V_ROWS * h + HEAD_DIM, :] = (
            vt[HEAD_DIM * h:HEAD_DIM * (h + 1), :].astype(BF16))
        vt_ref[0, V_ROWS * h + HEAD_DIM:V_ROWS * (h + 1), :] = ones_row.astype(BF16)


def _slab_norm_t(qt, n_heads, scale):
    outs = []
    for h in range(n_heads):
        s = qt[HEAD_SLAB * h:HEAD_SLAB * (h + 1), :]
        ms = jnp.sum(s * s, axis=0, keepdims=True) * (1.0 / HEAD_DIM)
        outs.append(s * (lax.rsqrt(ms + RMS_EPS) * scale))
    return outs


def _inproj0_kernel(x_ref, g_ref, wm_ref, wqt_ref, main_ref, qt_ref):
    hb = _rms(x_ref[...], g_ref[...]).astype(BF16)
    main_ref[...] = jnp.dot(hb, wm_ref[...], preferred_element_type=F32)
    qt = lax.dot_general(wqt_ref[...], hb, _NT, preferred_element_type=F32)
    for h, s in enumerate(_slab_norm_t(qt, MEM_HEADS, HEAD_DIM ** -0.5 * LOG2_E)):
        qt_ref[0, HEAD_SLAB * h:HEAD_SLAB * (h + 1), :] = s.astype(BF16)


def _inproj0(x2d, g, wm, wqt, batch, seq, tm):
    t, d = x2d.shape
    per_b = seq // tm
    nm = wm.shape[1]
    nq = wqt.shape[0]
    return pl.pallas_call(
        _inproj0_kernel,
        grid=(t // tm,),
        in_specs=[
            pl.BlockSpec((tm, d), lambda i: (i, 0)),
            pl.BlockSpec((1, d), lambda i: (0, 0)),
            pl.BlockSpec((d, nm), lambda i: (0, 0)),
            pl.BlockSpec((nq, d), lambda i: (0, 0)),
        ],
        out_specs=[
            pl.BlockSpec((tm, nm), lambda i: (i, 0)),
            pl.BlockSpec((1, nq, tm), lambda i: (i // per_b, 0, i % per_b)),
        ],
        out_shape=[
            jax.ShapeDtypeStruct((t, nm), F32),
            jax.ShapeDtypeStruct((batch, nq, seq), BF16),
        ],
        compiler_params=_cparams(("parallel",)),
        name="inproj0",
    )(x2d, g, wm, wqt)


def _mem_kv_kernel(mem_ref, g_ref, wk_ref, wvt_ref, ones_ref, kg_ref, k_ref, vt_ref):
    mb = _rms(mem_ref[0], g_ref[...]).astype(BF16)
    k = jnp.dot(mb, wk_ref[...], preferred_element_type=F32)
    ms = jnp.dot(k * k, ones_ref[...], preferred_element_type=F32,
                 precision=lax.Precision.HIGHEST) * (1.0 / HEAD_DIM)
    k_ref[0] = (k * lax.rsqrt(ms + RMS_EPS) * kg_ref[...]).astype(BF16)
    _store_vt_with_ones(vt_ref, lax.dot_general(wvt_ref[...], mb, _NT,
                                                preferred_element_type=F32), MEM_HEADS)


def _mem_kv(mem, g, wk, wvt, ones, kg):
    b, m, d = mem.shape
    w = wk.shape[1]
    v_rows = MEM_HEADS * V_ROWS
    return pl.pallas_call(
        _mem_kv_kernel,
        grid=(b,),
        in_specs=[
            pl.BlockSpec((1, m, d), lambda i: (i, 0, 0)),
            pl.BlockSpec((1, d), lambda i: (0, 0)),
            pl.BlockSpec((d, w), lambda i: (0, 0)),
            pl.BlockSpec((w, d), lambda i: (0, 0)),
            pl.BlockSpec((w, w), lambda i: (0, 0)),
            pl.BlockSpec((1, w), lambda i: (0, 0)),
        ],
        out_specs=[
            pl.BlockSpec((1, m, w), lambda i: (i, 0, 0)),
            pl.BlockSpec((1, v_rows, m), lambda i: (i, 0, 0)),
        ],
        out_shape=[
            jax.ShapeDtypeStruct((b, m, w), BF16),
            jax.ShapeDtypeStruct((b, v_rows, m), BF16),
        ],
        compiler_params=_cparams(("parallel",)),
        name="mem_kv",
    )(mem, g, wk, wvt, ones, kg)


def _mem_scores_t(qt_ref, k_ref):
    scores = []
    for h in range(MEM_HEADS):
        pair = h // 2
        k2 = k_ref[0, :, HEAD_SLAB * pair:HEAD_SLAB * (pair + 1)]
        q = qt_ref[0, HEAD_SLAB * h:HEAD_SLAB * (h + 1), :]
        scores.append(jnp.dot(k2, q, preferred_element_type=F32))
    return scores


def _mem_attend_t(scores, vt_ref):
    outs = []
    for h, s in enumerate(scores):
        p = jnp.exp2(s - jnp.max(s, axis=0, keepdims=True)).astype(BF16)
        v = vt_ref[0, V_ROWS * h:V_ROWS * (h + 1), :]
        acc = jnp.dot(v, p, preferred_element_type=F32)
        outs.append((acc[0:HEAD_DIM, :] / acc[HEAD_DIM:HEAD_DIM + 1, :]).astype(BF16))
    return jnp.concatenate(outs, axis=0)


def _mem_attn_specs(qt, k, vt, per_b, tm):
    return [
        pl.BlockSpec((1, qt.shape[1], tm), lambda i: (i // per_b, 0, i % per_b)),
        pl.BlockSpec((1,) + k.shape[1:], lambda i: (i // per_b, 0, 0)),
        pl.BlockSpec((1,) + vt.shape[1:], lambda i: (i // per_b, 0, 0)),
    ]


def _s5_kernel(u_ref, bd_ref, cd_ref, ar_ref, ai_ref, d_ref, wglu_ref, bglu_ref, o_ref,
               buf_ref, st_ref, ut_ref, ot_ref, *, ts, n_slabs):
    half = S5_SLAB_STATES
    width = 2 * half
    batch = u_ref.shape[0]

    @pl.when(pl.program_id(0) == 0)
    def _():
        st_ref[...] = jnp.zeros_like(st_ref)

    for b in range(batch):
        for j in range(n_slabs):
            ut_ref[j, pl.ds(b, ts, stride=batch), :] = u_ref[b, :, HEAD_SLAB * j:HEAD_SLAB * (j + 1)]

    for j in range(n_slabs):
        buf_ref[:, width * j:width * (j + 1)] = jnp.dot(
            ut_ref[j].astype(BF16), bd_ref[j], preferred_element_type=F32)

    for j in range(n_slabs):
        re = slice(width * j, width * j + half)
        im = slice(width * j + half, width * (j + 1))
        ar = jnp.broadcast_to(ar_ref[j:j + 1, :], (SUBLANES, half))
        ai = jnp.broadcast_to(ai_ref[j:j + 1, :], (SUBLANES, half))

        def step(t, carry, re=re, im=im, ar=ar, ai=ai):
            xr, xi = carry
            rows = pl.ds(pl.multiple_of(t * SUBLANES, SUBLANES), SUBLANES)
            nxr = ar * xr - ai * xi + buf_ref[rows, re]
            nxi = ar * xi + ai * xr + buf_ref[rows, im]
            buf_ref[rows, re] = nxr
            buf_ref[rows, im] = nxi
            return nxr, nxi

        xr, xi = lax.fori_loop(0, ts, step, (st_ref[:, re], st_ref[:, im]), unroll=True)
        st_ref[:, re] = xr
        st_ref[:, im] = xi

    ys = []
    for j in range(n_slabs):
        cols = slice(HEAD_SLAB * j, HEAD_SLAB * (j + 1))
        ys.append(jnp.dot(buf_ref[:, width * j:width * (j + 1)].astype(BF16), cd_ref[j],
                          preferred_element_type=F32) + d_ref[:, cols] * ut_ref[j])
    g = jax.nn.gelu(jnp.concatenate(ys, axis=1))
    z = jnp.dot(g.astype(BF16), wglu_ref[...], preferred_element_type=F32) + bglu_ref[...]
    out = g / (1.0 + jnp.exp(-z))
    for j in range(n_slabs):
        ot_ref[j] = out[:, HEAD_SLAB * j:HEAD_SLAB * (j + 1)]
    for b in range(batch):
        for j in range(n_slabs):
            o_ref[b, :, HEAD_SLAB * j:HEAD_SLAB * (j + 1)] = (
                ot_ref[j, pl.ds(b, ts, stride=batch), :].astype(BF16))


def _s5(u, bd, cd, ar, ai, dskip, wglu, bglu, ts):
    batch, seq, c = u.shape
    n_slabs = c // HEAD_SLAB
    width = 2 * S5_SLAB_STATES
    tile = ts * batch
    return pl.pallas_call(
        functools.partial(_s5_kernel, ts=ts, n_slabs=n_slabs),
        grid=(seq // ts,),
        in_specs=[
            pl.BlockSpec((batch, ts, c), lambda i: (0, i, 0)),
            pl.BlockSpec((n_slabs, HEAD_SLAB, width), lambda i: (0, 0, 0)),
            pl.BlockSpec((n_slabs, width, HEAD_SLAB), lambda i: (0, 0, 0)),
            pl.BlockSpec((n_slabs, S5_SLAB_STATES), lambda i: (0, 0)),
            pl.BlockSpec((n_slabs, S5_SLAB_STATES), lambda i: (0, 0)),
            pl.BlockSpec((1, c), lambda i: (0, 0)),
            pl.BlockSpec((c, c), lambda i: (0, 0)),
            pl.BlockSpec((1, c), lambda i: (0, 0)),
        ],
        out_specs=pl.BlockSpec((batch, ts, c), lambda i: (0, i, 0)),
        out_shape=jax.ShapeDtypeStruct((batch, seq, c), BF16),
        scratch_shapes=[
            pltpu.VMEM((tile, n_slabs * width), F32),
            pltpu.VMEM((batch, n_slabs * width), F32),
            pltpu.VMEM((n_slabs, tile, HEAD_SLAB), F32),
            pltpu.VMEM((n_slabs, tile, HEAD_SLAB), F32),
        ],
        compiler_params=_cparams(("arbitrary",)),
        name="s5_mixer",
    )(u, bd, cd, ar, ai, dskip, wglu, bglu)


def _outproj_body(a, a_transposed, mem_refs, wa_ref, wb_ref, x_ref, g_ref):
    qt_ref, mk_ref, mvt_ref = mem_refs
    dn = _TN if a_transposed else (((1,), (0,)), ((), ()))
    scores = _mem_scores_t(qt_ref, mk_ref)
    y = lax.dot_general(a, wa_ref[...], dn, preferred_element_type=F32)
    y = y + lax.dot_general(_mem_attend_t(scores, mvt_ref), wb_ref[...], _TN,
                            preferred_element_type=F32)
    xn = x_ref[...] + y
    return xn, _rms(xn, g_ref[...])


def _outproj0_kernel(a_ref, qt_ref, mk_ref, mvt_ref, wa_ref, wb_ref, x_ref, g_ref, xo_ref, h_ref):
    xn, h = _outproj_body(a_ref[...], False, (qt_ref, mk_ref, mvt_ref), wa_ref, wb_ref, x_ref,
                          g_ref)
    xo_ref[...] = xn
    h_ref[...] = h.astype(BF16)


def _outproj0(a, mem_qkv, wa, wb, x2d, g, seq, tm):
    t, d = x2d.shape
    per_b = seq // tm
    ka, kb = a.shape[1], wb.shape[0]
    return pl.pallas_call(
        _outproj0_kernel,
        grid=(t // tm,),
        in_specs=[
            pl.BlockSpec((tm, ka), lambda i: (i, 0)),
            *_mem_attn_specs(*mem_qkv, per_b, tm),
            pl.BlockSpec((ka, d), lambda i: (0, 0)),
            pl.BlockSpec((kb, d), lambda i: (0, 0)),
            pl.BlockSpec((tm, d), lambda i: (i, 0)),
            pl.BlockSpec((1, d), lambda i: (0, 0)),
        ],
        out_specs=[
            pl.BlockSpec((tm, d), lambda i: (i, 0)),
            pl.BlockSpec((tm, d), lambda i: (i, 0)),
        ],
        out_shape=[
            jax.ShapeDtypeStruct((t, d), F32),
            jax.ShapeDtypeStruct((t, d), BF16),
        ],
        compiler_params=_cparams(("parallel",)),
        name="outproj0",
    )(a, *mem_qkv, wa, wb, x2d, g)


def _outproj1_kernel(at_ref, qt_ref, mk_ref, mvt_ref, wa_ref, wb_ref, x_ref, g_ref, wrh_ref,
                     wrl_ref, xo_ref, h_ref, rec_ref, cnt_ref, run_ref, *, tm):
    i = pl.program_id(0)

    @pl.when(i == 0)
    def _():
        run_ref[...] = jnp.zeros_like(run_ref)

    xn, h = _outproj_body(at_ref[0], True, (qt_ref, mk_ref, mvt_ref), wa_ref, wb_ref, x_ref,
                          g_ref)
    xo_ref[...] = xn
    _to_row_tiles(h_ref, h)

    h_hi = h.astype(BF16)
    h_lo = (h - h_hi.astype(F32)).astype(BF16)
    logits = (lax.dot_general(wrh_ref[...], h_hi, _NT, preferred_element_type=F32)
              + lax.dot_general(wrh_ref[...], h_lo, _NT, preferred_element_type=F32)
              + lax.dot_general(wrl_ref[...], h_hi, _NT, preferred_element_type=F32))
    ex = lax.broadcasted_iota(I32, (ROUTER_ROWS, tm), 0)
    logits = jnp.where(ex < N_EXPERTS, logits, -jnp.inf)
    m1 = jnp.max(logits, axis=0, keepdims=True)
    e1 = jnp.min(jnp.where(logits == m1, ex, ROUTER_ROWS), axis=0, keepdims=True)
    rest = jnp.where(ex == e1, -jnp.inf, logits)
    m2 = jnp.max(rest, axis=0, keepdims=True)
    e2 = jnp.min(jnp.where(rest == m2, ex, ROUTER_ROWS), axis=0, keepdims=True)
    w2 = jnp.exp(m2 - m1)
    g1 = 1.0 / (1.0 + w2)
    g2 = w2 / (1.0 + w2)

    hot = jnp.where((ex == e1) | (ex == e2), 1.0, 0.0)
    r_io = lax.broadcasted_iota(I32, (tm, tm), 0)
    c_io = lax.broadcasted_iota(I32, (tm, tm), 1)
    triu = jnp.where(r_io < c_io, 1.0, 0.0).astype(BF16)
    before = (jnp.dot(hot.astype(BF16), triu, preferred_element_type=F32)
              + run_ref[:, 0:1])
    r1 = jnp.sum(jnp.where(ex == e1, before, 0.0), axis=0, keepdims=True)
    r2 = jnp.sum(jnp.where(ex == e2, before, 0.0), axis=0, keepdims=True)
    run = run_ref[...] + jnp.sum(hot, axis=1, keepdims=True)
    run_ref[...] = run
    cnt_ref[...] = run

    slot = lax.broadcasted_iota(I32, (SUBLANES, tm), 0)
    rec = jnp.where(slot == 0, e1.astype(F32), 0.0)
    rec = jnp.where(slot == 1, e2.astype(F32), rec)
    rec = jnp.where(slot == 2, r1, rec)
    rec = jnp.where(slot == 3, r2, rec)
    rec = jnp.where(slot == 4, g1, rec)
    rec = jnp.where(slot == 5, g2, rec)
    rec_ref[...] = rec


def _outproj1(at, mem_qkv, wa, wb, x2d, g, wr_hi, wr_lo, seq, tm):
    t, d = x2d.shape
    per_b = seq // tm
    ka, kb = at.shape[1], wb.shape[0]
    return pl.pallas_call(
        functools.partial(_outproj1_kernel, tm=tm),
        grid=(t // tm,),
        in_specs=[
            pl.BlockSpec((1, ka, tm), lambda i: (i // per_b, 0, i % per_b)),
            *_mem_attn_specs(*mem_qkv, per_b, tm),
            pl.BlockSpec((ka, d), lambda i: (0, 0)),
            pl.BlockSpec((kb, d), lambda i: (0, 0)),
            pl.BlockSpec((tm, d), lambda i: (i, 0)),
            pl.BlockSpec((1, d), lambda i: (0, 0)),
            pl.BlockSpec((ROUTER_ROWS, d), lambda i: (0, 0)),
            pl.BlockSpec((ROUTER_ROWS, d), lambda i: (0, 0)),
        ],
        out_specs=[
            pl.BlockSpec((tm, d), lambda i: (i, 0)),
            pl.BlockSpec((tm * ROW_TILE, LANES), lambda i: (i, 0)),
            pl.BlockSpec((SUBLANES, tm), lambda i: (0, i)),
            pl.BlockSpec((ROUTER_ROWS, ROUTER_LANES), lambda i: (0, 0)),
        ],
        out_shape=[
            jax.ShapeDtypeStruct((t, d), F32),
            jax.ShapeDtypeStruct((t * ROW_TILE, LANES), F32),
            jax.ShapeDtypeStruct((SUBLANES, t), F32),
            jax.ShapeDtypeStruct((ROUTER_ROWS, ROUTER_LANES), F32),
        ],
        scratch_shapes=[pltpu.VMEM((ROUTER_ROWS, ROUTER_LANES), F32)],
        compiler_params=_cparams(("arbitrary",)),
        name="outproj1_router",
    )(at, *mem_qkv, wa, wb, x2d, g, wr_hi, wr_lo)


def _ffn_kernel(h_ref, res_ref, wg_ref, wu_ref, wd_ref, o_ref):
    half = h_ref.shape[0] // 2
    for rows in (slice(0, half), slice(half, 2 * half)):
        hb = h_ref[rows, :]
        gt = jnp.dot(hb, wg_ref[...], preferred_element_type=F32)
        up = jnp.dot(hb, wu_ref[...], preferred_element_type=F32)
        mid = (gt / (1.0 + jnp.exp(-gt))) * up
        o_ref[rows, :] = res_ref[rows, :] + jnp.dot(mid.astype(BF16), wd_ref[...],
                                                    preferred_element_type=F32)


def _ffn(h, res, wg, wu, wd, tm):
    rows, d = h.shape
    ff = wg.shape[1]
    resident = dict(pipeline_mode=pl.Buffered(1))
    return pl.pallas_call(
        _ffn_kernel,
        grid=(rows // tm,),
        in_specs=[
            pl.BlockSpec((tm, d), lambda i: (i, 0)),
            pl.BlockSpec((tm, d), lambda i: (i, 0)),
            pl.BlockSpec((d, ff), lambda i: (0, 0), **resident),
            pl.BlockSpec((d, ff), lambda i: (0, 0), **resident),
            pl.BlockSpec((ff, d), lambda i: (0, 0), **resident),
        ],
        out_specs=pl.BlockSpec((tm, d), lambda i: (i, 0)),
        out_shape=jax.ShapeDtypeStruct((rows, d), F32),
        compiler_params=_cparams(("parallel",)),
        name="ffn_dense",
    )(h, res, wg, wu, wd)


def _from_row_tiles(ref, base, n_rows, n_feat):
    parts = [ref[pl.ds(base + k, n_rows, stride=ROW_TILE), :] for k in range(n_feat // LANES)]
    return jnp.concatenate(parts, axis=1)


def _to_row_tiles(ref, value):
    for k in range(value.shape[1] // LANES):
        ref[pl.ds(k, value.shape[0], stride=ROW_TILE), :] = value[:, LANES * k:LANES * (k + 1)]


def _moe_ffn_kernel(te_ref, nv_ref, nu_ref, xs_ref, wg_ref, wu_ref, wd_ref, o_ref,
                    acc_ref, xb_ref, *, tm):
    del te_ref
    i = pl.program_id(0)
    f = pl.program_id(1)
    nf = pl.num_programs(1)
    half = tm // 2
    d = xb_ref.shape[1]

    def partial_out(rows):
        xb = xb_ref[rows, :]
        gt = jnp.dot(xb, wg_ref[0].astype(BF16), preferred_element_type=F32)
        up = jnp.dot(xb, wu_ref[0].astype(BF16), preferred_element_type=F32)
        mid = (gt / (1.0 + jnp.exp(-gt))) * up
        acc_ref[rows, :] += jnp.dot(mid.astype(BF16), wd_ref[0].astype(BF16),
                                    preferred_element_type=F32)

    @pl.when(i < nu_ref[0])
    def _():
        @pl.when(f == 0)
        def _():
            acc_ref[...] = jnp.zeros_like(acc_ref)
            xb_ref[...] = _from_row_tiles(xs_ref, 0, tm, d).astype(BF16)

        @pl.when(nv_ref[i] > half)
        def _():
            partial_out(slice(0, half))
            partial_out(slice(half, tm))

        @pl.when(nv_ref[i] <= half)
        def _():
            partial_out(slice(0, half))

        @pl.when(f == nf - 1)
        def _():
            _to_row_tiles(o_ref, acc_ref[...])

    @pl.when((i >= nu_ref[0]) & (f == nf - 1))
    def _():
        o_ref[...] = jnp.zeros_like(o_ref)


def _moe_ffn(tile_expert, n_valid, n_used, xs_rt, wg, wu, wd, tm, tf):
    d = wg.shape[1]
    ff = wg.shape[2]
    nf = ff // tf
    n_tiles = xs_rt.shape[0] // (tm * ROW_TILE)

    def live(i, nu):
        return jnp.minimum(i, jnp.maximum(nu[0] - 1, 0))

    def f_of(i, f, nu):
        return jnp.where(i < nu[0], f, nf - 1)

    def wcol_map(i, f, te, nv, nu):
        return (te[live(i, nu)], 0, f_of(i, f, nu))

    def wrow_map(i, f, te, nv, nu):
        return (te[live(i, nu)], f_of(i, f, nu), 0)

    grid_spec = pltpu.PrefetchScalarGridSpec(
        num_scalar_prefetch=3,
        grid=(n_tiles, nf),
        in_specs=[
            pl.BlockSpec((tm * ROW_TILE, LANES), lambda i, f, te, nv, nu: (live(i, nu), 0)),
            pl.BlockSpec((1, d, tf), wcol_map),
            pl.BlockSpec((1, d, tf), wcol_map),
            pl.BlockSpec((1, tf, d), wrow_map),
        ],
        out_specs=pl.BlockSpec((tm * ROW_TILE, LANES), lambda i, f, te, nv, nu: (i, 0)),
        scratch_shapes=[pltpu.VMEM((tm, d), F32), pltpu.VMEM((tm, d), BF16)],
    )
    return pl.pallas_call(
        functools.partial(_moe_ffn_kernel, tm=tm),
        grid_spec=grid_spec,
        out_shape=jax.ShapeDtypeStruct(xs_rt.shape, F32),
        compiler_params=_cparams(("arbitrary", "arbitrary")),
        name="ffn_moe",
    )(tile_expert, n_valid, n_used, xs_rt, wg, wu, wd)


def _proj1_kernel(x_ref, g1_ref, g2_ref, wqt_ref, wk_ref, wvt_ref, kg_ref, qg_ref,
                  qt_ref, qmt_ref, k_ref, km_ref, vt_ref, *, tm, n_q, per_b):
    x = x_ref[...]
    h1 = _rms(x, g1_ref[...]).astype(BF16)
    h2 = _rms(x, g2_ref[...]).astype(BF16)
    qt = lax.dot_general(wqt_ref[...], h1, _NT, preferred_element_type=F32)

    for h in range(n_q):
        s = qt[HEAD_DIM * h:HEAD_DIM * (h + 1), :]
        ms = jnp.mean(s * s, axis=0, keepdims=True)
        qt_ref[0, HEAD_DIM * h:HEAD_DIM * (h + 1), :] = (
            s * (lax.rsqrt(ms + RMS_EPS) * (HEAD_DIM ** -0.5 * LOG2_E))).astype(BF16)
    qm = qt[HEAD_DIM * n_q:, :]
    for h, s in enumerate(_slab_norm_t(qm, MEM_HEADS, HEAD_DIM ** -0.5 * LOG2_E)):
        qmt_ref[0, HEAD_SLAB * h:HEAD_SLAB * (h + 1), :] = s.astype(BF16)

    k = jnp.dot(h2, wk_ref[...], preferred_element_type=F32)
    nblk = tm // MOBA_BLOCK
    first_blk = lax.rem(pl.program_id(0), per_b) * nblk
    lane = lax.broadcasted_iota(I32, (MOBA_BLOCK, HEAD_SLAB), 1)
    for kv in range(MOBA_KV_HEADS):
        cols = slice(HEAD_SLAB * kv, HEAD_SLAB * (kv + 1))
        ks = k[:, cols]
        ms = jnp.sum(ks * ks, axis=-1, keepdims=True) * (1.0 / HEAD_DIM)
        kn = ks * lax.rsqrt(ms + RMS_EPS) * kg_ref[:, cols]
        kq = kn * qg_ref[:, cols]
        for j in range(nblk):
            rows = slice(MOBA_BLOCK * j, MOBA_BLOCK * (j + 1))
            km_ref[j, :, cols] = jnp.mean(kn[rows], axis=0, keepdims=True) * qg_ref[:, cols]
            k_ref[rows, cols] = jnp.where(lane == HEAD_DIM + first_blk + j, 1.0,
                                          kq[rows]).astype(BF16)

    _store_vt_with_ones(vt_ref, lax.dot_general(wvt_ref[...], h2, _NT,
                                                preferred_element_type=F32), MOBA_KV_HEADS)


def _proj1(x2d, g1, g2, wqt, wk, wvt, kg, qg, batch, seq, tm):
    t, d = x2d.shape
    per_b = seq // tm
    nq_rows = wqt.shape[0]
    n_q = (nq_rows - MEM_HEADS * HEAD_SLAB) // HEAD_DIM
    kw = wk.shape[1]
    vw = wvt.shape[0]
    v_rows = MOBA_KV_HEADS * V_ROWS
    nblk = tm // MOBA_BLOCK
    return pl.pallas_call(
        functools.partial(_proj1_kernel, tm=tm, n_q=n_q, per_b=per_b),
        grid=(t // tm,),
        in_specs=[
            pl.BlockSpec((tm, d), lambda i: (i, 0)),
            pl.BlockSpec((1, d), lambda i: (0, 0)),
            pl.BlockSpec((1, d), lambda i: (0, 0)),
            pl.BlockSpec((nq_rows, d), lambda i: (0, 0)),
            pl.BlockSpec((d, kw), lambda i: (0, 0)),
            pl.BlockSpec((vw, d), lambda i: (0, 0)),
            pl.BlockSpec((1, kw), lambda i: (0, 0)),
            pl.BlockSpec((1, kw), lambda i: (0, 0)),
        ],
        out_specs=[
            pl.BlockSpec((1, n_q * HEAD_DIM, tm), lambda i: (i // per_b, 0, i % per_b)),
            pl.BlockSpec((1, MEM_HEADS * HEAD_SLAB, tm), lambda i: (i // per_b, 0, i % per_b)),
            pl.BlockSpec((tm, kw), lambda i: (i, 0)),
            pl.BlockSpec((nblk, 1, kw), lambda i: (i, 0, 0)),
            pl.BlockSpec((1, v_rows, tm), lambda i: (i // per_b, 0, i % per_b)),
        ],
        out_shape=[
            jax.ShapeDtypeStruct((batch, n_q * HEAD_DIM, seq), BF16),
            jax.ShapeDtypeStruct((batch, MEM_HEADS * HEAD_SLAB, seq), BF16),
            jax.ShapeDtypeStruct((t, kw), BF16),
            jax.ShapeDtypeStruct((t // MOBA_BLOCK, 1, kw), F32),
            jax.ShapeDtypeStruct((batch, v_rows, seq), BF16),
        ],
        compiler_params=_cparams(("parallel",)),
        name="proj1",
    )(x2d, g1, g2, wqt, wk, wvt, kg, qg)


def _moba_kernel(qt_ref, k_ref, vt_ref, km_ref, o_ref, s_ref):
    qb = pl.program_id(2)
    nb = km_ref.shape[1]
    bq = MOBA_BLOCK
    nq = MOBA_GROUP * bq
    q_rows = MOBA_GROUP * HEAD_DIM
    blk = lax.broadcasted_iota(I32, (nb, nq), 0)

    def gated_queries(kv):
        q64 = jnp.concatenate(
            [qt_ref[0, q_rows * kv + HEAD_DIM * g:q_rows * kv + HEAD_DIM * (g + 1), :]
             for g in range(MOBA_GROUP)], axis=1)
        km = km_ref[0, :, HEAD_SLAB * kv:HEAD_SLAB * kv + HEAD_DIM]
        km_hi = km.astype(BF16)
        km_lo = (km - km_hi.astype(F32)).astype(BF16)
        gate = (jnp.dot(km_hi, q64, preferred_element_type=F32)
                + jnp.dot(km_lo, q64, preferred_element_type=F32))
        gate = jnp.where(blk < qb, gate, NEG_INF)
        rank = jnp.zeros((nb, nq), F32)
        for m in range(nb):
            gm = gate[m:m + 1, :]
            rank = rank + jnp.where(gm > gate, 1.0,
                                    jnp.where((gm == gate) & (blk > m), 1.0, 0.0))
        dropped = (blk < qb) & (rank >= MOBA_TOP_BLOCKS)
        bias = jnp.where(dropped, NEG_INF, 0.0)
        pad_rows = HEAD_SLAB - HEAD_DIM - 2 * nb
        return jnp.concatenate(
            [q64, jnp.concatenate([bias, jnp.zeros((nb, nq), F32)], axis=0).astype(BF16),
             jnp.zeros((pad_rows, nq), BF16)], axis=0)

    q3 = [gated_queries(kv) for kv in range(MOBA_KV_PER_STEP)]

    key_pos = lax.broadcasted_iota(I32, (bq, nq), 0)
    qry_pos = lax.broadcasted_iota(I32, (bq, nq), 1) & (bq - 1)
    causal = key_pos <= qry_pos

    for n_blocks in range(1, nb + 1):
        @pl.when(qb == n_blocks - 1)
        def _(n_blocks=n_blocks):
            mx = []
            for kv in range(MOBA_KV_PER_STEP):
                m = None
                for n in range(n_blocks):
                    s = jnp.dot(k_ref[0, bq * n:bq * (n + 1), HEAD_SLAB * kv:HEAD_SLAB * (kv + 1)],
                                q3[kv], preferred_element_type=F32)
                    if n == n_blocks - 1:
                        s = jnp.where(causal, s, NEG_INF)
                    s_ref[kv, n] = s
                    cm = jnp.max(s, axis=0, keepdims=True)
                    m = cm if m is None else jnp.maximum(m, cm)
                mx.append(m)
            for kv in range(MOBA_KV_PER_STEP):
                acc = jnp.zeros((V_ROWS, nq), F32)
                for n in range(n_blocks):
                    p = jnp.exp2(s_ref[kv, n] - mx[kv]).astype(BF16)
                    acc = acc + jnp.dot(vt_ref[0, V_ROWS * kv:V_ROWS * (kv + 1),
                                               bq * n:bq * (n + 1)], p,
                                        preferred_element_type=F32)
                out = acc[0:HEAD_DIM, :] / acc[HEAD_DIM:HEAD_DIM + 1, :]
                for g in range(MOBA_GROUP):
                    r0 = q_rows * kv + HEAD_DIM * g
                    o_ref[0, r0:r0 + HEAD_DIM, :] = out[:, bq * g:bq * (g + 1)].astype(BF16)


def _moba(qt, k3, vt, km3):
    b, nq_rows, s = qt.shape
    nb = s // MOBA_BLOCK
    per = MOBA_KV_PER_STEP
    rows = per * MOBA_GROUP * HEAD_DIM
    return pl.pallas_call(
        _moba_kernel,
        grid=(b, MOBA_KV_HEADS // per, nb),
        in_specs=[
            pl.BlockSpec((1, rows, MOBA_BLOCK), lambda i, j, q: (i, j, q)),
            pl.BlockSpec((1, s, per * HEAD_SLAB), lambda i, j, q: (i, 0, j)),
            pl.BlockSpec((1, per * V_ROWS, s), lambda i, j, q: (i, j, 0)),
            pl.BlockSpec((1, nb, per * HEAD_SLAB), lambda i, j, q: (i, 0, j)),
        ],
        out_specs=pl.BlockSpec((1, rows, MOBA_BLOCK), lambda i, j, q: (i, j, q)),
        out_shape=jax.ShapeDtypeStruct((b, nq_rows, s), BF16),
        scratch_shapes=[pltpu.VMEM((per, nb, MOBA_BLOCK, MOBA_GROUP * MOBA_BLOCK), F32)],
        compiler_params=_cparams(("parallel", "parallel", "arbitrary")),
        name="moba_attn",
    )(qt, k3, vt, km3)


def _row_copy(src_ref, src_row, dst_ref, dst_row, sem):
    def tile(ref, r):
        return ref.at[pl.ds(pl.multiple_of(r * ROW_TILE, ROW_TILE), ROW_TILE)]

    return pltpu.make_async_copy(tile(src_ref, src_row), tile(dst_ref, dst_row), sem)


def _for_each_row(n_rows, fn):
    def body(j8, carry):
        for u in range(DMA_UNROLL):
            fn(j8 * DMA_UNROLL + u, u % 2)
        return carry

    lax.fori_loop(0, n_rows // DMA_UNROLL, body, 0)


def _dispatch_kernel(pad_lo_ref, pad_n_ref, nu_ref, dest_ref, h_ref, xs_ref, sem, pad_sem, *,
                     tm, pad_bits, tile_rows, n_tiles):
    @pl.when(pl.program_id(0) == 0)
    def _():
        def fill_tile(j, carry):
            cp = pltpu.make_async_copy(
                h_ref.at[pl.ds(0, tile_rows * ROW_TILE)],
                xs_ref.at[pl.ds(pl.multiple_of(j * tile_rows * ROW_TILE, ROW_TILE),
                                tile_rows * ROW_TILE)], pad_sem)
            cp.start()
            cp.wait()
            return carry

        lax.fori_loop(nu_ref[0], n_tiles, fill_tile, 0)

        for start in (True, False):
            for e in range(N_EXPERTS):
                n = pad_n_ref[e]
                for bit in reversed(range(pad_bits)):
                    size = 1 << bit

                    @pl.when((n & size) != 0)
                    def _(n=n, e=e, bit=bit, size=size, start=start):
                        first = pad_lo_ref[e] + ((n >> (bit + 1)) << (bit + 1))
                        cp = pltpu.make_async_copy(
                            h_ref.at[pl.ds(0, size * ROW_TILE)],
                            xs_ref.at[pl.ds(pl.multiple_of(first * ROW_TILE, ROW_TILE),
                                            size * ROW_TILE)], pad_sem)
                        cp.start() if start else cp.wait()

    def copy(j, slot):
        return _row_copy(h_ref, j, xs_ref, dest_ref[0, 0, slot * tm + j], sem)

    for slot in range(2):
        _for_each_row(tm, lambda j, lane, slot=slot: copy(j, slot).start(priority=lane))
    for slot in range(2):
        _for_each_row(tm, lambda j, lane, slot=slot: copy(j, slot).wait())


def _dispatch(pad_lo, pad_n, n_used, dest3, h_rt, n_tiles, tile_rows, tm):
    assert tile_rows <= tm
    grid_spec = pltpu.PrefetchScalarGridSpec(
        num_scalar_prefetch=3,
        grid=(h_rt.shape[0] // (tm * ROW_TILE),),
        in_specs=[
            pl.BlockSpec((1, 1, 2 * tm), lambda i, lo, n, nu: (i, 0, 0), memory_space=pltpu.SMEM),
            pl.BlockSpec((tm * ROW_TILE, LANES), lambda i, lo, n, nu: (i, 0)),
        ],
        out_specs=pl.BlockSpec(memory_space=pl.ANY),
        scratch_shapes=[pltpu.SemaphoreType.DMA(()), pltpu.SemaphoreType.DMA(())],
    )
    return pl.pallas_call(
        functools.partial(_dispatch_kernel, tm=tm, pad_bits=(tile_rows - 1).bit_length(),
                          tile_rows=tile_rows, n_tiles=n_tiles),
        grid_spec=grid_spec,
        out_shape=jax.ShapeDtypeStruct((n_tiles * tile_rows * ROW_TILE, LANES), F32),
        compiler_params=_cparams(("arbitrary",)),
        name="moe_dispatch",
    )(pad_lo, pad_n, n_used, dest3, h_rt)


def _combine_kernel(dest_ref, next_ref, ys_ref, x_ref, gate_ref, o_ref, buf_ref, sem, *, tm):
    i = pl.program_id(0)
    cur = lax.rem(i, 2)

    def copy(idx_ref, buf_slot, j):
        return _row_copy(ys_ref, idx_ref[0, 0, j], buf_ref.at[buf_slot], j, sem.at[buf_slot])

    @pl.when(i == 0)
    def _():
        _for_each_row(2 * tm, lambda j, lane: copy(dest_ref, 0, j).start(priority=lane))

    @pl.when(i + 1 < pl.num_programs(0))
    def _():
        _for_each_row(2 * tm, lambda j, lane: copy(next_ref, 1 - cur, j).start(priority=lane))

    _for_each_row(2 * tm, lambda j, lane: copy(dest_ref, cur, j).wait())
    g1 = gate_ref[:, 0:1]
    g2 = gate_ref[:, 1:2]
    d = x_ref.shape[1]
    for slot in range(2):
        @pl.when(cur == slot)
        def _(slot=slot):
            rows = buf_ref.at[slot]
            y1 = _from_row_tiles(rows, 0, tm, d)
            y2 = _from_row_tiles(rows, tm * ROW_TILE, tm, d)
            o_ref[...] = x_ref[...] + (g1 * y1 + g2 * y2)


def _combine(dest3, ys, x2d, gates, tm):
    t, d = x2d.shape
    n = t // tm
    return pl.pallas_call(
        functools.partial(_combine_kernel, tm=tm),
        grid=(n,),
        in_specs=[
            pl.BlockSpec((1, 1, 2 * tm), lambda i: (i, 0, 0), memory_space=pltpu.SMEM),
            pl.BlockSpec((1, 1, 2 * tm), lambda i: (jnp.minimum(i + 1, n - 1), 0, 0),
                         memory_space=pltpu.SMEM),
            pl.BlockSpec(memory_space=pl.ANY),
            pl.BlockSpec((tm, d), lambda i: (i, 0)),
            pl.BlockSpec((tm, gates.shape[1]), lambda i: (i, 0)),
        ],
        out_specs=pl.BlockSpec((tm, d), lambda i: (i, 0)),
        out_shape=jax.ShapeDtypeStruct((t, d), F32),
        scratch_shapes=[pltpu.VMEM((2, 2 * tm * ROW_TILE, LANES), F32),
                        pltpu.SemaphoreType.DMA((2,))],
        compiler_params=_cparams(("arbitrary",)),
        name="moe_combine",
    )(dest3, dest3, ys, x2d, gates)


def _pad_heads_t(w_cols, offsets):
    d = w_cols.shape[0]
    nh = w_cols.shape[1] // HEAD_DIM
    wt = w_cols.T.reshape(nh, HEAD_DIM, d)
    hi = jnp.asarray(offsets, I32).reshape(nh, 1, 1) > 0
    z = jnp.zeros_like(wt)
    slab = jnp.concatenate([jnp.where(hi, z, wt), jnp.where(hi, wt, z)], axis=1)
    return slab.reshape(nh * HEAD_SLAB, d)


def _pad_head_cols(w):
    rows = w.shape[0]
    nh = w.shape[1] // HEAD_DIM
    w3 = w.reshape(rows, nh, HEAD_DIM)
    return jnp.concatenate([w3, jnp.zeros_like(w3)], axis=2).reshape(rows, nh * HEAD_SLAB)


def _head_block_ones(width):
    idx = jnp.arange(width) // HEAD_DIM
    return (idx[:, None] == idx[None, :]).astype(F32)


def _tile_gain(g, n_heads):
    return jnp.tile(g.astype(F32), n_heads).reshape(1, n_heads * HEAD_DIM)


def _s5_params(lam_re, lam_im, log_dt, b_re, b_im, c_re, c_im):
    f32 = F32
    g, p = lam_re.shape
    n = b_re.shape[2]
    lam = lax.complex(lam_re.astype(f32), lam_im.astype(f32))
    dt = jnp.exp(log_dt.astype(f32))[:, None]
    lam_bar = jnp.exp(lam * dt)
    b_bar = ((lam_bar - 1.0) / lam)[..., None] * lax.complex(b_re.astype(f32), b_im.astype(f32))
    ns = g // S5_GROUPS_PER_SLAB
    eye = jnp.eye(S5_GROUPS_PER_SLAB, dtype=f32)

    def in_map(part):
        blk = part.reshape(ns, S5_GROUPS_PER_SLAB, p, n).transpose(0, 1, 3, 2)
        return jnp.einsum('sgnp,gh->sgnhp', blk, eye).reshape(
            ns, S5_GROUPS_PER_SLAB * n, S5_GROUPS_PER_SLAB * p)

    def out_map(part):
        blk = part.reshape(ns, S5_GROUPS_PER_SLAB, n, p).transpose(0, 1, 3, 2)
        return jnp.einsum('sgpn,gh->sgphn', blk, eye).reshape(
            ns, S5_GROUPS_PER_SLAB * p, S5_GROUPS_PER_SLAB * n)

    bd = jnp.concatenate([in_map(jnp.real(b_bar)), in_map(jnp.imag(b_bar))], axis=2)
    cd = jnp.concatenate([out_map(c_re.astype(f32)), out_map(-c_im.astype(f32))], axis=1)
    ar = jnp.real(lam_bar).reshape(ns, S5_GROUPS_PER_SLAB * p)
    ai = jnp.imag(lam_bar).reshape(ns, S5_GROUPS_PER_SLAB * p)
    return bd.astype(BF16), cd.astype(BF16), ar, ai


def kernel(x, mem, l0_mix_norm, l0_w_in, l0_s5_lam_re, l0_s5_lam_im, l0_s5_log_dt, l0_s5_b_re, l0_s5_b_im, l0_s5_c_re, l0_s5_c_im, l0_s5_d, l0_s5_w_glu, l0_s5_b_glu, l0_mem_norm, l0_w_mem_k, l0_w_mem_v, l0_mem_q_gain, l0_mem_k_gain, l0_w_out, l0_ffn_norm, l0_ffn_w_gate, l0_ffn_w_up, l0_ffn_w_down, kv_norm, kv_w_k, kv_w_v, kv_k_gain, l1_mix_norm, l1_w_in, l1_moba_q_gain, l1_mem_norm, l1_w_mem_k, l1_w_mem_v, l1_mem_q_gain, l1_mem_k_gain, l1_w_out, l1_ffn_norm, l1_moe_router, l1_moe_w_gate, l1_moe_w_up, l1_moe_w_down):
    batch, seq, d = x.shape
    t = batch * seq
    main_w = l0_s5_w_glu.shape[0]
    mem_w = l0_w_mem_k.shape[1]
    n_q = main_w // HEAD_DIM
    tm = 512
    row = lambda v: v.astype(F32).reshape(1, -1)

    mem_off = [HEAD_DIM * (h % 2) for h in range(MEM_HEADS)]
    ones_kv = _head_block_ones(mem_w)
    x2d = x.reshape(t, d)

    main_in, qm0_t = _inproj0(
        x2d, row(l0_mix_norm), l0_w_in[:, :main_w].astype(BF16),
        _pad_heads_t(l0_w_in[:, main_w:], mem_off).astype(BF16), batch, seq, tm)
    k0, v0_t = _mem_kv(
        mem, row(l0_mem_norm), l0_w_mem_k.astype(BF16), l0_w_mem_v.T.astype(BF16), ones_kv,
        _tile_gain(l0_mem_k_gain, MEM_HEADS) * _tile_gain(l0_mem_q_gain, MEM_HEADS))

    bd, cd, ar, ai = _s5_params(l0_s5_lam_re, l0_s5_lam_im, l0_s5_log_dt, l0_s5_b_re, l0_s5_b_im,
                                l0_s5_c_re, l0_s5_c_im)
    s5_out = _s5(main_in.reshape(batch, seq, main_w), bd, cd, ar, ai, row(l0_s5_d),
                 l0_s5_w_glu.astype(BF16), row(l0_s5_b_glu), 64).reshape(t, main_w)

    x1, h1 = _outproj0(s5_out, (qm0_t, k0, v0_t), l0_w_out[:main_w].astype(BF16),
                       l0_w_out[main_w:].astype(BF16), x2d, row(l0_ffn_norm), seq, tm)
    x2 = _ffn(h1, x1, l0_ffn_w_gate.astype(BF16), l0_ffn_w_up.astype(BF16),
              l0_ffn_w_down.astype(BF16), tm)

    wq1_t = jnp.concatenate([l1_w_in[:, :main_w].T,
                             _pad_heads_t(l1_w_in[:, main_w:], mem_off)], axis=0).astype(BF16)
    q_t, qm1_t, kq, km, v_t = _proj1(
        x2, row(l1_mix_norm), row(kv_norm), wq1_t, _pad_head_cols(kv_w_k).astype(BF16),
        kv_w_v.T.astype(BF16), _pad_head_cols(_tile_gain(kv_k_gain, MOBA_KV_HEADS)),
        _pad_head_cols(_tile_gain(l1_moba_q_gain, MOBA_KV_HEADS)), batch, seq, tm)
    nb = seq // MOBA_BLOCK
    moba_t = _moba(q_t, kq.reshape(batch, seq, -1), v_t, km.reshape(batch, nb, -1))
    k1, v1_t = _mem_kv(
        mem, row(l1_mem_norm), l1_w_mem_k.astype(BF16), l1_w_mem_v.T.astype(BF16), ones_kv,
        _tile_gain(l1_mem_k_gain, MEM_HEADS) * _tile_gain(l1_mem_q_gain, MEM_HEADS))

    wr_t = jnp.zeros((ROUTER_ROWS, d), F32).at[:N_EXPERTS].set(l1_moe_router.astype(F32).T)
    wr_hi = wr_t.astype(BF16)
    wr_lo = (wr_t - wr_hi.astype(F32)).astype(BF16)
    x3, h3, rec, cnt = _outproj1(moba_t, (qm1_t, k1, v1_t), l1_w_out[:main_w].astype(BF16),
                                 l1_w_out[main_w:].astype(BF16), x2, row(l1_ffn_norm),
                                 wr_hi, wr_lo, seq, tm)

    assert d == ROW_TILE * LANES
    tme = 1024
    counts = cnt[:N_EXPERTS, 0].astype(I32)
    padded = ((counts + tme - 1) // tme) * tme
    ends = jnp.cumsum(padded)
    starts = ends - padded
    max_tiles = (2 * t) // tme + N_EXPERTS
    def lookup(table, idx):
        hit = idx[None, :] == jnp.arange(N_EXPERTS, dtype=I32)[:, None]
        return jnp.sum(jnp.where(hit, table[:, None], 0), axis=0)

    tile_ids = jnp.arange(max_tiles, dtype=I32)
    tile_expert = jnp.minimum(
        jnp.sum((ends // tme)[None, :] <= tile_ids[:, None], axis=1), N_EXPERTS - 1).astype(I32)
    n_valid = jnp.clip(lookup(counts, tile_expert) + lookup(starts, tile_expert)
                       - tile_ids * tme, 0, tme)
    n_used = (ends[-1] // tme).astype(I32).reshape(1)
    d1 = lookup(starts, rec[0].astype(I32)) + rec[2].astype(I32)
    d2 = lookup(starts, rec[1].astype(I32)) + rec[3].astype(I32)
    gates = rec[4:6].T
    def tile_dest(tile):
        return jnp.concatenate([d1.reshape(t // tile, 1, tile), d2.reshape(t // tile, 1, tile)],
                               axis=2)

    tmd, tmc = 1024, 512
    xs = _dispatch(starts + counts, padded - counts, n_used, tile_dest(tmd), h3, max_tiles, tme,
                   tmd)
    ys = _moe_ffn(tile_expert, n_valid.astype(I32), n_used, xs, l1_moe_w_gate, l1_moe_w_up,
                  l1_moe_w_down, tme, 512)
    out = _combine(tile_dest(tmc), ys, x3, gates, tmc)
    return out.reshape(batch, seq, d)
```

```python
import functools

import jax
import jax.numpy as jnp
from jax import lax
from jax.experimental import pallas as pl
from jax.experimental.pallas import tpu as pltpu

F32 = jnp.float32
BF16 = jnp.bfloat16
I32 = jnp.int32

RMS_EPS = 1e-6
NEG_INF = -1e30
HEAD_DIM = 64
HEAD_SLAB = 128
MEM_HEADS = 4
MOBA_KV_HEADS = 4
MOBA_GROUP = 3
MOBA_BLOCK = 256
MOBA_TOP_BLOCKS = 3
MOBA_KV_PER_STEP = 2
S5_GROUP_DIM = 16
S5_STATE = 64
S5_GROUPS_PER_SLAB = HEAD_SLAB // S5_GROUP_DIM
S5_SLAB_STATES = S5_GROUPS_PER_SLAB * S5_STATE
N_EXPERTS = 8
ROUTER_LANES = 128
ROUTER_ROWS = 16
SUBLANES = 8
LANES = 128
ROW_TILE = SUBLANES
DMA_UNROLL = 8
LOG2_E = 1.4426950408889634
V_ROWS = HEAD_DIM + 16

VMEM_LIMIT_BYTES = 56 * 1024 * 1024

_NT = (((1,), (1,)), ((), ()))
_TN = (((0,), (0,)), ((), ()))


def _cparams(sem):
    return pltpu.CompilerParams(dimension_semantics=sem, vmem_limit_bytes=VMEM_LIMIT_BYTES)


def _rms(x, g):
    return x * lax.rsqrt(jnp.mean(x * x, axis=-1, keepdims=True) + RMS_EPS) * g


def _store_vt_with_ones(vt_ref, vt, n_heads):
    n = vt.shape[1]
    ones_row = jnp.where(lax.broadcasted_iota(I32, (V_ROWS - HEAD_DIM, n), 0) == 0, 1.0, 0.0)
    for h in range(n_heads):
        vt_ref[0, V_ROWS * h:V_ROWS * h + HEAD_DIM, :] = (
            vt[HEAD_DIM * h:HEAD_DIM * (h + 1), :].astype(BF16))
        vt_ref[0, V_ROWS * h + HEAD_DIM:V_ROWS * (h + 1), :] = ones_row.astype(BF16)


def _slab_norm_t(qt, n_heads, scale):
    outs = []
    for h in range(n_heads):
        s = qt[HEAD_SLAB * h:HEAD_SLAB * (h + 1), :]
        ms = jnp.sum(s * s, axis=0, keepdims=True) * (1.0 / HEAD_DIM)
        outs.append(s * (lax.rsqrt(ms + RMS_EPS) * scale))
    return outs


def _inproj0_kernel(x_ref, g_ref, wm_ref, wqt_ref, main_ref, qt_ref):
    hb = _rms(x_ref[...], g_ref[...]).astype(BF16)
    main_ref[...] = jnp.dot(hb, wm_ref[...], preferred_element_type=F32)
    qt = lax.dot_general(wqt_ref[...], hb, _NT, preferred_element_type=F32)
    for h, s in enumerate(_slab_norm_t(qt, MEM_HEADS, HEAD_DIM ** -0.5 * LOG2_E)):
        qt_ref[0, HEAD_SLAB * h:HEAD_SLAB * (h + 1), :] = s.astype(BF16)


def _inproj0(x2d, g, wm, wqt, batch, seq, tm):
    t, d = x2d.shape
    per_b = seq // tm
    nm = wm.shape[1]
    nq = wqt.shape[0]
    return pl.pallas_call(
        _inproj0_kernel,
        grid=(t // tm,),
        in_specs=[
            pl.BlockSpec((tm, d), lambda i: (i, 0)),
            pl.BlockSpec((1, d), lambda i: (0, 0)),
            pl.BlockSpec((d, nm), lambda i: (0, 0)),
            pl.BlockSpec((nq, d), lambda i: (0, 0)),
        ],
        out_specs=[
            pl.BlockSpec((tm, nm), lambda i: (i, 0)),
            pl.BlockSpec((1, nq, tm), lambda i: (i // per_b, 0, i % per_b)),
        ],
        out_shape=[
            jax.ShapeDtypeStruct((t, nm), F32),
            jax.ShapeDtypeStruct((batch, nq, seq), BF16),
        ],
        compiler_params=_cparams(("parallel",)),
        name="inproj0",
    )(x2d, g, wm, wqt)


def _mem_kv_kernel(mem_ref, g_ref, wk_ref, wvt_ref, ones_ref, kg_ref, k_ref, vt_ref):
    mb = _rms(mem_ref[0], g_ref[...]).astype(BF16)
    k = jnp.dot(mb, wk_ref[...], preferred_element_type=F32)
    ms = jnp.dot(k * k, ones_ref[...], preferred_element_type=F32,
                 precision=lax.Precision.HIGHEST) * (1.0 / HEAD_DIM)
    k_ref[0] = (k * lax.rsqrt(ms + RMS_EPS) * kg_ref[...]).astype(BF16)
    _store_vt_with_ones(vt_ref, lax.dot_general(wvt_ref[...], mb, _NT,
                                                preferred_element_type=F32), MEM_HEADS)


def _mem_kv(mem, g, wk, wvt, ones, kg):
    b, m, d = mem.shape
    w = wk.shape[1]
    v_rows = MEM_HEADS * V_ROWS
    return pl.pallas_call(
        _mem_kv_kernel,
        grid=(b,),
        in_specs=[
            pl.BlockSpec((1, m, d), lambda i: (i, 0, 0)),
            pl.BlockSpec((1, d), lambda i: (0, 0)),
            pl.BlockSpec((d, w), lambda i: (0, 0)),
            pl.BlockSpec((w, d), lambda i: (0, 0)),
            pl.BlockSpec((w, w), lambda i: (0, 0)),
            pl.BlockSpec((1, w), lambda i: (0, 0)),
        ],
        out_specs=[
            pl.BlockSpec((1, m, w), lambda i: (i, 0, 0)),
            pl.BlockSpec((1, v_rows, m), lambda i: (i, 0, 0)),
        ],
        out_shape=[
            jax.ShapeDtypeStruct((b, m, w), BF16),
            jax.ShapeDtypeStruct((b, v_rows, m), BF16),
        ],
        compiler_params=_cparams(("parallel",)),
        name="mem_kv",
    )(mem, g, wk, wvt, ones, kg)


def _mem_scores_t(qt_ref, k_ref):
    scores = []
    for h in range(MEM_HEADS):
        pair = h // 2
        k2 = k_ref[0, :, HEAD_SLAB * pair:HEAD_SLAB * (pair + 1)]
        q = qt_ref[0, HEAD_SLAB * h:HEAD_SLAB * (h + 1), :]
        scores.append(jnp.dot(k2, q, preferred_element_type=F32))
    return scores


def _mem_attend_t(scores, vt_ref):
    outs = []
    for h, s in enumerate(scores):
        p = jnp.exp2(s - jnp.max(s, axis=0, keepdims=True)).astype(BF16)
        v = vt_ref[0, V_ROWS * h:V_ROWS * (h + 1), :]
        acc = jnp.dot(v, p, preferred_element_type=F32)
        outs.append((acc[0:HEAD_DIM, :] / acc[HEAD_DIM:HEAD_DIM + 1, :]).astype(BF16))
    return jnp.concatenate(outs, axis=0)


def _mem_attn_specs(qt, k, vt, per_b, tm):
    return [
        pl.BlockSpec((1, qt.shape[1], tm), lambda i: (i // per_b, 0, i % per_b)),
        pl.BlockSpec((1,) + k.shape[1:], lambda i: (i // per_b, 0, 0)),
        pl.BlockSpec((1,) + vt.shape[1:], lambda i: (i // per_b, 0, 0)),
    ]


def _s5_kernel(u_ref, bd_ref, cd_ref, ar_ref, ai_ref, d_ref, wglu_ref, bglu_ref, o_ref,
               buf_ref, st_ref, ut_ref, ot_ref, *, ts, n_slabs):
    half = S5_SLAB_STATES
    width = 2 * half
    batch = u_ref.shape[0]

    @pl.when(pl.program_id(0) == 0)
    def _():
        st_ref[...] = jnp.zeros_like(st_ref)

    for b in range(batch):
        for j in range(n_slabs):
            ut_ref[j, pl.ds(b, ts, stride=batch), :] = u_ref[b, :, HEAD_SLAB * j:HEAD_SLAB * (j + 1)]

    for j in range(n_slabs):
        buf_ref[:, width * j:width * (j + 1)] = jnp.dot(
            ut_ref[j].astype(BF16), bd_ref[j], preferred_element_type=F32)

    for j in range(n_slabs):
        re = slice(width * j, width * j + half)
        im = slice(width * j + half, width * (j + 1))
        ar = jnp.broadcast_to(ar_ref[j:j + 1, :], (SUBLANES, half))
        ai = jnp.broadcast_to(ai_ref[j:j + 1, :], (SUBLANES, half))

        def step(t, carry, re=re, im=im, ar=ar, ai=ai):
            xr, xi = carry
            rows = pl.ds(pl.multiple_of(t * SUBLANES, SUBLANES), SUBLANES)
            nxr = ar * xr - ai * xi + buf_ref[rows, re]
            nxi = ar * xi + ai * xr + buf_ref[rows, im]
            buf_ref[rows, re] = nxr
            buf_ref[rows, im] = nxi
            return nxr, nxi

        xr, xi = lax.fori_loop(0, ts, step, (st_ref[:, re], st_ref[:, im]), unroll=True)
        st_ref[:, re] = xr
        st_ref[:, im] = xi

    ys = []
    for j in range(n_slabs):
        cols = slice(HEAD_SLAB * j, HEAD_SLAB * (j + 1))
        ys.append(jnp.dot(buf_ref[:, width * j:width * (j + 1)].astype(BF16), cd_ref[j],
                          preferred_element_type=F32) + d_ref[:, cols] * ut_ref[j])
    g = jax.nn.gelu(jnp.concatenate(ys, axis=1))
    z = jnp.dot(g.astype(BF16), wglu_ref[...], preferred_element_type=F32) + bglu_ref[...]
    out = g / (1.0 + jnp.exp(-z))
    for j in range(n_slabs):
        ot_ref[j] = out[:, HEAD_SLAB * j:HEAD_SLAB * (j + 1)]
    for b in range(batch):
        for j in range(n_slabs):
            o_ref[b, :, HEAD_SLAB * j:HEAD_SLAB * (j + 1)] = (
                ot_ref[j, pl.ds(b, ts, stride=batch), :].astype(BF16))


def _s5(u, bd, cd, ar, ai, dskip, wglu, bglu, ts):
    batch, seq, c = u.shape
    n_slabs = c // HEAD_SLAB
    width = 2 * S5_SLAB_STATES
    tile = ts * batch
    return pl.pallas_call(
        functools.partial(_s5_kernel, ts=ts, n_slabs=n_slabs),
        grid=(seq // ts,),
        in_specs=[
            pl.BlockSpec((batch, ts, c), lambda i: (0, i, 0)),
            pl.BlockSpec((n_slabs, HEAD_SLAB, width), lambda i: (0, 0, 0)),
            pl.BlockSpec((n_slabs, width, HEAD_SLAB), lambda i: (0, 0, 0)),
            pl.BlockSpec((n_slabs, S5_SLAB_STATES), lambda i: (0, 0)),
            pl.BlockSpec((n_slabs, S5_SLAB_STATES), lambda i: (0, 0)),
            pl.BlockSpec((1, c), lambda i: (0, 0)),
            pl.BlockSpec((c, c), lambda i: (0, 0)),
            pl.BlockSpec((1, c), lambda i: (0, 0)),
        ],
        out_specs=pl.BlockSpec((batch, ts, c), lambda i: (0, i, 0)),
        out_shape=jax.ShapeDtypeStruct((batch, seq, c), BF16),
        scratch_shapes=[
            pltpu.VMEM((tile, n_slabs * width), F32),
            pltpu.VMEM((batch, n_slabs * width), F32),
            pltpu.VMEM((n_slabs, tile, HEAD_SLAB), F32),
            pltpu.VMEM((n_slabs, tile, HEAD_SLAB), F32),
        ],
        compiler_params=_cparams(("arbitrary",)),
        name="s5_mixer",
    )(u, bd, cd, ar, ai, dskip, wglu, bglu)


def _outproj_body(a, a_transposed, mem_refs, wa_ref, wb_ref, x_ref, g_ref):
    qt_ref, mk_ref, mvt_ref = mem_refs
    dn = _TN if a_transposed else (((1,), (0,)), ((), ()))
    scores = _mem_scores_t(qt_ref, mk_ref)
    y = lax.dot_general(a, wa_ref[...], dn, preferred_element_type=F32)
    y = y + lax.dot_general(_mem_attend_t(scores, mvt_ref), wb_ref[...], _TN,
                            preferred_element_type=F32)
    xn = x_ref[...] + y
    return xn, _rms(xn, g_ref[...])


def _outproj0_kernel(a_ref, qt_ref, mk_ref, mvt_ref, wa_ref, wb_ref, x_ref, g_ref, xo_ref, h_ref):
    xn, h = _outproj_body(a_ref[...], False, (qt_ref, mk_ref, mvt_ref), wa_ref, wb_ref, x_ref,
                          g_ref)
    xo_ref[...] = xn
    h_ref[...] = h.astype(BF16)


def _outproj0(a, mem_qkv, wa, wb, x2d, g, seq, tm):
    t, d = x2d.shape
    per_b = seq // tm
    ka, kb = a.shape[1], wb.shape[0]
    return pl.pallas_call(
        _outproj0_kernel,
        grid=(t // tm,),
        in_specs=[
            pl.BlockSpec((tm, ka), lambda i: (i, 0)),
            *_mem_attn_specs(*mem_qkv, per_b, tm),
            pl.BlockSpec((ka, d), lambda i: (0, 0)),
            pl.BlockSpec((kb, d), lambda i: (0, 0)),
            pl.BlockSpec((tm, d), lambda i: (i, 0)),
            pl.BlockSpec((1, d), lambda i: (0, 0)),
        ],
        out_specs=[
            pl.BlockSpec((tm, d), lambda i: (i, 0)),
            pl.BlockSpec((tm, d), lambda i: (i, 0)),
        ],
        out_shape=[
            jax.ShapeDtypeStruct((t, d), F32),
            jax.ShapeDtypeStruct((t, d), BF16),
        ],
        compiler_params=_cparams(("parallel",)),
        name="outproj0",
    )(a, *mem_qkv, wa, wb, x2d, g)


def _outproj1_kernel(at_ref, qt_ref, mk_ref, mvt_ref, wa_ref, wb_ref, x_ref, g_ref, wrh_ref,
                     wrl_ref, xo_ref, h_ref, rec_ref, cnt_ref, run_ref, *, tm):
    i = pl.program_id(0)

    @pl.when(i == 0)
    def _():
        run_ref[...] = jnp.zeros_like(run_ref)

    xn, h = _outproj_body(at_ref[0], True, (qt_ref, mk_ref, mvt_ref), wa_ref, wb_ref, x_ref,
                          g_ref)
    xo_ref[...] = xn
    _to_row_tiles(h_ref, h)

    h_hi = h.astype(BF16)
    h_lo = (h - h_hi.astype(F32)).astype(BF16)
    logits = (lax.dot_general(wrh_ref[...], h_hi, _NT, preferred_element_type=F32)
              + lax.dot_general(wrh_ref[...], h_lo, _NT, preferred_element_type=F32)
              + lax.dot_general(wrl_ref[...], h_hi, _NT, preferred_element_type=F32))
    ex = lax.broadcasted_iota(I32, (ROUTER_ROWS, tm), 0)
    logits = jnp.where(ex < N_EXPERTS, logits, -jnp.inf)
    m1 = jnp.max(logits, axis=0, keepdims=True)
    e1 = jnp.min(jnp.where(logits == m1, ex, ROUTER_ROWS), axis=0, keepdims=True)
    rest = jnp.where(ex == e1, -jnp.inf, logits)
    m2 = jnp.max(rest, axis=0, keepdims=True)
    e2 = jnp.min(jnp.where(rest == m2, ex, ROUTER_ROWS), axis=0, keepdims=True)
    w2 = jnp.exp(m2 - m1)
    g1 = 1.0 / (1.0 + w2)
    g2 = w2 / (1.0 + w2)

    hot = jnp.where((ex == e1) | (ex == e2), 1.0, 0.0)
    r_io = lax.broadcasted_iota(I32, (tm, tm), 0)
    c_io = lax.broadcasted_iota(I32, (tm, tm), 1)
    triu = jnp.where(r_io < c_io, 1.0, 0.0).astype(BF16)
    before = (jnp.dot(hot.astype(BF16), triu, preferred_element_type=F32)
              + run_ref[:, 0:1])
    r1 = jnp.sum(jnp.where(ex == e1, before, 0.0), axis=0, keepdims=True)
    r2 = jnp.sum(jnp.where(ex == e2, before, 0.0), axis=0, keepdims=True)
    run = run_ref[...] + jnp.sum(hot, axis=1, keepdims=True)
    run_ref[...] = run
    cnt_ref[...] = run

    slot = lax.broadcasted_iota(I32, (SUBLANES, tm), 0)
    rec = jnp.where(slot == 0, e1.astype(F32), 0.0)
    rec = jnp.where(slot == 1, e2.astype(F32), rec)
    rec = jnp.where(slot == 2, r1, rec)
    rec = jnp.where(slot == 3, r2, rec)
    rec = jnp.where(slot == 4, g1, rec)
    rec = jnp.where(slot == 5, g2, rec)
    rec_ref[...] = rec


def _outproj1(at, mem_qkv, wa, wb, x2d, g, wr_hi, wr_lo, seq, tm):
    t, d = x2d.shape
    per_b = seq // tm
    ka, kb = at.shape[1], wb.shape[0]
    return pl.pallas_call(
        functools.partial(_outproj1_kernel, tm=tm),
        grid=(t // tm,),
        in_specs=[
            pl.BlockSpec((1, ka, tm), lambda i: (i // per_b, 0, i % per_b)),
            *_mem_attn_specs(*mem_qkv, per_b, tm),
            pl.BlockSpec((ka, d), lambda i: (0, 0)),
            pl.BlockSpec((kb, d), lambda i: (0, 0)),
            pl.BlockSpec((tm, d), lambda i: (i, 0)),
            pl.BlockSpec((1, d), lambda i: (0, 0)),
            pl.BlockSpec((ROUTER_ROWS, d), lambda i: (0, 0)),
            pl.BlockSpec((ROUTER_ROWS, d), lambda i: (0, 0)),
        ],
        out_specs=[
            pl.BlockSpec((tm, d), lambda i: (i, 0)),
            pl.BlockSpec((tm * ROW_TILE, LANES), lambda i: (i, 0)),
            pl.BlockSpec((SUBLANES, tm), lambda i: (0, i)),
            pl.BlockSpec((ROUTER_ROWS, ROUTER_LANES), lambda i: (0, 0)),
        ],
        out_shape=[
            jax.ShapeDtypeStruct((t, d), F32),
            jax.ShapeDtypeStruct((t * ROW_TILE, LANES), F32),
            jax.ShapeDtypeStruct((SUBLANES, t), F32),
            jax.ShapeDtypeStruct((ROUTER_ROWS, ROUTER_LANES), F32),
        ],
        scratch_shapes=[pltpu.VMEM((ROUTER_ROWS, ROUTER_LANES), F32)],
        compiler_params=_cparams(("arbitrary",)),
        name="outproj1_router",
    )(at, *mem_qkv, wa, wb, x2d, g, wr_hi, wr_lo)


def _ffn_kernel(h_ref, res_ref, wg_ref, wu_ref, wd_ref, o_ref):
    half = h_ref.shape[0] // 2
    for rows in (slice(0, half), slice(half, 2 * half)):
        hb = h_ref[rows, :]
        gt = jnp.dot(hb, wg_ref[...], preferred_element_type=F32)
        up = jnp.dot(hb, wu_ref[...], preferred_element_type=F32)
        mid = (gt / (1.0 + jnp.exp(-gt))) * up
        o_ref[rows, :] = res_ref[rows, :] + jnp.dot(mid.astype(BF16), wd_ref[...],
                                                    preferred_element_type=F32)


def _ffn(h, res, wg, wu, wd, tm):
    rows, d = h.shape
    ff = wg.shape[1]
    resident = dict(pipeline_mode=pl.Buffered(1))
    return pl.pallas_call(
        _ffn_kernel,
        grid=(rows // tm,),
        in_specs=[
            pl.BlockSpec((tm, d), lambda i: (i, 0)),
            pl.BlockSpec((tm, d), lambda i: (i, 0)),
            pl.BlockSpec((d, ff), lambda i: (0, 0), **resident),
            pl.BlockSpec((d, ff), lambda i: (0, 0), **resident),
            pl.BlockSpec((ff, d), lambda i: (0, 0), **resident),
        ],
        out_specs=pl.BlockSpec((tm, d), lambda i: (i, 0)),
        out_shape=jax.ShapeDtypeStruct((rows, d), F32),
        compiler_params=_cparams(("parallel",)),
        name="ffn_dense",
    )(h, res, wg, wu, wd)


def _from_row_tiles(ref, base, n_rows, n_feat):
    parts = [ref[pl.ds(base + k, n_rows, stride=ROW_TILE), :] for k in range(n_feat // LANES)]
    return jnp.concatenate(parts, axis=1)


def _to_row_tiles(ref, value):
    for k in range(value.shape[1] // LANES):
        ref[pl.ds(k, value.shape[0], stride=ROW_TILE), :] = value[:, LANES * k:LANES * (k + 1)]


def _moe_ffn_kernel(te_ref, nv_ref, nu_ref, xs_ref, wg_ref, wu_ref, wd_ref, o_ref,
                    acc_ref, xb_ref, *, tm):
    del te_ref
    i = pl.program_id(0)
    f = pl.program_id(1)
    nf = pl.num_programs(1)
    half = tm // 2
    d = xb_ref.shape[1]

    def partial_out(row_sets):
        wg = wg_ref[0].astype(BF16)
        wu = wu_ref[0].astype(BF16)
        wd = wd_ref[0].astype(BF16)
        mids = []
        for rows in row_sets:
            xb = xb_ref[rows, :]
            gt = jnp.dot(xb, wg, preferred_element_type=F32)
            up = jnp.dot(xb, wu, preferred_element_type=F32)
            mids.append(((gt / (1.0 + jnp.exp(-gt))) * up).astype(BF16))
        for rows, mid in zip(row_sets, mids):
            acc_ref[rows, :] += jnp.dot(mid, wd, preferred_element_type=F32)

    @pl.when(i < nu_ref[0])
    def _():
        @pl.when(f == 0)
        def _():
            acc_ref[...] = jnp.zeros_like(acc_ref)
            xb_ref[...] = _from_row_tiles(xs_ref, 0, tm, d).astype(BF16)

        @pl.when(nv_ref[i] > half)
        def _():
            partial_out([slice(0, half), slice(half, tm)])

        @pl.when(nv_ref[i] <= half)
        def _():
            partial_out([slice(0, half)])

        @pl.when(f == nf - 1)
        def _():
            _to_row_tiles(o_ref, acc_ref[...])

    @pl.when((i >= nu_ref[0]) & (f == nf - 1))
    def _():
        o_ref[...] = jnp.zeros_like(o_ref)


def _moe_ffn(tile_expert, n_valid, n_used, xs_rt, wg, wu, wd, tm, tf):
    d = wg.shape[1]
    ff = wg.shape[2]
    nf = ff // tf
    n_tiles = xs_rt.shape[0] // (tm * ROW_TILE)

    def live(i, nu):
        return jnp.minimum(i, jnp.maximum(nu[0] - 1, 0))

    def f_of(i, f, nu):
        return jnp.where(i < nu[0], f, nf - 1)

    def wcol_map(i, f, te, nv, nu):
        return (te[live(i, nu)], 0, f_of(i, f, nu))

    def wrow_map(i, f, te, nv, nu):
        return (te[live(i, nu)], f_of(i, f, nu), 0)

    grid_spec = pltpu.PrefetchScalarGridSpec(
        num_scalar_prefetch=3,
        grid=(n_tiles, nf),
        in_specs=[
            pl.BlockSpec((tm * ROW_TILE, LANES), lambda i, f, te, nv, nu: (live(i, nu), 0)),
            pl.BlockSpec((1, d, tf), wcol_map),
            pl.BlockSpec((1, d, tf), wcol_map),
            pl.BlockSpec((1, tf, d), wrow_map),
        ],
        out_specs=pl.BlockSpec((tm * ROW_TILE, LANES), lambda i, f, te, nv, nu: (i, 0)),
        scratch_shapes=[pltpu.VMEM((tm, d), F32), pltpu.VMEM((tm, d), BF16)],
    )
    return pl.pallas_call(
        functools.partial(_moe_ffn_kernel, tm=tm),
        grid_spec=grid_spec,
        out_shape=jax.ShapeDtypeStruct(xs_rt.shape, F32),
        compiler_params=_cparams(("arbitrary", "arbitrary")),
        name="ffn_moe",
    )(tile_expert, n_valid, n_used, xs_rt, wg, wu, wd)


def _proj1_kernel(x_ref, g1_ref, g2_ref, wqt_ref, wk_ref, wvt_ref, kg_ref, qg_ref,
                  qt_ref, qmt_ref, k_ref, km_ref, vt_ref, *, tm, n_q, per_b):
    x = x_ref[...]
    h1 = _rms(x, g1_ref[...]).astype(BF16)
    h2 = _rms(x, g2_ref[...]).astype(BF16)
    qt = lax.dot_general(wqt_ref[...], h1, _NT, preferred_element_type=F32)

    for h in range(n_q):
        s = qt[HEAD_DIM * h:HEAD_DIM * (h + 1), :]
        ms = jnp.mean(s * s, axis=0, keepdims=True)
        qt_ref[0, HEAD_DIM * h:HEAD_DIM * (h + 1), :] = (
            s * (lax.rsqrt(ms + RMS_EPS) * (HEAD_DIM ** -0.5 * LOG2_E))).astype(BF16)
    qm = qt[HEAD_DIM * n_q:, :]
    for h, s in enumerate(_slab_norm_t(qm, MEM_HEADS, HEAD_DIM ** -0.5 * LOG2_E)):
        qmt_ref[0, HEAD_SLAB * h:HEAD_SLAB * (h + 1), :] = s.astype(BF16)

    k = jnp.dot(h2, wk_ref[...], preferred_element_type=F32)
    nblk = tm // MOBA_BLOCK
    first_blk = lax.rem(pl.program_id(0), per_b) * nblk
    lane = lax.broadcasted_iota(I32, (MOBA_BLOCK, HEAD_SLAB), 1)
    for kv in range(MOBA_KV_HEADS):
        cols = slice(HEAD_SLAB * kv, HEAD_SLAB * (kv + 1))
        ks = k[:, cols]
        ms = jnp.sum(ks * ks, axis=-1, keepdims=True) * (1.0 / HEAD_DIM)
        kn = ks * lax.rsqrt(ms + RMS_EPS) * kg_ref[:, cols]
        kq = kn * qg_ref[:, cols]
        for j in range(nblk):
            rows = slice(MOBA_BLOCK * j, MOBA_BLOCK * (j + 1))
            km_ref[j, :, cols] = jnp.mean(kn[rows], axis=0, keepdims=True) * qg_ref[:, cols]
            k_ref[rows, cols] = jnp.where(lane == HEAD_DIM + first_blk + j, 1.0,
                                          kq[rows]).astype(BF16)

    _store_vt_with_ones(vt_ref, lax.dot_general(wvt_ref[...], h2, _NT,
                                                preferred_element_type=F32), MOBA_KV_HEADS)


def _proj1(x2d, g1, g2, wqt, wk, wvt, kg, qg, batch, seq, tm):
    t, d = x2d.shape
    per_b = seq // tm
    nq_rows = wqt.shape[0]
    n_q = (nq_rows - MEM_HEADS * HEAD_SLAB) // HEAD_DIM
    kw = wk.shape[1]
    vw = wvt.shape[0]
    v_rows = MOBA_KV_HEADS * V_ROWS
    nblk = tm // MOBA_BLOCK
    return pl.pallas_call(
        functools.partial(_proj1_kernel, tm=tm, n_q=n_q, per_b=per_b),
        grid=(t // tm,),
        in_specs=[
            pl.BlockSpec((tm, d), lambda i: (i, 0)),
            pl.BlockSpec((1, d), lambda i: (0, 0)),
            pl.BlockSpec((1, d), lambda i: (0, 0)),
            pl.BlockSpec((nq_rows, d), lambda i: (0, 0)),
            pl.BlockSpec((d, kw), lambda i: (0, 0)),
            pl.BlockSpec((vw, d), lambda i: (0, 0)),
            pl.BlockSpec((1, kw), lambda i: (0, 0)),
            pl.BlockSpec((1, kw), lambda i: (0, 0)),
        ],
        out_specs=[
            pl.BlockSpec((1, n_q * HEAD_DIM, tm), lambda i: (i // per_b, 0, i % per_b)),
            pl.BlockSpec((1, MEM_HEADS * HEAD_SLAB, tm), lambda i: (i // per_b, 0, i % per_b)),
            pl.BlockSpec((tm, kw), lambda i: (i, 0)),
            pl.BlockSpec((nblk, 1, kw), lambda i: (i, 0, 0)),
            pl.BlockSpec((1, v_rows, tm), lambda i: (i // per_b, 0, i % per_b)),
        ],
        out_shape=[
            jax.ShapeDtypeStruct((batch, n_q * HEAD_DIM, seq), BF16),
            jax.ShapeDtypeStruct((batch, MEM_HEADS * HEAD_SLAB, seq), BF16),
            jax.ShapeDtypeStruct((t, kw), BF16),
            jax.ShapeDtypeStruct((t // MOBA_BLOCK, 1, kw), F32),
            jax.ShapeDtypeStruct((batch, v_rows, seq), BF16),
        ],
        compiler_params=_cparams(("parallel",)),
        name="proj1",
    )(x2d, g1, g2, wqt, wk, wvt, kg, qg)


def _moba_kernel(qt_ref, k_ref, vt_ref, km_ref, o_ref, s_ref):
    qb = pl.program_id(2)
    nb = km_ref.shape[1]
    bq = MOBA_BLOCK
    nq = MOBA_GROUP * bq
    q_rows = MOBA_GROUP * HEAD_DIM
    blk = lax.broadcasted_iota(I32, (nb, nq), 0)

    def gated_queries(kv):
        q64 = jnp.concatenate(
            [qt_ref[0, q_rows * kv + HEAD_DIM * g:q_rows * kv + HEAD_DIM * (g + 1), :]
             for g in range(MOBA_GROUP)], axis=1)
        km = km_ref[0, :, HEAD_SLAB * kv:HEAD_SLAB * kv + HEAD_DIM]
        km_hi = km.astype(BF16)
        km_lo = (km - km_hi.astype(F32)).astype(BF16)
        gate = (jnp.dot(km_hi, q64, preferred_element_type=F32)
                + jnp.dot(km_lo, q64, preferred_element_type=F32))
        gate = jnp.where(blk < qb, gate, NEG_INF)
        rank = jnp.zeros((nb, nq), F32)
        for m in range(nb):
            gm = gate[m:m + 1, :]
            rank = rank + jnp.where(gm > gate, 1.0,
                                    jnp.where((gm == gate) & (blk > m), 1.0, 0.0))
        dropped = (blk < qb) & (rank >= MOBA_TOP_BLOCKS)
        bias = jnp.where(dropped, NEG_INF, 0.0)
        pad_rows = HEAD_SLAB - HEAD_DIM - 2 * nb
        return jnp.concatenate(
            [q64, jnp.concatenate([bias, jnp.zeros((nb, nq), F32)], axis=0).astype(BF16),
             jnp.zeros((pad_rows, nq), BF16)], axis=0)

    q3 = [gated_queries(kv) for kv in range(MOBA_KV_PER_STEP)]

    key_pos = lax.broadcasted_iota(I32, (bq, nq), 0)
    qry_pos = lax.broadcasted_iota(I32, (bq, nq), 1) & (bq - 1)
    causal = key_pos <= qry_pos

    for n_blocks in range(1, nb + 1):
        @pl.when(qb == n_blocks - 1)
        def _(n_blocks=n_blocks):
            mx = []
            for kv in range(MOBA_KV_PER_STEP):
                m = None
                for n in range(n_blocks):
                    s = jnp.dot(k_ref[0, bq * n:bq * (n + 1), HEAD_SLAB * kv:HEAD_SLAB * (kv + 1)],
                                q3[kv], preferred_element_type=F32)
                    if n == n_blocks - 1:
                        s = jnp.where(causal, s, NEG_INF)
                    s_ref[kv, n] = s
                    cm = jnp.max(s, axis=0, keepdims=True)
                    m = cm if m is None else jnp.maximum(m, cm)
                mx.append(m)
            for kv in range(MOBA_KV_PER_STEP):
                acc = jnp.zeros((V_ROWS, nq), F32)
                for n in range(n_blocks):
                    p = jnp.exp2(s_ref[kv, n] - mx[kv]).astype(BF16)
                    acc = acc + jnp.dot(vt_ref[0, V_ROWS * kv:V_ROWS * (kv + 1),
                                               bq * n:bq * (n + 1)], p,
                                        preferred_element_type=F32)
                out = acc[0:HEAD_DIM, :] / acc[HEAD_DIM:HEAD_DIM + 1, :]
                for g in range(MOBA_GROUP):
                    r0 = q_rows * kv + HEAD_DIM * g
                    o_ref[0, r0:r0 + HEAD_DIM, :] = out[:, bq * g:bq * (g + 1)].astype(BF16)


def _moba(qt, k3, vt, km3):
    b, nq_rows, s = qt.shape
    nb = s // MOBA_BLOCK
    per = MOBA_KV_PER_STEP
    rows = per * MOBA_GROUP * HEAD_DIM
    return pl.pallas_call(
        _moba_kernel,
        grid=(b, MOBA_KV_HEADS // per, nb),
        in_specs=[
            pl.BlockSpec((1, rows, MOBA_BLOCK), lambda i, j, q: (i, j, q)),
            pl.BlockSpec((1, s, per * HEAD_SLAB), lambda i, j, q: (i, 0, j)),
            pl.BlockSpec((1, per * V_ROWS, s), lambda i, j, q: (i, j, 0)),
            pl.BlockSpec((1, nb, per * HEAD_SLAB), lambda i, j, q: (i, 0, j)),
        ],
        out_specs=pl.BlockSpec((1, rows, MOBA_BLOCK), lambda i, j, q: (i, j, q)),
        out_shape=jax.ShapeDtypeStruct((b, nq_rows, s), BF16),
        scratch_shapes=[pltpu.VMEM((per, nb, MOBA_BLOCK, MOBA_GROUP * MOBA_BLOCK), F32)],
        compiler_params=_cparams(("parallel", "parallel", "arbitrary")),
        name="moba_attn",
    )(qt, k3, vt, km3)


def _row_copy(src_ref, src_row, dst_ref, dst_row, sem):
    def tile(ref, r):
        return ref.at[pl.ds(pl.multiple_of(r * ROW_TILE, ROW_TILE), ROW_TILE)]

    return pltpu.make_async_copy(tile(src_ref, src_row), tile(dst_ref, dst_row), sem)


def _for_each_row(n_rows, fn):
    def body(j8, carry):
        for u in range(DMA_UNROLL):
            fn(j8 * DMA_UNROLL + u, u % 2)
        return carry

    lax.fori_loop(0, n_rows // DMA_UNROLL, body, 0)


def _dispatch_kernel(pad_lo_ref, pad_n_ref, nu_ref, dest_ref, h_ref, xs_ref, sem, pad_sem, *,
                     tm, pad_bits, tile_rows, n_tiles):
    @pl.when(pl.program_id(0) == 0)
    def _():
        def fill_tile(j, carry):
            cp = pltpu.make_async_copy(
                h_ref.at[pl.ds(0, tile_rows * ROW_TILE)],
                xs_ref.at[pl.ds(pl.multiple_of(j * tile_rows * ROW_TILE, ROW_TILE),
                                tile_rows * ROW_TILE)], pad_sem)
            cp.start()
            cp.wait()
            return carry

        lax.fori_loop(nu_ref[0], n_tiles, fill_tile, 0)

        for start in (True, False):
            for e in range(N_EXPERTS):
                n = pad_n_ref[e]
                for bit in reversed(range(pad_bits)):
                    size = 1 << bit

                    @pl.when((n & size) != 0)
                    def _(n=n, e=e, bit=bit, size=size, start=start):
                        first = pad_lo_ref[e] + ((n >> (bit + 1)) << (bit + 1))
                        cp = pltpu.make_async_copy(
                            h_ref.at[pl.ds(0, size * ROW_TILE)],
                            xs_ref.at[pl.ds(pl.multiple_of(first * ROW_TILE, ROW_TILE),
                                            size * ROW_TILE)], pad_sem)
                        cp.start() if start else cp.wait()

    def copy(j, slot):
        return _row_copy(h_ref, j, xs_ref, dest_ref[0, 0, slot * tm + j], sem)

    for slot in range(2):
        _for_each_row(tm, lambda j, lane, slot=slot: copy(j, slot).start(priority=lane))
    for slot in range(2):
        _for_each_row(tm, lambda j, lane, slot=slot: copy(j, slot).wait())


def _dispatch(pad_lo, pad_n, n_used, dest3, h_rt, n_tiles, tile_rows, tm):
    assert tile_rows <= tm
    grid_spec = pltpu.PrefetchScalarGridSpec(
        num_scalar_prefetch=3,
        grid=(h_rt.shape[0] // (tm * ROW_TILE),),
        in_specs=[
            pl.BlockSpec((1, 1, 2 * tm), lambda i, lo, n, nu: (i, 0, 0), memory_space=pltpu.SMEM),
            pl.BlockSpec((tm * ROW_TILE, LANES), lambda i, lo, n, nu: (i, 0)),
        ],
        out_specs=pl.BlockSpec(memory_space=pl.ANY),
        scratch_shapes=[pltpu.SemaphoreType.DMA(()), pltpu.SemaphoreType.DMA(())],
    )
    return pl.pallas_call(
        functools.partial(_dispatch_kernel, tm=tm, pad_bits=(tile_rows - 1).bit_length(),
                          tile_rows=tile_rows, n_tiles=n_tiles),
        grid_spec=grid_spec,
        out_shape=jax.ShapeDtypeStruct((n_tiles * tile_rows * ROW_TILE, LANES), F32),
        compiler_params=_cparams(("arbitrary",)),
        name="moe_dispatch",
    )(pad_lo, pad_n, n_used, dest3, h_rt)


def _combine_kernel(dest_ref, next_ref, ys_ref, x_ref, gate_ref, o_ref, buf_ref, sem, *, tm):
    i = pl.program_id(0)
    cur = lax.rem(i, 2)

    def copy(idx_ref, buf_slot, j):
        return _row_copy(ys_ref, idx_ref[0, 0, j], buf_ref.at[buf_slot], j, sem.at[buf_slot])

    @pl.when(i == 0)
    def _():
        _for_each_row(2 * tm, lambda j, lane: copy(dest_ref, 0, j).start(priority=lane))

    @pl.when(i + 1 < pl.num_programs(0))
    def _():
        _for_each_row(2 * tm, lambda j, lane: copy(next_ref, 1 - cur, j).start(priority=lane))

    _for_each_row(2 * tm, lambda j, lane: copy(dest_ref, cur, j).wait())
    g1 = gate_ref[:, 0:1]
    g2 = gate_ref[:, 1:2]
    d = x_ref.shape[1]
    for slot in range(2):
        @pl.when(cur == slot)
        def _(slot=slot):
            rows = buf_ref.at[slot]
            y1 = _from_row_tiles(rows, 0, tm, d)
            y2 = _from_row_tiles(rows, tm * ROW_TILE, tm, d)
            o_ref[...] = x_ref[...] + (g1 * y1 + g2 * y2)


def _combine(dest3, ys, x2d, gates, tm):
    t, d = x2d.shape
    n = t // tm
    return pl.pallas_call(
        functools.partial(_combine_kernel, tm=tm),
        grid=(n,),
        in_specs=[
            pl.BlockSpec((1, 1, 2 * tm), lambda i: (i, 0, 0), memory_space=pltpu.SMEM),
            pl.BlockSpec((1, 1, 2 * tm), lambda i: (jnp.minimum(i + 1, n - 1), 0, 0),
                         memory_space=pltpu.SMEM),
            pl.BlockSpec(memory_space=pl.ANY),
            pl.BlockSpec((tm, d), lambda i: (i, 0)),
            pl.BlockSpec((tm, gates.shape[1]), lambda i: (i, 0)),
        ],
        out_specs=pl.BlockSpec((tm, d), lambda i: (i, 0)),
        out_shape=jax.ShapeDtypeStruct((t, d), F32),
        scratch_shapes=[pltpu.VMEM((2, 2 * tm * ROW_TILE, LANES), F32),
                        pltpu.SemaphoreType.DMA((2,))],
        compiler_params=_cparams(("arbitrary",)),
        name="moe_combine",
    )(dest3, dest3, ys, x2d, gates)


def _pad_heads_t(w_cols, offsets):
    d = w_cols.shape[0]
    nh = w_cols.shape[1] // HEAD_DIM
    wt = w_cols.T.reshape(nh, HEAD_DIM, d)
    hi = jnp.asarray(offsets, I32).reshape(nh, 1, 1) > 0
    z = jnp.zeros_like(wt)
    slab = jnp.concatenate([jnp.where(hi, z, wt), jnp.where(hi, wt, z)], axis=1)
    return slab.reshape(nh * HEAD_SLAB, d)


def _pad_head_cols(w):
    rows = w.shape[0]
    nh = w.shape[1] // HEAD_DIM
    w3 = w.reshape(rows, nh, HEAD_DIM)
    return jnp.concatenate([w3, jnp.zeros_like(w3)], axis=2).reshape(rows, nh * HEAD_SLAB)


def _head_block_ones(width):
    idx = jnp.arange(width) // HEAD_DIM
    return (idx[:, None] == idx[None, :]).astype(F32)


def _tile_gain(g, n_heads):
    return jnp.tile(g.astype(F32), n_heads).reshape(1, n_heads * HEAD_DIM)


def _s5_params(lam_re, lam_im, log_dt, b_re, b_im, c_re, c_im):
    f32 = F32
    g, p = lam_re.shape
    n = b_re.shape[2]
    lam = lax.complex(lam_re.astype(f32), lam_im.astype(f32))
    dt = jnp.exp(log_dt.astype(f32))[:, None]
    lam_bar = jnp.exp(lam * dt)
    b_bar = ((lam_bar - 1.0) / lam)[..., None] * lax.complex(b_re.astype(f32), b_im.astype(f32))
    ns = g // S5_GROUPS_PER_SLAB
    eye = jnp.eye(S5_GROUPS_PER_SLAB, dtype=f32)

    def in_map(part):
        blk = part.reshape(ns, S5_GROUPS_PER_SLAB, p, n).transpose(0, 1, 3, 2)
        return jnp.einsum('sgnp,gh->sgnhp', blk, eye).reshape(
            ns, S5_GROUPS_PER_SLAB * n, S5_GROUPS_PER_SLAB * p)

    def out_map(part):
        blk = part.reshape(ns, S5_GROUPS_PER_SLAB, n, p).transpose(0, 1, 3, 2)
        return jnp.einsum('sgpn,gh->sgphn', blk, eye).reshape(
            ns, S5_GROUPS_PER_SLAB * p, S5_GROUPS_PER_SLAB * n)

    bd = jnp.concatenate([in_map(jnp.real(b_bar)), in_map(jnp.imag(b_bar))], axis=2)
    cd = jnp.concatenate([out_map(c_re.astype(f32)), out_map(-c_im.astype(f32))], axis=1)
    ar = jnp.real(lam_bar).reshape(ns, S5_GROUPS_PER_SLAB * p)
    ai = jnp.imag(lam_bar).reshape(ns, S5_GROUPS_PER_SLAB * p)
    return bd.astype(BF16), cd.astype(BF16), ar, ai


def kernel(x, mem, l0_mix_norm, l0_w_in, l0_s5_lam_re, l0_s5_lam_im, l0_s5_log_dt, l0_s5_b_re, l0_s5_b_im, l0_s5_c_re, l0_s5_c_im, l0_s5_d, l0_s5_w_glu, l0_s5_b_glu, l0_mem_norm, l0_w_mem_k, l0_w_mem_v, l0_mem_q_gain, l0_mem_k_gain, l0_w_out, l0_ffn_norm, l0_ffn_w_gate, l0_ffn_w_up, l0_ffn_w_down, kv_norm, kv_w_k, kv_w_v, kv_k_gain, l1_mix_norm, l1_w_in, l1_moba_q_gain, l1_mem_norm, l1_w_mem_k, l1_w_mem_v, l1_mem_q_gain, l1_mem_k_gain, l1_w_out, l1_ffn_norm, l1_moe_router, l1_moe_w_gate, l1_moe_w_up, l1_moe_w_down):
    batch, seq, d = x.shape
    t = batch * seq
    main_w = l0_s5_w_glu.shape[0]
    mem_w = l0_w_mem_k.shape[1]
    n_q = main_w // HEAD_DIM
    tm = 512
    row = lambda v: v.astype(F32).reshape(1, -1)

    mem_off = [HEAD_DIM * (h % 2) for h in range(MEM_HEADS)]
    ones_kv = _head_block_ones(mem_w)
    x2d = x.reshape(t, d)

    main_in, qm0_t = _inproj0(
        x2d, row(l0_mix_norm), l0_w_in[:, :main_w].astype(BF16),
        _pad_heads_t(l0_w_in[:, main_w:], mem_off).astype(BF16), batch, seq, tm)
    k0, v0_t = _mem_kv(
        mem, row(l0_mem_norm), l0_w_mem_k.astype(BF16), l0_w_mem_v.T.astype(BF16), ones_kv,
        _tile_gain(l0_mem_k_gain, MEM_HEADS) * _tile_gain(l0_mem_q_gain, MEM_HEADS))

    bd, cd, ar, ai = _s5_params(l0_s5_lam_re, l0_s5_lam_im, l0_s5_log_dt, l0_s5_b_re, l0_s5_b_im,
                                l0_s5_c_re, l0_s5_c_im)
    s5_out = _s5(main_in.reshape(batch, seq, main_w), bd, cd, ar, ai, row(l0_s5_d),
                 l0_s5_w_glu.astype(BF16), row(l0_s5_b_glu), 64).reshape(t, main_w)

    x1, h1 = _outproj0(s5_out, (qm0_t, k0, v0_t), l0_w_out[:main_w].astype(BF16),
                       l0_w_out[main_w:].astype(BF16), x2d, row(l0_ffn_norm), seq, tm)
    x2 = _ffn(h1, x1, l0_ffn_w_gate.astype(BF16), l0_ffn_w_up.astype(BF16),
              l0_ffn_w_down.astype(BF16), tm)

    wq1_t = jnp.concatenate([l1_w_in[:, :main_w].T,
                             _pad_heads_t(l1_w_in[:, main_w:], mem_off)], axis=0).astype(BF16)
    q_t, qm1_t, kq, km, v_t = _proj1(
        x2, row(l1_mix_norm), row(kv_norm), wq1_t, _pad_head_cols(kv_w_k).astype(BF16),
        kv_w_v.T.astype(BF16), _pad_head_cols(_tile_gain(kv_k_gain, MOBA_KV_HEADS)),
        _pad_head_cols(_tile_gain(l1_moba_q_gain, MOBA_KV_HEADS)), batch, seq, tm)
    nb = seq // MOBA_BLOCK
    moba_t = _moba(q_t, kq.reshape(batch, seq, -1), v_t, km.reshape(batch, nb, -1))
    k1, v1_t = _mem_kv(
        mem, row(l1_mem_norm), l1_w_mem_k.astype(BF16), l1_w_mem_v.T.astype(BF16), ones_kv,
        _tile_gain(l1_mem_k_gain, MEM_HEADS) * _tile_gain(l1_mem_q_gain, MEM_HEADS))

    wr_t = jnp.zeros((ROUTER_ROWS, d), F32).at[:N_EXPERTS].set(l1_moe_router.astype(F32).T)
    wr_hi = wr_t.astype(BF16)
    wr_lo = (wr_t - wr_hi.astype(F32)).astype(BF16)
    x3, h3, rec, cnt = _outproj1(moba_t, (qm1_t, k1, v1_t), l1_w_out[:main_w].astype(BF16),
                                 l1_w_out[main_w:].astype(BF16), x2, row(l1_ffn_norm),
                                 wr_hi, wr_lo, seq, tm)

    assert d == ROW_TILE * LANES
    tme = 1024
    counts = cnt[:N_EXPERTS, 0].astype(I32)
    padded = ((counts + tme - 1) // tme) * tme
    ends = jnp.cumsum(padded)
    starts = ends - padded
    max_tiles = (2 * t) // tme + N_EXPERTS
    def lookup(table, idx):
        hit = idx[None, :] == jnp.arange(N_EXPERTS, dtype=I32)[:, None]
        return jnp.sum(jnp.where(hit, table[:, None], 0), axis=0)

    tile_ids = jnp.arange(max_tiles, dtype=I32)
    tile_expert = jnp.minimum(
        jnp.sum((ends // tme)[None, :] <= tile_ids[:, None], axis=1), N_EXPERTS - 1).astype(I32)
    n_valid = jnp.clip(lookup(counts, tile_expert) + lookup(starts, tile_expert)
                       - tile_ids * tme, 0, tme)
    n_used = (ends[-1] // tme).astype(I32).reshape(1)
    d1 = lookup(starts, rec[0].astype(I32)) + rec[2].astype(I32)
    d2 = lookup(starts, rec[1].astype(I32)) + rec[3].astype(I32)
    gates = rec[4:6].T
    def tile_dest(tile):
        return jnp.concatenate([d1.reshape(t // tile, 1, tile), d2.reshape(t // tile, 1, tile)],
                               axis=2)

    tmd, tmc = 1024, 512
    xs = _dispatch(starts + counts, padded - counts, n_used, tile_dest(tmd), h3, max_tiles, tme,
                   tmd)
    ys = _moe_ffn(tile_expert, n_valid.astype(I32), n_used, xs, l1_moe_w_gate.astype(BF16),
                  l1_moe_w_up.astype(BF16), l1_moe_w_down.astype(BF16), tme, 512)
    out = _combine(tile_dest(tmc), ys, x3, gates, tmc)
    return out.reshape(batch, seq, d)
```

```python
import functools

import jax
import jax.numpy as jnp
from jax import lax
from jax.experimental import pallas as pl
from jax.experimental.pallas import tpu as pltpu

F32 = jnp.float32
BF16 = jnp.bfloat16
I32 = jnp.int32

RMS_EPS = 1e-6
NEG_INF = -1e30
HEAD_DIM = 64
HEAD_SLAB = 128
MEM_HEADS = 4
MOBA_KV_HEADS = 4
MOBA_GROUP = 3
MOBA_BLOCK = 256
MOBA_TOP_BLOCKS = 3
MOBA_KV_PER_STEP = 2
S5_GROUP_DIM = 16
S5_STATE = 64
S5_GROUPS_PER_SLAB = HEAD_SLAB // S5_GROUP_DIM
S5_SLAB_STATES = S5_GROUPS_PER_SLAB * S5_STATE
N_EXPERTS = 8
ROUTER_LANES = 128
ROUTER_ROWS = 16
SUBLANES = 8
LANES = 128
ROW_TILE = SUBLANES
DMA_UNROLL = 8
LOG2_E = 1.4426950408889634
V_ROWS = HEAD_DIM + 16

VMEM_LIMIT_BYTES = 56 * 1024 * 1024

_NT = (((1,), (1,)), ((), ()))
_TN = (((0,), (0,)), ((), ()))


def _cparams(sem):
    return pltpu.CompilerParams(dimension_semantics=sem, vmem_limit_bytes=VMEM_LIMIT_BYTES)


def _rms(x, g):
    return x * lax.rsqrt(jnp.mean(x * x, axis=-1, keepdims=True) + RMS_EPS) * g


def _store_vt_with_ones(vt_ref, vt, n_heads):
    n = vt.shape[1]
    ones_row = jnp.where(lax.broadcasted_iota(I32, (V_ROWS - HEAD_DIM, n), 0) == 0, 1.0, 0.0)
    for h in range(n_heads):
        vt_ref[0, V_ROWS * h:V_ROWS * h + HEAD_DIM, :] = (
            vt[HEAD_DIM * h:HEAD_DIM * (h + 1), :].astype(BF16))
        vt_ref[0, V_ROWS * h + HEAD_DIM:V_ROWS * (h + 1), :] = ones_row.astype(BF16)


def _slab_norm_t(qt, n_heads, scale):
    outs = []
    for h in range(n_heads):
        s = qt[HEAD_SLAB * h:HEAD_SLAB * (h + 1), :]
        ms = jnp.sum(s * s, axis=0, keepdims=True) * (1.0 / HEAD_DIM)
        outs.append(s * (lax.rsqrt(ms + RMS_EPS) * scale))
    return outs


def _inproj0_kernel(x_ref, g_ref, wm_ref, wqt_ref, main_ref, qt_ref):
    hb = _rms(x_ref[...], g_ref[...]).astype(BF16)
    main_ref[...] = jnp.dot(hb, wm_ref[...], preferred_element_type=F32)
    qt = lax.dot_general(wqt_ref[...], hb, _NT, preferred_element_type=F32)
    for h, s in enumerate(_slab_norm_t(qt, MEM_HEADS, HEAD_DIM ** -0.5 * LOG2_E)):
        qt_ref[0, HEAD_SLAB * h:HEAD_SLAB * (h + 1), :] = s.astype(BF16)


def _inproj0(x2d, g, wm, wqt, batch, seq, tm):
    t, d = x2d.shape
    per_b = seq // tm
    nm = wm.shape[1]
    nq = wqt.shape[0]
    return pl.pallas_call(
        _inproj0_kernel,
        grid=(t // tm,),
        in_specs=[
            pl.BlockSpec((tm, d), lambda i: (i, 0)),
            pl.BlockSpec((1, d), lambda i: (0, 0)),
            pl.BlockSpec((d, nm), lambda i: (0, 0)),
            pl.BlockSpec((nq, d), lambda i: (0, 0)),
        ],
        out_specs=[
            pl.BlockSpec((tm, nm), lambda i: (i, 0)),
            pl.BlockSpec((1, nq, tm), lambda i: (i // per_b, 0, i % per_b)),
        ],
        out_shape=[
            jax.ShapeDtypeStruct((t, nm), F32),
            jax.ShapeDtypeStruct((batch, nq, seq), BF16),
        ],
        compiler_params=_cparams(("parallel",)),
        name="inproj0",
    )(x2d, g, wm, wqt)


def _mem_kv_kernel(mem_ref, g_ref, wk_ref, wvt_ref, ones_ref, kg_ref, k_ref, vt_ref):
    mb = _rms(mem_ref[0], g_ref[...]).astype(BF16)
    k = jnp.dot(mb, wk_ref[...], preferred_element_type=F32)
    ms = jnp.dot(k * k, ones_ref[...], preferred_element_type=F32,
                 precision=lax.Precision.HIGHEST) * (1.0 / HEAD_DIM)
    k_ref[0] = (k * lax.rsqrt(ms + RMS_EPS) * kg_ref[...]).astype(BF16)
    _store_vt_with_ones(vt_ref, lax.dot_general(wvt_ref[...], mb, _NT,
                                                preferred_element_type=F32), MEM_HEADS)


def _mem_kv(mem, g, wk, wvt, ones, kg):
    b, m, d = mem.shape
    w = wk.shape[1]
    v_rows = MEM_HEADS * V_ROWS
    return pl.pallas_call(
        _mem_kv_kernel,
        grid=(b,),
        in_specs=[
            pl.BlockSpec((1, m, d), lambda i: (i, 0, 0)),
            pl.BlockSpec((1, d), lambda i: (0, 0)),
            pl.BlockSpec((d, w), lambda i: (0, 0)),
            pl.BlockSpec((w, d), lambda i: (0, 0)),
            pl.BlockSpec((w, w), lambda i: (0, 0)),
            pl.BlockSpec((1, w), lambda i: (0, 0)),
        ],
        out_specs=[
            pl.BlockSpec((1, m, w), lambda i: (i, 0, 0)),
            pl.BlockSpec((1, v_rows, m), lambda i: (i, 0, 0)),
        ],
        out_shape=[
            jax.ShapeDtypeStruct((b, m, w), BF16),
            jax.ShapeDtypeStruct((b, v_rows, m), BF16),
        ],
        compiler_params=_cparams(("parallel",)),
        name="mem_kv",
    )(mem, g, wk, wvt, ones, kg)


def _mem_scores_t(qt_ref, k_ref):
    scores = []
    for h in range(MEM_HEADS):
        pair = h // 2
        k2 = k_ref[0, :, HEAD_SLAB * pair:HEAD_SLAB * (pair + 1)]
        q = qt_ref[0, HEAD_SLAB * h:HEAD_SLAB * (h + 1), :]
        scores.append(jnp.dot(k2, q, preferred_element_type=F32))
    return scores


def _mem_attend_t(scores, vt_ref):
    outs = []
    for h, s in enumerate(scores):
        p = jnp.exp2(s - jnp.max(s, axis=0, keepdims=True)).astype(BF16)
        v = vt_ref[0, V_ROWS * h:V_ROWS * (h + 1), :]
        acc = jnp.dot(v, p, preferred_element_type=F32)
        outs.append((acc[0:HEAD_DIM, :] / acc[HEAD_DIM:HEAD_DIM + 1, :]).astype(BF16))
    return jnp.concatenate(outs, axis=0)


def _mem_attn_specs(qt, k, vt, per_b, tm):
    return [
        pl.BlockSpec((1, qt.shape[1], tm), lambda i: (i // per_b, 0, i % per_b)),
        pl.BlockSpec((1,) + k.shape[1:], lambda i: (i // per_b, 0, 0)),
        pl.BlockSpec((1,) + vt.shape[1:], lambda i: (i // per_b, 0, 0)),
    ]


def _s5_kernel(u_ref, bd_ref, cd_ref, ar_ref, ai_ref, d_ref, wglu_ref, bglu_ref, o_ref,
               buf_ref, st_ref, ut_ref, ot_ref, *, ts, n_slabs):
    half = S5_SLAB_STATES
    width = 2 * half
    batch = u_ref.shape[0]

    @pl.when(pl.program_id(0) == 0)
    def _():
        st_ref[...] = jnp.zeros_like(st_ref)

    for b in range(batch):
        for j in range(n_slabs):
            ut_ref[j, pl.ds(b, ts, stride=batch), :] = u_ref[b, :, HEAD_SLAB * j:HEAD_SLAB * (j + 1)]

    for j in range(n_slabs):
        buf_ref[:, width * j:width * (j + 1)] = jnp.dot(
            ut_ref[j].astype(BF16), bd_ref[j], preferred_element_type=F32)

    for j in range(n_slabs):
        re = slice(width * j, width * j + half)
        im = slice(width * j + half, width * (j + 1))
        ar = jnp.broadcast_to(ar_ref[j:j + 1, :], (SUBLANES, half))
        ai = jnp.broadcast_to(ai_ref[j:j + 1, :], (SUBLANES, half))

        def step(t, carry, re=re, im=im, ar=ar, ai=ai):
            xr, xi = carry
            rows = pl.ds(pl.multiple_of(t * SUBLANES, SUBLANES), SUBLANES)
            nxr = ar * xr - ai * xi + buf_ref[rows, re]
            nxi = ar * xi + ai * xr + buf_ref[rows, im]
            buf_ref[rows, re] = nxr
            buf_ref[rows, im] = nxi
            return nxr, nxi

        xr, xi = lax.fori_loop(0, ts, step, (st_ref[:, re], st_ref[:, im]), unroll=True)
        st_ref[:, re] = xr
        st_ref[:, im] = xi

    ys = []
    for j in range(n_slabs):
        cols = slice(HEAD_SLAB * j, HEAD_SLAB * (j + 1))
        ys.append(jnp.dot(buf_ref[:, width * j:width * (j + 1)].astype(BF16), cd_ref[j],
                          preferred_element_type=F32) + d_ref[:, cols] * ut_ref[j])
    g = jax.nn.gelu(jnp.concatenate(ys, axis=1))
    z = jnp.dot(g.astype(BF16), wglu_ref[...], preferred_element_type=F32) + bglu_ref[...]
    out = g / (1.0 + jnp.exp(-z))
    for j in range(n_slabs):
        ot_ref[j] = out[:, HEAD_SLAB * j:HEAD_SLAB * (j + 1)]
    for b in range(batch):
        for j in range(n_slabs):
            o_ref[b, :, HEAD_SLAB * j:HEAD_SLAB * (j + 1)] = (
                ot_ref[j, pl.ds(b, ts, stride=batch), :].astype(BF16))


def _s5(u, bd, cd, ar, ai, dskip, wglu, bglu, ts):
    batch, seq, c = u.shape
    n_slabs = c // HEAD_SLAB
    width = 2 * S5_SLAB_STATES
    tile = ts * batch
    return pl.pallas_call(
        functools.partial(_s5_kernel, ts=ts, n_slabs=n_slabs),
        grid=(seq // ts,),
        in_specs=[
            pl.BlockSpec((batch, ts, c), lambda i: (0, i, 0)),
            pl.BlockSpec((n_slabs, HEAD_SLAB, width), lambda i: (0, 0, 0)),
            pl.BlockSpec((n_slabs, width, HEAD_SLAB), lambda i: (0, 0, 0)),
            pl.BlockSpec((n_slabs, S5_SLAB_STATES), lambda i: (0, 0)),
            pl.BlockSpec((n_slabs, S5_SLAB_STATES), lambda i: (0, 0)),
            pl.BlockSpec((1, c), lambda i: (0, 0)),
            pl.BlockSpec((c, c), lambda i: (0, 0)),
            pl.BlockSpec((1, c), lambda i: (0, 0)),
        ],
        out_specs=pl.BlockSpec((batch, ts, c), lambda i: (0, i, 0)),
        out_shape=jax.ShapeDtypeStruct((batch, seq, c), BF16),
        scratch_shapes=[
            pltpu.VMEM((tile, n_slabs * width), F32),
            pltpu.VMEM((batch, n_slabs * width), F32),
            pltpu.VMEM((n_slabs, tile, HEAD_SLAB), F32),
            pltpu.VMEM((n_slabs, tile, HEAD_SLAB), F32),
        ],
        compiler_params=_cparams(("arbitrary",)),
        name="s5_mixer",
    )(u, bd, cd, ar, ai, dskip, wglu, bglu)


def _outproj_body(a, a_transposed, mem_refs, wa_ref, wb_ref, x_ref, g_ref):
    qt_ref, mk_ref, mvt_ref = mem_refs
    dn = _TN if a_transposed else (((1,), (0,)), ((), ()))
    scores = _mem_scores_t(qt_ref, mk_ref)
    y = lax.dot_general(a, wa_ref[...], dn, preferred_element_type=F32)
    y = y + lax.dot_general(_mem_attend_t(scores, mvt_ref), wb_ref[...], _TN,
                            preferred_element_type=F32)
    xn = x_ref[...] + y
    return xn, _rms(xn, g_ref[...])


def _outproj0_ffn_kernel(a_ref, qt_ref, mk_ref, mvt_ref, wa_ref, wb_ref, x_ref, g_ref,
                         wg_ref, wu_ref, wd_ref, o_ref, h_ref):
    xn, h = _outproj_body(a_ref[...], False, (qt_ref, mk_ref, mvt_ref), wa_ref, wb_ref, x_ref,
                          g_ref)
    h_ref[...] = h.astype(BF16)
    half = h_ref.shape[0] // 2
    halves = (slice(0, half), slice(half, 2 * half))
    mids = []
    for rows in halves:
        hb = h_ref[rows, :]
        gt = jnp.dot(hb, wg_ref[...], preferred_element_type=F32)
        up = jnp.dot(hb, wu_ref[...], preferred_element_type=F32)
        mids.append(((gt / (1.0 + jnp.exp(-gt))) * up).astype(BF16))
    for rows, mid in zip(halves, mids):
        o_ref[rows, :] = xn[rows, :] + jnp.dot(mid, wd_ref[...], preferred_element_type=F32)


def _outproj0_ffn(a, mem_qkv, wa, wb, x2d, g, wg, wu, wd, seq, tm):
    t, d = x2d.shape
    per_b = seq // tm
    ka, kb = a.shape[1], wb.shape[0]
    ff = wg.shape[1]
    resident = dict(pipeline_mode=pl.Buffered(1))
    return pl.pallas_call(
        _outproj0_ffn_kernel,
        grid=(t // tm,),
        in_specs=[
            pl.BlockSpec((tm, ka), lambda i: (i, 0)),
            *_mem_attn_specs(*mem_qkv, per_b, tm),
            pl.BlockSpec((ka, d), lambda i: (0, 0), **resident),
            pl.BlockSpec((kb, d), lambda i: (0, 0), **resident),
            pl.BlockSpec((tm, d), lambda i: (i, 0)),
            pl.BlockSpec((1, d), lambda i: (0, 0)),
            pl.BlockSpec((d, ff), lambda i: (0, 0), **resident),
            pl.BlockSpec((d, ff), lambda i: (0, 0), **resident),
            pl.BlockSpec((ff, d), lambda i: (0, 0), **resident),
        ],
        out_specs=pl.BlockSpec((tm, d), lambda i: (i, 0)),
        out_shape=jax.ShapeDtypeStruct((t, d), F32),
        scratch_shapes=[pltpu.VMEM((tm, d), BF16)],
        compiler_params=_cparams(("parallel",)),
        name="outproj0_ffn",
    )(a, *mem_qkv, wa, wb, x2d, g, wg, wu, wd)


def _outproj1_kernel(at_ref, qt_ref, mk_ref, mvt_ref, wa_ref, wb_ref, x_ref, g_ref, wrh_ref,
                     wrl_ref, xo_ref, h_ref, rec_ref, cnt_ref, run_ref, *, tm):
    i = pl.program_id(0)

    @pl.when(i == 0)
    def _():
        run_ref[...] = jnp.zeros_like(run_ref)

    xn, h = _outproj_body(at_ref[0], True, (qt_ref, mk_ref, mvt_ref), wa_ref, wb_ref, x_ref,
                          g_ref)
    xo_ref[...] = xn
    _to_row_tiles(h_ref, h)

    h_hi = h.astype(BF16)
    h_lo = (h - h_hi.astype(F32)).astype(BF16)
    logits = (lax.dot_general(wrh_ref[...], h_hi, _NT, preferred_element_type=F32)
              + lax.dot_general(wrh_ref[...], h_lo, _NT, preferred_element_type=F32)
              + lax.dot_general(wrl_ref[...], h_hi, _NT, preferred_element_type=F32))
    ex = lax.broadcasted_iota(I32, (ROUTER_ROWS, tm), 0)
    logits = jnp.where(ex < N_EXPERTS, logits, -jnp.inf)
    m1 = jnp.max(logits, axis=0, keepdims=True)
    e1 = jnp.min(jnp.where(logits == m1, ex, ROUTER_ROWS), axis=0, keepdims=True)
    rest = jnp.where(ex == e1, -jnp.inf, logits)
    m2 = jnp.max(rest, axis=0, keepdims=True)
    e2 = jnp.min(jnp.where(rest == m2, ex, ROUTER_ROWS), axis=0, keepdims=True)
    w2 = jnp.exp(m2 - m1)
    g1 = 1.0 / (1.0 + w2)
    g2 = w2 / (1.0 + w2)

    hot = jnp.where((ex == e1) | (ex == e2), 1.0, 0.0)
    r_io = lax.broadcasted_iota(I32, (tm, tm), 0)
    c_io = lax.broadcasted_iota(I32, (tm, tm), 1)
    triu = jnp.where(r_io < c_io, 1.0, 0.0).astype(BF16)
    before = (jnp.dot(hot.astype(BF16), triu, preferred_element_type=F32)
              + run_ref[:, 0:1])
    r1 = jnp.sum(jnp.where(ex == e1, before, 0.0), axis=0, keepdims=True)
    r2 = jnp.sum(jnp.where(ex == e2, before, 0.0), axis=0, keepdims=True)
    run = run_ref[...] + jnp.sum(hot, axis=1, keepdims=True)
    run_ref[...] = run
    cnt_ref[...] = run

    slot = lax.broadcasted_iota(I32, (SUBLANES, tm), 0)
    rec = jnp.where(slot == 0, e1.astype(F32), 0.0)
    rec = jnp.where(slot == 1, e2.astype(F32), rec)
    rec = jnp.where(slot == 2, r1, rec)
    rec = jnp.where(slot == 3, r2, rec)
    rec = jnp.where(slot == 4, g1, rec)
    rec = jnp.where(slot == 5, g2, rec)
    rec_ref[...] = rec


def _outproj1(at, mem_qkv, wa, wb, x2d, g, wr_hi, wr_lo, seq, tm):
    t, d = x2d.shape
    per_b = seq // tm
    ka, kb = at.shape[1], wb.shape[0]
    return pl.pallas_call(
        functools.partial(_outproj1_kernel, tm=tm),
        grid=(t // tm,),
        in_specs=[
            pl.BlockSpec((1, ka, tm), lambda i: (i // per_b, 0, i % per_b)),
            *_mem_attn_specs(*mem_qkv, per_b, tm),
            pl.BlockSpec((ka, d), lambda i: (0, 0)),
            pl.BlockSpec((kb, d), lambda i: (0, 0)),
            pl.BlockSpec((tm, d), lambda i: (i, 0)),
            pl.BlockSpec((1, d), lambda i: (0, 0)),
            pl.BlockSpec((ROUTER_ROWS, d), lambda i: (0, 0)),
            pl.BlockSpec((ROUTER_ROWS, d), lambda i: (0, 0)),
        ],
        out_specs=[
            pl.BlockSpec((tm, d), lambda i: (i, 0)),
            pl.BlockSpec((tm * ROW_TILE, LANES), lambda i: (i, 0)),
            pl.BlockSpec((SUBLANES, tm), lambda i: (0, i)),
            pl.BlockSpec((ROUTER_ROWS, ROUTER_LANES), lambda i: (0, 0)),
        ],
        out_shape=[
            jax.ShapeDtypeStruct((t, d), F32),
            jax.ShapeDtypeStruct((t * ROW_TILE, LANES), F32),
            jax.ShapeDtypeStruct((SUBLANES, t), F32),
            jax.ShapeDtypeStruct((ROUTER_ROWS, ROUTER_LANES), F32),
        ],
        scratch_shapes=[pltpu.VMEM((ROUTER_ROWS, ROUTER_LANES), F32)],
        compiler_params=_cparams(("arbitrary",)),
        name="outproj1_router",
    )(at, *mem_qkv, wa, wb, x2d, g, wr_hi, wr_lo)


def _from_row_tiles(ref, base, n_rows, n_feat):
    parts = [ref[pl.ds(base + k, n_rows, stride=ROW_TILE), :] for k in range(n_feat // LANES)]
    return jnp.concatenate(parts, axis=1)


def _to_row_tiles(ref, value):
    for k in range(value.shape[1] // LANES):
        ref[pl.ds(k, value.shape[0], stride=ROW_TILE), :] = value[:, LANES * k:LANES * (k + 1)]


def _moe_ffn_kernel(te_ref, nv_ref, nu_ref, xs_ref, wg_ref, wu_ref, wd_ref, o_ref,
                    acc_ref, xb_ref, *, tm):
    del te_ref
    i = pl.program_id(0)
    f = pl.program_id(1)
    nf = pl.num_programs(1)
    half = tm // 2
    d = xb_ref.shape[1]

    def partial_out(row_sets):
        wg = wg_ref[0].astype(BF16)
        wu = wu_ref[0].astype(BF16)
        wd = wd_ref[0].astype(BF16)
        mids = []
        for rows in row_sets:
            xb = xb_ref[rows, :]
            gt = jnp.dot(xb, wg, preferred_element_type=F32)
            up = jnp.dot(xb, wu, preferred_element_type=F32)
            mids.append(((gt / (1.0 + jnp.exp(-gt))) * up).astype(BF16))
        for rows, mid in zip(row_sets, mids):
            acc_ref[rows, :] += jnp.dot(mid, wd, preferred_element_type=F32)

    @pl.when(i < nu_ref[0])
    def _():
        @pl.when(f == 0)
        def _():
            acc_ref[...] = jnp.zeros_like(acc_ref)
            xb_ref[...] = _from_row_tiles(xs_ref, 0, tm, d).astype(BF16)

        @pl.when(nv_ref[i] > half)
        def _():
            partial_out([slice(0, half), slice(half, tm)])

        @pl.when(nv_ref[i] <= half)
        def _():
            partial_out([slice(0, half)])

        @pl.when(f == nf - 1)
        def _():
            _to_row_tiles(o_ref, acc_ref[...])

    @pl.when((i >= nu_ref[0]) & (f == nf - 1))
    def _():
        o_ref[...] = jnp.zeros_like(o_ref)


def _moe_ffn(tile_expert, n_valid, n_used, xs_rt, wg, wu, wd, tm, tf):
    d = wg.shape[1]
    ff = wg.shape[2]
    nf = ff // tf
    n_tiles = xs_rt.shape[0] // (tm * ROW_TILE)

    def live(i, nu):
        return jnp.minimum(i, jnp.maximum(nu[0] - 1, 0))

    def f_of(i, f, nu):
        return jnp.where(i < nu[0], f, nf - 1)

    def wcol_map(i, f, te, nv, nu):
        return (te[live(i, nu)], 0, f_of(i, f, nu))

    def wrow_map(i, f, te, nv, nu):
        return (te[live(i, nu)], f_of(i, f, nu), 0)

    grid_spec = pltpu.PrefetchScalarGridSpec(
        num_scalar_prefetch=3,
        grid=(n_tiles, nf),
        in_specs=[
            pl.BlockSpec((tm * ROW_TILE, LANES), lambda i, f, te, nv, nu: (live(i, nu), 0)),
            pl.BlockSpec((1, d, tf), wcol_map),
            pl.BlockSpec((1, d, tf), wcol_map),
            pl.BlockSpec((1, tf, d), wrow_map),
        ],
        out_specs=pl.BlockSpec((tm * ROW_TILE, LANES), lambda i, f, te, nv, nu: (i, 0)),
        scratch_shapes=[pltpu.VMEM((tm, d), F32), pltpu.VMEM((tm, d), BF16)],
    )
    return pl.pallas_call(
        functools.partial(_moe_ffn_kernel, tm=tm),
        grid_spec=grid_spec,
        out_shape=jax.ShapeDtypeStruct(xs_rt.shape, F32),
        compiler_params=_cparams(("arbitrary", "arbitrary")),
        name="ffn_moe",
    )(tile_expert, n_valid, n_used, xs_rt, wg, wu, wd)


def _proj1_kernel(x_ref, g1_ref, g2_ref, wqt_ref, wk_ref, wvt_ref, kg_ref, qg_ref,
                  qt_ref, qmt_ref, k_ref, km_ref, vt_ref, *, tm, n_q, per_b):
    x = x_ref[...]
    h1 = _rms(x, g1_ref[...]).astype(BF16)
    h2 = _rms(x, g2_ref[...]).astype(BF16)
    qt = lax.dot_general(wqt_ref[...], h1, _NT, preferred_element_type=F32)

    for h in range(n_q):
        s = qt[HEAD_DIM * h:HEAD_DIM * (h + 1), :]
        ms = jnp.mean(s * s, axis=0, keepdims=True)
        qt_ref[0, HEAD_DIM * h:HEAD_DIM * (h + 1), :] = (
            s * (lax.rsqrt(ms + RMS_EPS) * (HEAD_DIM ** -0.5 * LOG2_E))).astype(BF16)
    qm = qt[HEAD_DIM * n_q:, :]
    for h, s in enumerate(_slab_norm_t(qm, MEM_HEADS, HEAD_DIM ** -0.5 * LOG2_E)):
        qmt_ref[0, HEAD_SLAB * h:HEAD_SLAB * (h + 1), :] = s.astype(BF16)

    k = jnp.dot(h2, wk_ref[...], preferred_element_type=F32)
    nblk = tm // MOBA_BLOCK
    first_blk = lax.rem(pl.program_id(0), per_b) * nblk
    lane = lax.broadcasted_iota(I32, (MOBA_BLOCK, HEAD_SLAB), 1)
    for kv in range(MOBA_KV_HEADS):
        cols = slice(HEAD_SLAB * kv, HEAD_SLAB * (kv + 1))
        ks = k[:, cols]
        ms = jnp.sum(ks * ks, axis=-1, keepdims=True) * (1.0 / HEAD_DIM)
        kn = ks * lax.rsqrt(ms + RMS_EPS) * kg_ref[:, cols]
        kq = kn * qg_ref[:, cols]
        for j in range(nblk):
            rows = slice(MOBA_BLOCK * j, MOBA_BLOCK * (j + 1))
            km_ref[j, :, cols] = jnp.mean(kn[rows], axis=0, keepdims=True) * qg_ref[:, cols]
            k_ref[rows, cols] = jnp.where(lane == HEAD_DIM + first_blk + j, 1.0,
                                          kq[rows]).astype(BF16)

    _store_vt_with_ones(vt_ref, lax.dot_general(wvt_ref[...], h2, _NT,
                                                preferred_element_type=F32), MOBA_KV_HEADS)


def _proj1(x2d, g1, g2, wqt, wk, wvt, kg, qg, batch, seq, tm):
    t, d = x2d.shape
    per_b = seq // tm
    nq_rows = wqt.shape[0]
    n_q = (nq_rows - MEM_HEADS * HEAD_SLAB) // HEAD_DIM
    kw = wk.shape[1]
    vw = wvt.shape[0]
    v_rows = MOBA_KV_HEADS * V_ROWS
    nblk = tm // MOBA_BLOCK
    return pl.pallas_call(
        functools.partial(_proj1_kernel, tm=tm, n_q=n_q, per_b=per_b),
        grid=(t // tm,),
        in_specs=[
            pl.BlockSpec((tm, d), lambda i: (i, 0)),
            pl.BlockSpec((1, d), lambda i: (0, 0)),
            pl.BlockSpec((1, d), lambda i: (0, 0)),
            pl.BlockSpec((nq_rows, d), lambda i: (0, 0)),
            pl.BlockSpec((d, kw), lambda i: (0, 0)),
            pl.BlockSpec((vw, d), lambda i: (0, 0)),
            pl.BlockSpec((1, kw), lambda i: (0, 0)),
            pl.BlockSpec((1, kw), lambda i: (0, 0)),
        ],
        out_specs=[
            pl.BlockSpec((1, n_q * HEAD_DIM, tm), lambda i: (i // per_b, 0, i % per_b)),
            pl.BlockSpec((1, MEM_HEADS * HEAD_SLAB, tm), lambda i: (i // per_b, 0, i % per_b)),
            pl.BlockSpec((tm, kw), lambda i: (i, 0)),
            pl.BlockSpec((nblk, 1, kw), lambda i: (i, 0, 0)),
            pl.BlockSpec((1, v_rows, tm), lambda i: (i // per_b, 0, i % per_b)),
        ],
        out_shape=[
            jax.ShapeDtypeStruct((batch, n_q * HEAD_DIM, seq), BF16),
            jax.ShapeDtypeStruct((batch, MEM_HEADS * HEAD_SLAB, seq), BF16),
            jax.ShapeDtypeStruct((t, kw), BF16),
            jax.ShapeDtypeStruct((t // MOBA_BLOCK, 1, kw), F32),
            jax.ShapeDtypeStruct((batch, v_rows, seq), BF16),
        ],
        compiler_params=_cparams(("parallel",)),
        name="proj1",
    )(x2d, g1, g2, wqt, wk, wvt, kg, qg)


def _moba_kernel(qt_ref, k_ref, vt_ref, km_ref, o_ref, s_ref):
    qb = pl.program_id(2)
    nb = km_ref.shape[1]
    bq = MOBA_BLOCK
    nq = MOBA_GROUP * bq
    q_rows = MOBA_GROUP * HEAD_DIM
    blk = lax.broadcasted_iota(I32, (nb, nq), 0)

    def queries(kv):
        return jnp.concatenate(
            [qt_ref[0, q_rows * kv + HEAD_DIM * g:q_rows * kv + HEAD_DIM * (g + 1), :]
             for g in range(MOBA_GROUP)], axis=1)

    def pad_to_slab(q64, extra):
        pieces = [q64] if extra is None else [q64, extra]
        rows = sum(p.shape[0] for p in pieces)
        return jnp.concatenate(pieces + [jnp.zeros((HEAD_SLAB - rows, nq), BF16)], axis=0)

    def gated_queries(q64, kv):
        km = km_ref[0, :, HEAD_SLAB * kv:HEAD_SLAB * kv + HEAD_DIM]
        km_hi = km.astype(BF16)
        km_lo = (km - km_hi.astype(F32)).astype(BF16)
        gate = (jnp.dot(km_hi, q64, preferred_element_type=F32)
                + jnp.dot(km_lo, q64, preferred_element_type=F32))
        gate = jnp.where(blk < qb, gate, NEG_INF)
        rank = jnp.zeros((nb, nq), F32)
        for m in range(nb):
            gm = gate[m:m + 1, :]
            rank = rank + jnp.where(gm > gate, 1.0,
                                    jnp.where((gm == gate) & (blk > m), 1.0, 0.0))
        dropped = (blk < qb) & (rank >= MOBA_TOP_BLOCKS)
        bias = jnp.where(dropped, NEG_INF, 0.0)
        return pad_to_slab(
            q64, jnp.concatenate([bias, jnp.zeros((nb, nq), F32)], axis=0).astype(BF16))

    key_pos = lax.broadcasted_iota(I32, (bq, nq), 0)
    qry_pos = lax.broadcasted_iota(I32, (bq, nq), 1) & (bq - 1)
    causal = key_pos <= qry_pos

    for n_blocks in range(1, nb + 1):
        @pl.when(qb == n_blocks - 1)
        def _(n_blocks=n_blocks):
            def scores(kv, n, q):
                return jnp.dot(k_ref[0, bq * n:bq * (n + 1), HEAD_SLAB * kv:HEAD_SLAB * (kv + 1)],
                               q, preferred_element_type=F32)

            q64 = [queries(kv) for kv in range(MOBA_KV_PER_STEP)]
            mx = []
            for kv in range(MOBA_KV_PER_STEP):
                s = jnp.where(causal, scores(kv, n_blocks - 1, pad_to_slab(q64[kv], None)),
                              NEG_INF)
                s_ref[kv, n_blocks - 1] = s
                mx.append(jnp.max(s, axis=0, keepdims=True))
            for kv in range(MOBA_KV_PER_STEP if n_blocks > 1 else 0):
                q3 = gated_queries(q64[kv], kv)
                for n in range(n_blocks - 1):
                    s = scores(kv, n, q3)
                    s_ref[kv, n] = s
                    mx[kv] = jnp.maximum(mx[kv], jnp.max(s, axis=0, keepdims=True))
            for kv in range(MOBA_KV_PER_STEP):
                acc = jnp.zeros((V_ROWS, nq), F32)
                for n in range(n_blocks):
                    p = jnp.exp2(s_ref[kv, n] - mx[kv]).astype(BF16)
                    acc = acc + jnp.dot(vt_ref[0, V_ROWS * kv:V_ROWS * (kv + 1),
                                               bq * n:bq * (n + 1)], p,
                                        preferred_element_type=F32)
                out = acc[0:HEAD_DIM, :] / acc[HEAD_DIM:HEAD_DIM + 1, :]
                for g in range(MOBA_GROUP):
                    r0 = q_rows * kv + HEAD_DIM * g
                    o_ref[0, r0:r0 + HEAD_DIM, :] = out[:, bq * g:bq * (g + 1)].astype(BF16)


def _moba(qt, k3, vt, km3):
    b, nq_rows, s = qt.shape
    nb = s // MOBA_BLOCK
    per = MOBA_KV_PER_STEP
    rows = per * MOBA_GROUP * HEAD_DIM
    return pl.pallas_call(
        _moba_kernel,
        grid=(b, MOBA_KV_HEADS // per, nb),
        in_specs=[
            pl.BlockSpec((1, rows, MOBA_BLOCK), lambda i, j, q: (i, j, q)),
            pl.BlockSpec((1, s, per * HEAD_SLAB), lambda i, j, q: (i, 0, j)),
            pl.BlockSpec((1, per * V_ROWS, s), lambda i, j, q: (i, j, 0)),
            pl.BlockSpec((1, nb, per * HEAD_SLAB), lambda i, j, q: (i, 0, j)),
        ],
        out_specs=pl.BlockSpec((1, rows, MOBA_BLOCK), lambda i, j, q: (i, j, q)),
        out_shape=jax.ShapeDtypeStruct((b, nq_rows, s), BF16),
        scratch_shapes=[pltpu.VMEM((per, nb, MOBA_BLOCK, MOBA_GROUP * MOBA_BLOCK), F32)],
        compiler_params=_cparams(("parallel", "parallel", "arbitrary")),
        name="moba_attn",
    )(qt, k3, vt, km3)


def _row_copy(src_ref, src_row, dst_ref, dst_row, sem):
    def tile(ref, r):
        return ref.at[pl.ds(pl.multiple_of(r * ROW_TILE, ROW_TILE), ROW_TILE)]

    return pltpu.make_async_copy(tile(src_ref, src_row), tile(dst_ref, dst_row), sem)


def _for_each_row(n_rows, fn):
    def body(j8, carry):
        for u in range(DMA_UNROLL):
            fn(j8 * DMA_UNROLL + u, u % 2)
        return carry

    lax.fori_loop(0, n_rows // DMA_UNROLL, body, 0)


def _dispatch_kernel(pad_lo_ref, pad_n_ref, nu_ref, dest_ref, h_ref, xs_ref, sem, pad_sem, *,
                     tm, pad_bits, tile_rows, n_tiles):
    @pl.when(pl.program_id(0) == 0)
    def _():
        def fill_tile(j, carry):
            cp = pltpu.make_async_copy(
                h_ref.at[pl.ds(0, tile_rows * ROW_TILE)],
                xs_ref.at[pl.ds(pl.multiple_of(j * tile_rows * ROW_TILE, ROW_TILE),
                                tile_rows * ROW_TILE)], pad_sem)
            cp.start()
            cp.wait()
            return carry

        lax.fori_loop(nu_ref[0], n_tiles, fill_tile, 0)

        for start in (True, False):
            for e in range(N_EXPERTS):
                n = pad_n_ref[e]
                for bit in reversed(range(pad_bits)):
                    size = 1 << bit

                    @pl.when((n & size) != 0)
                    def _(n=n, e=e, bit=bit, size=size, start=start):
                        first = pad_lo_ref[e] + ((n >> (bit + 1)) << (bit + 1))
                        cp = pltpu.make_async_copy(
                            h_ref.at[pl.ds(0, size * ROW_TILE)],
                            xs_ref.at[pl.ds(pl.multiple_of(first * ROW_TILE, ROW_TILE),
                                            size * ROW_TILE)], pad_sem)
                        cp.start() if start else cp.wait()

    def copy(j, slot):
        return _row_copy(h_ref, j, xs_ref, dest_ref[0, 0, slot * tm + j], sem)

    for slot in range(2):
        _for_each_row(tm, lambda j, lane, slot=slot: copy(j, slot).start(priority=lane))
    for slot in range(2):
        _for_each_row(tm, lambda j, lane, slot=slot: copy(j, slot).wait())


def _dispatch(pad_lo, pad_n, n_used, dest3, h_rt, n_tiles, tile_rows, tm):
    assert tile_rows <= tm
    grid_spec = pltpu.PrefetchScalarGridSpec(
        num_scalar_prefetch=3,
        grid=(h_rt.shape[0] // (tm * ROW_TILE),),
        in_specs=[
            pl.BlockSpec((1, 1, 2 * tm), lambda i, lo, n, nu: (i, 0, 0), memory_space=pltpu.SMEM),
            pl.BlockSpec((tm * ROW_TILE, LANES), lambda i, lo, n, nu: (i, 0)),
        ],
        out_specs=pl.BlockSpec(memory_space=pl.ANY),
        scratch_shapes=[pltpu.SemaphoreType.DMA(()), pltpu.SemaphoreType.DMA(())],
    )
    return pl.pallas_call(
        functools.partial(_dispatch_kernel, tm=tm, pad_bits=(tile_rows - 1).bit_length(),
                          tile_rows=tile_rows, n_tiles=n_tiles),
        grid_spec=grid_spec,
        out_shape=jax.ShapeDtypeStruct((n_tiles * tile_rows * ROW_TILE, LANES), F32),
        compiler_params=_cparams(("arbitrary",)),
        name="moe_dispatch",
    )(pad_lo, pad_n, n_used, dest3, h_rt)


def _combine_kernel(dest_ref, next_ref, ys_ref, x_ref, gate_ref, o_ref, buf_ref, sem, *, tm):
    i = pl.program_id(0)
    cur = lax.rem(i, 2)

    def copy(idx_ref, buf_slot, j):
        return _row_copy(ys_ref, idx_ref[0, 0, j], buf_ref.at[buf_slot], j, sem.at[buf_slot])

    @pl.when(i == 0)
    def _():
        _for_each_row(2 * tm, lambda j, lane: copy(dest_ref, 0, j).start(priority=lane))

    @pl.when(i + 1 < pl.num_programs(0))
    def _():
        _for_each_row(2 * tm, lambda j, lane: copy(next_ref, 1 - cur, j).start(priority=lane))

    _for_each_row(2 * tm, lambda j, lane: copy(dest_ref, cur, j).wait())
    g1 = gate_ref[:, 0:1]
    g2 = gate_ref[:, 1:2]
    d = x_ref.shape[1]
    for slot in range(2):
        @pl.when(cur == slot)
        def _(slot=slot):
            rows = buf_ref.at[slot]
            y1 = _from_row_tiles(rows, 0, tm, d)
            y2 = _from_row_tiles(rows, tm * ROW_TILE, tm, d)
            o_ref[...] = x_ref[...] + (g1 * y1 + g2 * y2)


def _combine(dest3, ys, x2d, gates, tm):
    t, d = x2d.shape
    n = t // tm
    return pl.pallas_call(
        functools.partial(_combine_kernel, tm=tm),
        grid=(n,),
        in_specs=[
            pl.BlockSpec((1, 1, 2 * tm), lambda i: (i, 0, 0), memory_space=pltpu.SMEM),
            pl.BlockSpec((1, 1, 2 * tm), lambda i: (jnp.minimum(i + 1, n - 1), 0, 0),
                         memory_space=pltpu.SMEM),
            pl.BlockSpec(memory_space=pl.ANY),
            pl.BlockSpec((tm, d), lambda i: (i, 0)),
            pl.BlockSpec((tm, gates.shape[1]), lambda i: (i, 0)),
        ],
        out_specs=pl.BlockSpec((tm, d), lambda i: (i, 0)),
        out_shape=jax.ShapeDtypeStruct((t, d), F32),
        scratch_shapes=[pltpu.VMEM((2, 2 * tm * ROW_TILE, LANES), F32),
                        pltpu.SemaphoreType.DMA((2,))],
        compiler_params=_cparams(("arbitrary",)),
        name="moe_combine",
    )(dest3, dest3, ys, x2d, gates)


def _pad_heads_t(w_cols, offsets):
    d = w_cols.shape[0]
    nh = w_cols.shape[1] // HEAD_DIM
    wt = w_cols.T.reshape(nh, HEAD_DIM, d)
    hi = jnp.asarray(offsets, I32).reshape(nh, 1, 1) > 0
    z = jnp.zeros_like(wt)
    slab = jnp.concatenate([jnp.where(hi, z, wt), jnp.where(hi, wt, z)], axis=1)
    return slab.reshape(nh * HEAD_SLAB, d)


def _pad_head_cols(w):
    rows = w.shape[0]
    nh = w.shape[1] // HEAD_DIM
    w3 = w.reshape(rows, nh, HEAD_DIM)
    return jnp.concatenate([w3, jnp.zeros_like(w3)], axis=2).reshape(rows, nh * HEAD_SLAB)


def _head_block_ones(width):
    idx = jnp.arange(width) // HEAD_DIM
    return (idx[:, None] == idx[None, :]).astype(F32)


def _tile_gain(g, n_heads):
    return jnp.tile(g.astype(F32), n_heads).reshape(1, n_heads * HEAD_DIM)


def _s5_params(lam_re, lam_im, log_dt, b_re, b_im, c_re, c_im):
    f32 = F32
    g, p = lam_re.shape
    n = b_re.shape[2]
    lam = lax.complex(lam_re.astype(f32), lam_im.astype(f32))
    dt = jnp.exp(log_dt.astype(f32))[:, None]
    lam_bar = jnp.exp(lam * dt)
    b_bar = ((lam_bar - 1.0) / lam)[..., None] * lax.complex(b_re.astype(f32), b_im.astype(f32))
    ns = g // S5_GROUPS_PER_SLAB
    eye = jnp.eye(S5_GROUPS_PER_SLAB, dtype=f32)

    def in_map(part):
        blk = part.reshape(ns, S5_GROUPS_PER_SLAB, p, n).transpose(0, 1, 3, 2)
        return jnp.einsum('sgnp,gh->sgnhp', blk, eye).reshape(
            ns, S5_GROUPS_PER_SLAB * n, S5_GROUPS_PER_SLAB * p)

    def out_map(part):
        blk = part.reshape(ns, S5_GROUPS_PER_SLAB, n, p).transpose(0, 1, 3, 2)
        return jnp.einsum('sgpn,gh->sgphn', blk, eye).reshape(
            ns, S5_GROUPS_PER_SLAB * p, S5_GROUPS_PER_SLAB * n)

    bd = jnp.concatenate([in_map(jnp.real(b_bar)), in_map(jnp.imag(b_bar))], axis=2)
    cd = jnp.concatenate([out_map(c_re.astype(f32)), out_map(-c_im.astype(f32))], axis=1)
    ar = jnp.real(lam_bar).reshape(ns, S5_GROUPS_PER_SLAB * p)
    ai = jnp.imag(lam_bar).reshape(ns, S5_GROUPS_PER_SLAB * p)
    return bd.astype(BF16), cd.astype(BF16), ar, ai


def kernel(x, mem, l0_mix_norm, l0_w_in, l0_s5_lam_re, l0_s5_lam_im, l0_s5_log_dt, l0_s5_b_re, l0_s5_b_im, l0_s5_c_re, l0_s5_c_im, l0_s5_d, l0_s5_w_glu, l0_s5_b_glu, l0_mem_norm, l0_w_mem_k, l0_w_mem_v, l0_mem_q_gain, l0_mem_k_gain, l0_w_out, l0_ffn_norm, l0_ffn_w_gate, l0_ffn_w_up, l0_ffn_w_down, kv_norm, kv_w_k, kv_w_v, kv_k_gain, l1_mix_norm, l1_w_in, l1_moba_q_gain, l1_mem_norm, l1_w_mem_k, l1_w_mem_v, l1_mem_q_gain, l1_mem_k_gain, l1_w_out, l1_ffn_norm, l1_moe_router, l1_moe_w_gate, l1_moe_w_up, l1_moe_w_down):
    batch, seq, d = x.shape
    t = batch * seq
    main_w = l0_s5_w_glu.shape[0]
    mem_w = l0_w_mem_k.shape[1]
    n_q = main_w // HEAD_DIM
    tm = 512
    row = lambda v: v.astype(F32).reshape(1, -1)

    mem_off = [HEAD_DIM * (h % 2) for h in range(MEM_HEADS)]
    ones_kv = _head_block_ones(mem_w)
    x2d = x.reshape(t, d)

    main_in, qm0_t = _inproj0(
        x2d, row(l0_mix_norm), l0_w_in[:, :main_w].astype(BF16),
        _pad_heads_t(l0_w_in[:, main_w:], mem_off).astype(BF16), batch, seq, tm)
    k0, v0_t = _mem_kv(
        mem, row(l0_mem_norm), l0_w_mem_k.astype(BF16), l0_w_mem_v.T.astype(BF16), ones_kv,
        _tile_gain(l0_mem_k_gain, MEM_HEADS) * _tile_gain(l0_mem_q_gain, MEM_HEADS))

    bd, cd, ar, ai = _s5_params(l0_s5_lam_re, l0_s5_lam_im, l0_s5_log_dt, l0_s5_b_re, l0_s5_b_im,
                                l0_s5_c_re, l0_s5_c_im)
    s5_out = _s5(main_in.reshape(batch, seq, main_w), bd, cd, ar, ai, row(l0_s5_d),
                 l0_s5_w_glu.astype(BF16), row(l0_s5_b_glu), 64).reshape(t, main_w)

    x2 = _outproj0_ffn(s5_out, (qm0_t, k0, v0_t), l0_w_out[:main_w].astype(BF16),
                       l0_w_out[main_w:].astype(BF16), x2d, row(l0_ffn_norm),
                       l0_ffn_w_gate.astype(BF16), l0_ffn_w_up.astype(BF16),
                       l0_ffn_w_down.astype(BF16), seq, tm)

    wq1_t = jnp.concatenate([l1_w_in[:, :main_w].T,
                             _pad_heads_t(l1_w_in[:, main_w:], mem_off)], axis=0).astype(BF16)
    q_t, qm1_t, kq, km, v_t = _proj1(
        x2, row(l1_mix_norm), row(kv_norm), wq1_t, _pad_head_cols(kv_w_k).astype(BF16),
        kv_w_v.T.astype(BF16), _pad_head_cols(_tile_gain(kv_k_gain, MOBA_KV_HEADS)),
        _pad_head_cols(_tile_gain(l1_moba_q_gain, MOBA_KV_HEADS)), batch, seq, tm)
    nb = seq // MOBA_BLOCK
    moba_t = _moba(q_t, kq.reshape(batch, seq, -1), v_t, km.reshape(batch, nb, -1))
    k1, v1_t = _mem_kv(
        mem, row(l1_mem_norm), l1_w_mem_k.astype(BF16), l1_w_mem_v.T.astype(BF16), ones_kv,
        _tile_gain(l1_mem_k_gain, MEM_HEADS) * _tile_gain(l1_mem_q_gain, MEM_HEADS))

    wr_t = jnp.zeros((ROUTER_ROWS, d), F32).at[:N_EXPERTS].set(l1_moe_router.astype(F32).T)
    wr_hi = wr_t.astype(BF16)
    wr_lo = (wr_t - wr_hi.astype(F32)).astype(BF16)
    x3, h3, rec, cnt = _outproj1(moba_t, (qm1_t, k1, v1_t), l1_w_out[:main_w].astype(BF16),
                                 l1_w_out[main_w:].astype(BF16), x2, row(l1_ffn_norm),
                                 wr_hi, wr_lo, seq, tm)

    assert d == ROW_TILE * LANES
    tme = 1024
    counts = cnt[:N_EXPERTS, 0].astype(I32)
    padded = ((counts + tme - 1) // tme) * tme
    ends = jnp.cumsum(padded)
    starts = ends - padded
    max_tiles = (2 * t) // tme + N_EXPERTS
    def lookup(table, idx):
        hit = idx[None, :] == jnp.arange(N_EXPERTS, dtype=I32)[:, None]
        return jnp.sum(jnp.where(hit, table[:, None], 0), axis=0)

    tile_ids = jnp.arange(max_tiles, dtype=I32)
    tile_expert = jnp.minimum(
        jnp.sum((ends // tme)[None, :] <= tile_ids[:, None], axis=1), N_EXPERTS - 1).astype(I32)
    n_valid = jnp.clip(lookup(counts, tile_expert) + lookup(starts, tile_expert)
                       - tile_ids * tme, 0, tme)
    n_used = (ends[-1] // tme).astype(I32).reshape(1)
    d1 = lookup(starts, rec[0].astype(I32)) + rec[2].astype(I32)
    d2 = lookup(starts, rec[1].astype(I32)) + rec[3].astype(I32)
    gates = rec[4:6].T
    def tile_dest(tile):
        return jnp.concatenate([d1.reshape(t // tile, 1, tile), d2.reshape(t // tile, 1, tile)],
                               axis=2)

    tmd, tmc = 1024, 512
    xs = _dispatch(starts + counts, padded - counts, n_used, tile_dest(tmd), h3, max_tiles, tme,
                   tmd)
    ys = _moe_ffn(tile_expert, n_valid.astype(I32), n_used, xs, l1_moe_w_gate, l1_moe_w_up,
                  l1_moe_w_down, tme, 512)
    out = _combine(tile_dest(tmc), ys, x3, gates, tmc)
    return out.reshape(batch, seq, d)
```

```python
import functools

import jax
import jax.numpy as jnp
from jax import lax
from jax.experimental import pallas as pl
from jax.experimental.pallas import tpu as pltpu

F32 = jnp.float32
BF16 = jnp.bfloat16
I32 = jnp.int32

RMS_EPS = 1e-6
NEG_INF = -1e30
HEAD_DIM = 64
HEAD_SLAB = 128
MEM_HEADS = 4
MOBA_KV_HEADS = 4
MOBA_GROUP = 3
MOBA_BLOCK = 256
MOBA_TOP_BLOCKS = 3
MOBA_KV_PER_STEP = 2
S5_GROUP_DIM = 16
S5_STATE = 64
S5_GROUPS_PER_SLAB = HEAD_SLAB // S5_GROUP_DIM
S5_SLAB_STATES = S5_GROUPS_PER_SLAB * S5_STATE
N_EXPERTS = 8
ROUTER_LANES = 128
ROUTER_ROWS = 16
SUBLANES = 8
LANES = 128
ROW_TILE = SUBLANES
DMA_UNROLL = 8
LOG2_E = 1.4426950408889634
V_ROWS = HEAD_DIM + 16

VMEM_LIMIT_BYTES = 56 * 1024 * 1024

_NT = (((1,), (1,)), ((), ()))
_TN = (((0,), (0,)), ((), ()))


def _cparams(sem):
    return pltpu.CompilerParams(dimension_semantics=sem, vmem_limit_bytes=VMEM_LIMIT_BYTES)


def _rms(x, g):
    return x * lax.rsqrt(jnp.mean(x * x, axis=-1, keepdims=True) + RMS_EPS) * g


def _store_vt_with_ones(vt_ref, vt, n_heads):
    n = vt.shape[1]
    ones_row = jnp.where(lax.broadcasted_iota(I32, (V_ROWS - HEAD_DIM, n), 0) == 0, 1.0, 0.0)
    for h in range(n_heads):
        vt_ref[0, V_ROWS * h:V_ROWS * h + HEAD_DIM, :] = (
            vt[HEAD_DIM * h:HEAD_DIM * (h + 1), :].astype(BF16))
        vt_ref[0, V_ROWS * h + HEAD_DIM:V_ROWS * (h + 1), :] = ones_row.astype(BF16)


def _slab_norm_t(qt, n_heads, scale):
    outs = []
    for h in range(n_heads):
        s = qt[HEAD_SLAB * h:HEAD_SLAB * (h + 1), :]
        ms = jnp.sum(s * s, axis=0, keepdims=True) * (1.0 / HEAD_DIM)
        outs.append(s * (lax.rsqrt(ms + RMS_EPS) * scale))
    return outs


def _inproj0_kernel(x_ref, g_ref, wm_ref, wqt_ref, main_ref, qt_ref):
    hb = _rms(x_ref[...], g_ref[...]).astype(BF16)
    main_ref[...] = jnp.dot(hb, wm_ref[...], preferred_element_type=F32)
    qt = lax.dot_general(wqt_ref[...], hb, _NT, preferred_element_type=F32)
    for h, s in enumerate(_slab_norm_t(qt, MEM_HEADS, HEAD_DIM ** -0.5 * LOG2_E)):
        qt_ref[0, HEAD_SLAB * h:HEAD_SLAB * (h + 1), :] = s.astype(BF16)


def _inproj0(x2d, g, wm, wqt, batch, seq, tm):
    t, d = x2d.shape
    per_b = seq // tm
    nm = wm.shape[1]
    nq = wqt.shape[0]
    return pl.pallas_call(
        _inproj0_kernel,
        grid=(t // tm,),
        in_specs=[
            pl.BlockSpec((tm, d), lambda i: (i, 0)),
            pl.BlockSpec((1, d), lambda i: (0, 0)),
            pl.BlockSpec((d, nm), lambda i: (0, 0)),
            pl.BlockSpec((nq, d), lambda i: (0, 0)),
        ],
        out_specs=[
            pl.BlockSpec((tm, nm), lambda i: (i, 0)),
            pl.BlockSpec((1, nq, tm), lambda i: (i // per_b, 0, i % per_b)),
        ],
        out_shape=[
            jax.ShapeDtypeStruct((t, nm), F32),
            jax.ShapeDtypeStruct((batch, nq, seq), BF16),
        ],
        compiler_params=_cparams(("parallel",)),
        name="inproj0",
    )(x2d, g, wm, wqt)


def _mem_kv_kernel(mem_ref, g_ref, wk_ref, wvt_ref, ones_ref, kg_ref, k_ref, vt_ref):
    mb = _rms(mem_ref[0], g_ref[...]).astype(BF16)
    k = jnp.dot(mb, wk_ref[...], preferred_element_type=F32)
    ms = jnp.dot(k * k, ones_ref[...], preferred_element_type=F32,
                 precision=lax.Precision.HIGHEST) * (1.0 / HEAD_DIM)
    k_ref[0] = (k * lax.rsqrt(ms + RMS_EPS) * kg_ref[...]).astype(BF16)
    _store_vt_with_ones(vt_ref, lax.dot_general(wvt_ref[...], mb, _NT,
                                                preferred_element_type=F32), MEM_HEADS)


def _mem_kv(mem, g, wk, wvt, ones, kg):
    b, m, d = mem.shape
    w = wk.shape[1]
    v_rows = MEM_HEADS * V_ROWS
    return pl.pallas_call(
        _mem_kv_kernel,
        grid=(b,),
        in_specs=[
            pl.BlockSpec((1, m, d), lambda i: (i, 0, 0)),
            pl.BlockSpec((1, d), lambda i: (0, 0)),
            pl.BlockSpec((d, w), lambda i: (0, 0)),
            pl.BlockSpec((w, d), lambda i: (0, 0)),
            pl.BlockSpec((w, w), lambda i: (0, 0)),
            pl.BlockSpec((1, w), lambda i: (0, 0)),
        ],
        out_specs=[
            pl.BlockSpec((1, m, w), lambda i: (i, 0, 0)),
            pl.BlockSpec((1, v_rows, m), lambda i: (i, 0, 0)),
        ],
        out_shape=[
            jax.ShapeDtypeStruct((b, m, w), BF16),
            jax.ShapeDtypeStruct((b, v_rows, m), BF16),
        ],
        compiler_params=_cparams(("parallel",)),
        name="mem_kv",
    )(mem, g, wk, wvt, ones, kg)


def _mem_scores_t(qt_ref, k_ref):
    scores = []
    for h in range(MEM_HEADS):
        pair = h // 2
        k2 = k_ref[0, :, HEAD_SLAB * pair:HEAD_SLAB * (pair + 1)]
        q = qt_ref[0, HEAD_SLAB * h:HEAD_SLAB * (h + 1), :]
        scores.append(jnp.dot(k2, q, preferred_element_type=F32))
    return scores


def _mem_attend_t(scores, vt_ref):
    outs = []
    for h, s in enumerate(scores):
        p = jnp.exp2(s - jnp.max(s, axis=0, keepdims=True)).astype(BF16)
        v = vt_ref[0, V_ROWS * h:V_ROWS * (h + 1), :]
        acc = jnp.dot(v, p, preferred_element_type=F32)
        outs.append((acc[0:HEAD_DIM, :] / acc[HEAD_DIM:HEAD_DIM + 1, :]).astype(BF16))
    return jnp.concatenate(outs, axis=0)


def _mem_attn_specs(qt, k, vt, per_b, tm):
    return [
        pl.BlockSpec((1, qt.shape[1], tm), lambda i: (i // per_b, 0, i % per_b)),
        pl.BlockSpec((1,) + k.shape[1:], lambda i: (i // per_b, 0, 0)),
        pl.BlockSpec((1,) + vt.shape[1:], lambda i: (i // per_b, 0, 0)),
    ]


def _s5_kernel(u_ref, bd_ref, cd_ref, ar_ref, ai_ref, d_ref, wglu_ref, bglu_ref, o_ref,
               buf_ref, st_ref, ut_ref, ot_ref, *, ts, n_slabs):
    half = S5_SLAB_STATES
    width = 2 * half
    batch = u_ref.shape[0]

    @pl.when(pl.program_id(0) == 0)
    def _():
        st_ref[...] = jnp.zeros_like(st_ref)

    for b in range(batch):
        for j in range(n_slabs):
            ut_ref[j, pl.ds(b, ts, stride=batch), :] = u_ref[b, :, HEAD_SLAB * j:HEAD_SLAB * (j + 1)]

    for j in range(n_slabs):
        buf_ref[:, width * j:width * (j + 1)] = jnp.dot(
            ut_ref[j].astype(BF16), bd_ref[j], preferred_element_type=F32)

    for j in range(n_slabs):
        re = slice(width * j, width * j + half)
        im = slice(width * j + half, width * (j + 1))
        ar = jnp.broadcast_to(ar_ref[j:j + 1, :], (SUBLANES, half))
        ai = jnp.broadcast_to(ai_ref[j:j + 1, :], (SUBLANES, half))

        def step(t, carry, re=re, im=im, ar=ar, ai=ai):
            xr, xi = carry
            rows = pl.ds(pl.multiple_of(t * SUBLANES, SUBLANES), SUBLANES)
            nxr = ar * xr - ai * xi + buf_ref[rows, re]
            nxi = ar * xi + ai * xr + buf_ref[rows, im]
            buf_ref[rows, re] = nxr
            buf_ref[rows, im] = nxi
            return nxr, nxi

        xr, xi = lax.fori_loop(0, ts, step, (st_ref[:, re], st_ref[:, im]), unroll=True)
        st_ref[:, re] = xr
        st_ref[:, im] = xi

    ys = []
    for j in range(n_slabs):
        cols = slice(HEAD_SLAB * j, HEAD_SLAB * (j + 1))
        ys.append(jnp.dot(buf_ref[:, width * j:width * (j + 1)].astype(BF16), cd_ref[j],
                          preferred_element_type=F32) + d_ref[:, cols] * ut_ref[j])
    g = jax.nn.gelu(jnp.concatenate(ys, axis=1))
    z = jnp.dot(g.astype(BF16), wglu_ref[...], preferred_element_type=F32) + bglu_ref[...]
    out = g / (1.0 + jnp.exp(-z))
    for j in range(n_slabs):
        ot_ref[j] = out[:, HEAD_SLAB * j:HEAD_SLAB * (j + 1)]
    for b in range(batch):
        for j in range(n_slabs):
            o_ref[b, :, HEAD_SLAB * j:HEAD_SLAB * (j + 1)] = (
                ot_ref[j, pl.ds(b, ts, stride=batch), :].astype(BF16))


def _s5(u, bd, cd, ar, ai, dskip, wglu, bglu, ts):
    batch, seq, c = u.shape
    n_slabs = c // HEAD_SLAB
    width = 2 * S5_SLAB_STATES
    tile = ts * batch
    return pl.pallas_call(
        functools.partial(_s5_kernel, ts=ts, n_slabs=n_slabs),
        grid=(seq // ts,),
        in_specs=[
            pl.BlockSpec((batch, ts, c), lambda i: (0, i, 0)),
            pl.BlockSpec((n_slabs, HEAD_SLAB, width), lambda i: (0, 0, 0)),
            pl.BlockSpec((n_slabs, width, HEAD_SLAB), lambda i: (0, 0, 0)),
            pl.BlockSpec((n_slabs, S5_SLAB_STATES), lambda i: (0, 0)),
            pl.BlockSpec((n_slabs, S5_SLAB_STATES), lambda i: (0, 0)),
            pl.BlockSpec((1, c), lambda i: (0, 0)),
            pl.BlockSpec((c, c), lambda i: (0, 0)),
            pl.BlockSpec((1, c), lambda i: (0, 0)),
        ],
        out_specs=pl.BlockSpec((batch, ts, c), lambda i: (0, i, 0)),
        out_shape=jax.ShapeDtypeStruct((batch, seq, c), BF16),
        scratch_shapes=[
            pltpu.VMEM((tile, n_slabs * width), F32),
            pltpu.VMEM((batch, n_slabs * width), F32),
            pltpu.VMEM((n_slabs, tile, HEAD_SLAB), F32),
            pltpu.VMEM((n_slabs, tile, HEAD_SLAB), F32),
        ],
        compiler_params=_cparams(("arbitrary",)),
        name="s5_mixer",
    )(u, bd, cd, ar, ai, dskip, wglu, bglu)


def _outproj_body(a, a_transposed, mem_refs, wa_ref, wb_ref, x_ref, g_ref):
    qt_ref, mk_ref, mvt_ref = mem_refs
    dn = _TN if a_transposed else (((1,), (0,)), ((), ()))
    scores = _mem_scores_t(qt_ref, mk_ref)
    y = lax.dot_general(a, wa_ref[...], dn, preferred_element_type=F32)
    y = y + lax.dot_general(_mem_attend_t(scores, mvt_ref), wb_ref[...], _TN,
                            preferred_element_type=F32)
    xn = x_ref[...] + y
    return xn, _rms(xn, g_ref[...])


def _outproj0_ffn_kernel(a_ref, qt_ref, mk_ref, mvt_ref, wa_ref, wb_ref, x_ref, g_ref,
                         wg_ref, wu_ref, wd_ref, o_ref, h_ref):
    xn, h = _outproj_body(a_ref[...], False, (qt_ref, mk_ref, mvt_ref), wa_ref, wb_ref, x_ref,
                          g_ref)
    h_ref[...] = h.astype(BF16)
    half = h_ref.shape[0] // 2
    halves = (slice(0, half), slice(half, 2 * half))
    mids = []
    for rows in halves:
        hb = h_ref[rows, :]
        gt = jnp.dot(hb, wg_ref[...], preferred_element_type=F32)
        up = jnp.dot(hb, wu_ref[...], preferred_element_type=F32)
        mids.append(((gt / (1.0 + jnp.exp(-gt))) * up).astype(BF16))
    for rows, mid in zip(halves, mids):
        o_ref[rows, :] = xn[rows, :] + jnp.dot(mid, wd_ref[...], preferred_element_type=F32)


def _outproj0_ffn(a, mem_qkv, wa, wb, x2d, g, wg, wu, wd, seq, tm):
    t, d = x2d.shape
    per_b = seq // tm
    ka, kb = a.shape[1], wb.shape[0]
    ff = wg.shape[1]
    resident = dict(pipeline_mode=pl.Buffered(1))
    return pl.pallas_call(
        _outproj0_ffn_kernel,
        grid=(t // tm,),
        in_specs=[
            pl.BlockSpec((tm, ka), lambda i: (i, 0)),
            *_mem_attn_specs(*mem_qkv, per_b, tm),
            pl.BlockSpec((ka, d), lambda i: (0, 0), **resident),
            pl.BlockSpec((kb, d), lambda i: (0, 0), **resident),
            pl.BlockSpec((tm, d), lambda i: (i, 0)),
            pl.BlockSpec((1, d), lambda i: (0, 0)),
            pl.BlockSpec((d, ff), lambda i: (0, 0), **resident),
            pl.BlockSpec((d, ff), lambda i: (0, 0), **resident),
            pl.BlockSpec((ff, d), lambda i: (0, 0), **resident),
        ],
        out_specs=pl.BlockSpec((tm, d), lambda i: (i, 0)),
        out_shape=jax.ShapeDtypeStruct((t, d), F32),
        scratch_shapes=[pltpu.VMEM((tm, d), BF16)],
        compiler_params=_cparams(("parallel",)),
        name="outproj0_ffn",
    )(a, *mem_qkv, wa, wb, x2d, g, wg, wu, wd)


def _outproj1_kernel(at_ref, qt_ref, mk_ref, mvt_ref, wa_ref, wb_ref, x_ref, g_ref, wrh_ref,
                     wrl_ref, xo_ref, h_ref, rec_ref, cnt_ref, run_ref, *, tm):
    i = pl.program_id(0)

    @pl.when(i == 0)
    def _():
        run_ref[...] = jnp.zeros_like(run_ref)

    xn, h = _outproj_body(at_ref[0], True, (qt_ref, mk_ref, mvt_ref), wa_ref, wb_ref, x_ref,
                          g_ref)
    xo_ref[...] = xn
    _to_row_tiles(h_ref, h)

    h_hi = h.astype(BF16)
    h_lo = (h - h_hi.astype(F32)).astype(BF16)
    logits = (lax.dot_general(wrh_ref[...], h_hi, _NT, preferred_element_type=F32)
              + lax.dot_general(wrh_ref[...], h_lo, _NT, preferred_element_type=F32)
              + lax.dot_general(wrl_ref[...], h_hi, _NT, preferred_element_type=F32))
    ex = lax.broadcasted_iota(I32, (ROUTER_ROWS, tm), 0)
    logits = jnp.where(ex < N_EXPERTS, logits, -jnp.inf)
    m1 = jnp.max(logits, axis=0, keepdims=True)
    e1 = jnp.min(jnp.where(logits == m1, ex, ROUTER_ROWS), axis=0, keepdims=True)
    rest = jnp.where(ex == e1, -jnp.inf, logits)
    m2 = jnp.max(rest, axis=0, keepdims=True)
    e2 = jnp.min(jnp.where(rest == m2, ex, ROUTER_ROWS), axis=0, keepdims=True)
    w2 = jnp.exp(m2 - m1)
    g1 = 1.0 / (1.0 + w2)
    g2 = w2 / (1.0 + w2)

    hot = jnp.where((ex == e1) | (ex == e2), 1.0, 0.0)
    r_io = lax.broadcasted_iota(I32, (tm, tm), 0)
    c_io = lax.broadcasted_iota(I32, (tm, tm), 1)
    triu = jnp.where(r_io < c_io, 1.0, 0.0).astype(BF16)
    before = (jnp.dot(hot.astype(BF16), triu, preferred_element_type=F32)
              + run_ref[:, 0:1])
    r1 = jnp.sum(jnp.where(ex == e1, before, 0.0), axis=0, keepdims=True)
    r2 = jnp.sum(jnp.where(ex == e2, before, 0.0), axis=0, keepdims=True)
    run = run_ref[...] + jnp.sum(hot, axis=1, keepdims=True)
    run_ref[...] = run
    cnt_ref[...] = run

    slot = lax.broadcasted_iota(I32, (SUBLANES, tm), 0)
    rec = jnp.where(slot == 0, e1.astype(F32), 0.0)
    rec = jnp.where(slot == 1, e2.astype(F32), rec)
    rec = jnp.where(slot == 2, r1, rec)
    rec = jnp.where(slot == 3, r2, rec)
    rec = jnp.where(slot == 4, g1, rec)
    rec = jnp.where(slot == 5, g2, rec)
    rec_ref[...] = rec


def _outproj1(at, mem_qkv, wa, wb, x2d, g, wr_hi, wr_lo, seq, tm):
    t, d = x2d.shape
    per_b = seq // tm
    ka, kb = at.shape[1], wb.shape[0]
    return pl.pallas_call(
        functools.partial(_outproj1_kernel, tm=tm),
        grid=(t // tm,),
        in_specs=[
            pl.BlockSpec((1, ka, tm), lambda i: (i // per_b, 0, i % per_b)),
            *_mem_attn_specs(*mem_qkv, per_b, tm),
            pl.BlockSpec((ka, d), lambda i: (0, 0)),
            pl.BlockSpec((kb, d), lambda i: (0, 0)),
            pl.BlockSpec((tm, d), lambda i: (i, 0)),
            pl.BlockSpec((1, d), lambda i: (0, 0)),
            pl.BlockSpec((ROUTER_ROWS, d), lambda i: (0, 0)),
            pl.BlockSpec((ROUTER_ROWS, d), lambda i: (0, 0)),
        ],
        out_specs=[
            pl.BlockSpec((tm, d), lambda i: (i, 0)),
            pl.BlockSpec((tm * ROW_TILE, LANES), lambda i: (i, 0)),
            pl.BlockSpec((SUBLANES, tm), lambda i: (0, i)),
            pl.BlockSpec((ROUTER_ROWS, ROUTER_LANES), lambda i: (0, 0)),
        ],
        out_shape=[
            jax.ShapeDtypeStruct((t, d), F32),
            jax.ShapeDtypeStruct((t * ROW_TILE, LANES), F32),
            jax.ShapeDtypeStruct((SUBLANES, t), F32),
            jax.ShapeDtypeStruct((ROUTER_ROWS, ROUTER_LANES), F32),
        ],
        scratch_shapes=[pltpu.VMEM((ROUTER_ROWS, ROUTER_LANES), F32)],
        compiler_params=_cparams(("arbitrary",)),
        name="outproj1_router",
    )(at, *mem_qkv, wa, wb, x2d, g, wr_hi, wr_lo)


def _from_row_tiles(ref, base, n_rows, n_feat):
    parts = [ref[pl.ds(base + k, n_rows, stride=ROW_TILE), :] for k in range(n_feat // LANES)]
    return jnp.concatenate(parts, axis=1)


def _to_row_tiles(ref, value):
    for k in range(value.shape[1] // LANES):
        ref[pl.ds(k, value.shape[0], stride=ROW_TILE), :] = value[:, LANES * k:LANES * (k + 1)]


def _moe_ffn_kernel(te_ref, nv_ref, nu_ref, xs_ref, wg_hbm, wu_hbm, wd_hbm, o_ref,
                    acc_ref, xb_ref, wg_buf, wu_buf, wd_buf, sem, *, tm, tf, nf):
    i = pl.program_id(0)
    n_used = nu_ref[0]
    half = tm // 2
    quarter = tm // 4
    d = xb_ref.shape[1]

    def weight_copies(tile, f, slot):
        e = te_ref[tile]
        cols = pl.ds(pl.multiple_of(f * tf, tf), tf)
        return (pltpu.make_async_copy(wg_hbm.at[e, :, cols], wg_buf.at[slot], sem.at[slot, 0]),
                pltpu.make_async_copy(wu_hbm.at[e, :, cols], wu_buf.at[slot], sem.at[slot, 1]),
                pltpu.make_async_copy(wd_hbm.at[e, cols, :], wd_buf.at[slot], sem.at[slot, 2]))

    def partial_out(slot, row_sets):
        wg = wg_buf[slot].astype(BF16)
        wu = wu_buf[slot].astype(BF16)
        wd = wd_buf[slot].astype(BF16)
        mids = []
        for rows in row_sets:
            xb = xb_ref[rows, :]
            gt = jnp.dot(xb, wg, preferred_element_type=F32)
            up = jnp.dot(xb, wu, preferred_element_type=F32)
            mids.append(((gt / (1.0 + jnp.exp(-gt))) * up).astype(BF16))
        for rows, mid in zip(row_sets, mids):
            acc_ref[rows, :] += jnp.dot(mid, wd, preferred_element_type=F32)

    @pl.when(i < n_used)
    def _():
        @pl.when(i == 0)
        def _():
            for cp in weight_copies(0, 0, 0):
                cp.start()

        acc_ref[...] = jnp.zeros_like(acc_ref)
        xb_ref[...] = _from_row_tiles(xs_ref, 0, tm, d).astype(BF16)
        nv = nv_ref[i]

        def f_step(f, carry):
            slot = lax.rem(i * nf + f, 2)
            last = f == nf - 1
            nxt_tile = jnp.where(last, i + 1, i)
            nxt_f = jnp.where(last, 0, f + 1)

            @pl.when(nxt_tile < n_used)
            def _():
                for cp in weight_copies(nxt_tile, nxt_f, 1 - slot):
                    cp.start()

            for cp in weight_copies(i, f, slot):
                cp.wait()

            @pl.when(nv > 3 * quarter)
            def _():
                partial_out(slot, [slice(0, half), slice(half, tm)])

            @pl.when((nv > half) & (nv <= 3 * quarter))
            def _():
                partial_out(slot, [slice(0, half), slice(half, 3 * quarter)])

            @pl.when((nv > quarter) & (nv <= half))
            def _():
                partial_out(slot, [slice(0, half)])

            @pl.when(nv <= quarter)
            def _():
                partial_out(slot, [slice(0, quarter)])

            return carry

        lax.fori_loop(0, nf, f_step, 0)
        _to_row_tiles(o_ref, acc_ref[...])

    @pl.when(i >= n_used)
    def _():
        o_ref[...] = jnp.zeros_like(o_ref)


def _moe_ffn(tile_expert, n_valid, n_used, xs_rt, wg, wu, wd, tm, tf):
    d = wg.shape[1]
    ff = wg.shape[2]
    nf = ff // tf
    n_tiles = xs_rt.shape[0] // (tm * ROW_TILE)

    def live(i, nu):
        return jnp.minimum(i, jnp.maximum(nu[0] - 1, 0))

    grid_spec = pltpu.PrefetchScalarGridSpec(
        num_scalar_prefetch=3,
        grid=(n_tiles,),
        in_specs=[
            pl.BlockSpec((tm * ROW_TILE, LANES), lambda i, te, nv, nu: (live(i, nu), 0)),
            pl.BlockSpec(memory_space=pl.ANY),
            pl.BlockSpec(memory_space=pl.ANY),
            pl.BlockSpec(memory_space=pl.ANY),
        ],
        out_specs=pl.BlockSpec((tm * ROW_TILE, LANES), lambda i, te, nv, nu: (i, 0)),
        scratch_shapes=[pltpu.VMEM((tm, d), F32), pltpu.VMEM((tm, d), BF16),
                        pltpu.VMEM((2, d, tf), wg.dtype), pltpu.VMEM((2, d, tf), wu.dtype),
                        pltpu.VMEM((2, tf, d), wd.dtype), pltpu.SemaphoreType.DMA((2, 3))],
    )
    return pl.pallas_call(
        functools.partial(_moe_ffn_kernel, tm=tm, tf=tf, nf=nf),
        grid_spec=grid_spec,
        out_shape=jax.ShapeDtypeStruct(xs_rt.shape, F32),
        compiler_params=_cparams(("arbitrary",)),
        name="ffn_moe",
    )(tile_expert, n_valid, n_used, xs_rt, wg, wu, wd)


def _proj1_kernel(x_ref, g1_ref, g2_ref, wqt_ref, wk_ref, wvt_ref, kg_ref, qg_ref,
                  qt_ref, qmt_ref, k_ref, km_ref, vt_ref, *, tm, n_q, per_b):
    x = x_ref[...]
    h1 = _rms(x, g1_ref[...]).astype(BF16)
    h2 = _rms(x, g2_ref[...]).astype(BF16)
    qt = lax.dot_general(wqt_ref[...], h1, _NT, preferred_element_type=F32)

    for h in range(n_q):
        s = qt[HEAD_DIM * h:HEAD_DIM * (h + 1), :]
        ms = jnp.mean(s * s, axis=0, keepdims=True)
        qt_ref[0, HEAD_DIM * h:HEAD_DIM * (h + 1), :] = (
            s * (lax.rsqrt(ms + RMS_EPS) * (HEAD_DIM ** -0.5 * LOG2_E))).astype(BF16)
    qm = qt[HEAD_DIM * n_q:, :]
    for h, s in enumerate(_slab_norm_t(qm, MEM_HEADS, HEAD_DIM ** -0.5 * LOG2_E)):
        qmt_ref[0, HEAD_SLAB * h:HEAD_SLAB * (h + 1), :] = s.astype(BF16)

    k = jnp.dot(h2, wk_ref[...], preferred_element_type=F32)
    nblk = tm // MOBA_BLOCK
    first_blk = lax.rem(pl.program_id(0), per_b) * nblk
    lane = lax.broadcasted_iota(I32, (MOBA_BLOCK, HEAD_SLAB), 1)
    for kv in range(MOBA_KV_HEADS):
        cols = slice(HEAD_SLAB * kv, HEAD_SLAB * (kv + 1))
        ks = k[:, cols]
        ms = jnp.sum(ks * ks, axis=-1, keepdims=True) * (1.0 / HEAD_DIM)
        kn = ks * lax.rsqrt(ms + RMS_EPS) * kg_ref[:, cols]
        kq = kn * qg_ref[:, cols]
        for j in range(nblk):
            rows = slice(MOBA_BLOCK * j, MOBA_BLOCK * (j + 1))
            km_ref[j, :, cols] = jnp.mean(kn[rows], axis=0, keepdims=True) * qg_ref[:, cols]
            k_ref[rows, cols] = jnp.where(lane == HEAD_DIM + first_blk + j, 1.0,
                                          kq[rows]).astype(BF16)

    _store_vt_with_ones(vt_ref, lax.dot_general(wvt_ref[...], h2, _NT,
                                                preferred_element_type=F32), MOBA_KV_HEADS)


def _proj1(x2d, g1, g2, wqt, wk, wvt, kg, qg, batch, seq, tm):
    t, d = x2d.shape
    per_b = seq // tm
    nq_rows = wqt.shape[0]
    n_q = (nq_rows - MEM_HEADS * HEAD_SLAB) // HEAD_DIM
    kw = wk.shape[1]
    vw = wvt.shape[0]
    v_rows = MOBA_KV_HEADS * V_ROWS
    nblk = tm // MOBA_BLOCK
    return pl.pallas_call(
        functools.partial(_proj1_kernel, tm=tm, n_q=n_q, per_b=per_b),
        grid=(t // tm,),
        in_specs=[
            pl.BlockSpec((tm, d), lambda i: (i, 0)),
            pl.BlockSpec((1, d), lambda i: (0, 0)),
            pl.BlockSpec((1, d), lambda i: (0, 0)),
            pl.BlockSpec((nq_rows, d), lambda i: (0, 0)),
            pl.BlockSpec((d, kw), lambda i: (0, 0)),
            pl.BlockSpec((vw, d), lambda i: (0, 0)),
            pl.BlockSpec((1, kw), lambda i: (0, 0)),
            pl.BlockSpec((1, kw), lambda i: (0, 0)),
        ],
        out_specs=[
            pl.BlockSpec((1, n_q * HEAD_DIM, tm), lambda i: (i // per_b, 0, i % per_b)),
            pl.BlockSpec((1, MEM_HEADS * HEAD_SLAB, tm), lambda i: (i // per_b, 0, i % per_b)),
            pl.BlockSpec((tm, kw), lambda i: (i, 0)),
            pl.BlockSpec((nblk, 1, kw), lambda i: (i, 0, 0)),
            pl.BlockSpec((1, v_rows, tm), lambda i: (i // per_b, 0, i % per_b)),
        ],
        out_shape=[
            jax.ShapeDtypeStruct((batch, n_q * HEAD_DIM, seq), BF16),
            jax.ShapeDtypeStruct((batch, MEM_HEADS * HEAD_SLAB, seq), BF16),
            jax.ShapeDtypeStruct((t, kw), BF16),
            jax.ShapeDtypeStruct((t // MOBA_BLOCK, 1, kw), F32),
            jax.ShapeDtypeStruct((batch, v_rows, seq), BF16),
        ],
        compiler_params=_cparams(("parallel",)),
        name="proj1",
    )(x2d, g1, g2, wqt, wk, wvt, kg, qg)


def _moba_kernel(qt_ref, qtn_ref, k_ref, vt_ref, km_ref, o_ref, s_ref, bias_ref):
    qb = pl.program_id(2)
    nb = km_ref.shape[1]
    bq = MOBA_BLOCK
    nq = MOBA_GROUP * bq
    q_rows = MOBA_GROUP * HEAD_DIM
    blk = lax.broadcasted_iota(I32, (nb, nq), 0)

    def queries(ref, kv):
        return jnp.concatenate(
            [ref[0, q_rows * kv + HEAD_DIM * g:q_rows * kv + HEAD_DIM * (g + 1), :]
             for g in range(MOBA_GROUP)], axis=1)

    def block_bias(q64, kv, q_blk):
        km = km_ref[0, :, HEAD_SLAB * kv:HEAD_SLAB * kv + HEAD_DIM]
        km_hi = km.astype(BF16)
        km_lo = (km - km_hi.astype(F32)).astype(BF16)
        gate = (jnp.dot(km_hi, q64, preferred_element_type=F32)
                + jnp.dot(km_lo, q64, preferred_element_type=F32))
        gate = jnp.where(blk < q_blk, gate, NEG_INF)
        rank = jnp.zeros((nb, nq), F32)
        for m in range(nb):
            gm = gate[m:m + 1, :]
            rank = rank + jnp.where(gm > gate, 1.0,
                                    jnp.where((gm == gate) & (blk > m), 1.0, 0.0))
        dropped = (blk < q_blk) & (rank >= MOBA_TOP_BLOCKS)
        bias = jnp.where(dropped, NEG_INF, 0.0)
        return jnp.concatenate([bias, jnp.zeros((nb, nq), F32)], axis=0).astype(BF16)

    key_pos = lax.broadcasted_iota(I32, (bq, nq), 0)
    qry_pos = lax.broadcasted_iota(I32, (bq, nq), 1) & (bq - 1)
    causal = key_pos <= qry_pos

    for n_blocks in range(1, nb + 1):
        @pl.when(qb == n_blocks - 1)
        def _(n_blocks=n_blocks):
            mx = []
            for kv in range(MOBA_KV_PER_STEP):
                bias = bias_ref[kv] if n_blocks > 1 else jnp.zeros((2 * nb, nq), BF16)
                q3 = jnp.concatenate(
                    [queries(qt_ref, kv), bias,
                     jnp.zeros((HEAD_SLAB - HEAD_DIM - 2 * nb, nq), BF16)], axis=0)
                m = None
                for n in range(n_blocks):
                    s = jnp.dot(k_ref[0, bq * n:bq * (n + 1), HEAD_SLAB * kv:HEAD_SLAB * (kv + 1)],
                                q3, preferred_element_type=F32)
                    if n == n_blocks - 1:
                        s = jnp.where(causal, s, NEG_INF)
                    s_ref[kv, n] = s
                    cm = jnp.max(s, axis=0, keepdims=True)
                    m = cm if m is None else jnp.maximum(m, cm)
                mx.append(m)
            if n_blocks < nb:
                for kv in range(MOBA_KV_PER_STEP):
                    bias_ref[kv] = block_bias(queries(qtn_ref, kv), kv, n_blocks)
            for kv in range(MOBA_KV_PER_STEP):
                acc = jnp.zeros((V_ROWS, nq), F32)
                for n in range(n_blocks):
                    p = jnp.exp2(s_ref[kv, n] - mx[kv]).astype(BF16)
                    acc = acc + jnp.dot(vt_ref[0, V_ROWS * kv:V_ROWS * (kv + 1),
                                               bq * n:bq * (n + 1)], p,
                                        preferred_element_type=F32)
                out = acc[0:HEAD_DIM, :] / acc[HEAD_DIM:HEAD_DIM + 1, :]
                for g in range(MOBA_GROUP):
                    r0 = q_rows * kv + HEAD_DIM * g
                    o_ref[0, r0:r0 + HEAD_DIM, :] = out[:, bq * g:bq * (g + 1)].astype(BF16)


def _moba(qt, k3, vt, km3):
    b, nq_rows, s = qt.shape
    nb = s // MOBA_BLOCK
    per = MOBA_KV_PER_STEP
    rows = per * MOBA_GROUP * HEAD_DIM
    return pl.pallas_call(
        _moba_kernel,
        grid=(b, MOBA_KV_HEADS // per, nb),
        in_specs=[
            pl.BlockSpec((1, rows, MOBA_BLOCK), lambda i, j, q: (i, j, q)),
            pl.BlockSpec((1, rows, MOBA_BLOCK), lambda i, j, q: (i, j, jnp.minimum(q + 1, nb - 1))),
            pl.BlockSpec((1, s, per * HEAD_SLAB), lambda i, j, q: (i, 0, j)),
            pl.BlockSpec((1, per * V_ROWS, s), lambda i, j, q: (i, j, 0)),
            pl.BlockSpec((1, nb, per * HEAD_SLAB), lambda i, j, q: (i, 0, j)),
        ],
        out_specs=pl.BlockSpec((1, rows, MOBA_BLOCK), lambda i, j, q: (i, j, q)),
        out_shape=jax.ShapeDtypeStruct((b, nq_rows, s), BF16),
        scratch_shapes=[pltpu.VMEM((per, nb, MOBA_BLOCK, MOBA_GROUP * MOBA_BLOCK), F32),
                        pltpu.VMEM((per, 2 * nb, MOBA_GROUP * MOBA_BLOCK), BF16)],
        compiler_params=_cparams(("parallel", "parallel", "arbitrary")),
        name="moba_attn",
    )(qt, qt, k3, vt, km3)


def _row_copy(src_ref, src_row, dst_ref, dst_row, sem):
    def tile(ref, r):
        return ref.at[pl.ds(pl.multiple_of(r * ROW_TILE, ROW_TILE), ROW_TILE)]

    return pltpu.make_async_copy(tile(src_ref, src_row), tile(dst_ref, dst_row), sem)


def _for_each_row(n_rows, fn):
    def body(j8, carry):
        for u in range(DMA_UNROLL):
            fn(j8 * DMA_UNROLL + u, u % 2)
        return carry

    lax.fori_loop(0, n_rows // DMA_UNROLL, body, 0)


def _dispatch_kernel(pad_lo_ref, pad_n_ref, nu_ref, dest_ref, h_ref, xs_ref, sem, pad_sem, *,
                     tm, pad_bits, tile_rows, n_tiles):
    @pl.when(pl.program_id(0) == 0)
    def _():
        def fill_tile(j, carry):
            cp = pltpu.make_async_copy(
                h_ref.at[pl.ds(0, tile_rows * ROW_TILE)],
                xs_ref.at[pl.ds(pl.multiple_of(j * tile_rows * ROW_TILE, ROW_TILE),
                                tile_rows * ROW_TILE)], pad_sem)
            cp.start()
            cp.wait()
            return carry

        lax.fori_loop(nu_ref[0], n_tiles, fill_tile, 0)

        for start in (True, False):
            for e in range(N_EXPERTS):
                n = pad_n_ref[e]
                for bit in reversed(range(pad_bits)):
                    size = 1 << bit

                    @pl.when((n & size) != 0)
                    def _(n=n, e=e, bit=bit, size=size, start=start):
                        first = pad_lo_ref[e] + ((n >> (bit + 1)) << (bit + 1))
                        cp = pltpu.make_async_copy(
                            h_ref.at[pl.ds(0, size * ROW_TILE)],
                            xs_ref.at[pl.ds(pl.multiple_of(first * ROW_TILE, ROW_TILE),
                                            size * ROW_TILE)], pad_sem)
                        cp.start() if start else cp.wait()

    def copy(j, slot):
        return _row_copy(h_ref, j, xs_ref, dest_ref[0, 0, slot * tm + j], sem)

    for slot in range(2):
        _for_each_row(tm, lambda j, lane, slot=slot: copy(j, slot).start(priority=lane))
    for slot in range(2):
        _for_each_row(tm, lambda j, lane, slot=slot: copy(j, slot).wait())


def _dispatch(pad_lo, pad_n, n_used, dest3, h_rt, n_tiles, tile_rows, tm):
    assert tile_rows <= tm
    grid_spec = pltpu.PrefetchScalarGridSpec(
        num_scalar_prefetch=3,
        grid=(h_rt.shape[0] // (tm * ROW_TILE),),
        in_specs=[
            pl.BlockSpec((1, 1, 2 * tm), lambda i, lo, n, nu: (i, 0, 0), memory_space=pltpu.SMEM),
            pl.BlockSpec((tm * ROW_TILE, LANES), lambda i, lo, n, nu: (i, 0)),
        ],
        out_specs=pl.BlockSpec(memory_space=pl.ANY),
        scratch_shapes=[pltpu.SemaphoreType.DMA(()), pltpu.SemaphoreType.DMA(())],
    )
    return pl.pallas_call(
        functools.partial(_dispatch_kernel, tm=tm, pad_bits=(tile_rows - 1).bit_length(),
                          tile_rows=tile_rows, n_tiles=n_tiles),
        grid_spec=grid_spec,
        out_shape=jax.ShapeDtypeStruct((n_tiles * tile_rows * ROW_TILE, LANES), F32),
        compiler_params=_cparams(("arbitrary",)),
        name="moe_dispatch",
    )(pad_lo, pad_n, n_used, dest3, h_rt)


def _combine_kernel(dest_ref, next_ref, ys_ref, x_ref, gate_ref, o_ref, buf_ref, sem, *, tm):
    i = pl.program_id(0)
    cur = lax.rem(i, 2)

    def copy(idx_ref, buf_slot, j):
        return _row_copy(ys_ref, idx_ref[0, 0, j], buf_ref.at[buf_slot], j, sem.at[buf_slot])

    @pl.when(i == 0)
    def _():
        _for_each_row(2 * tm, lambda j, lane: copy(dest_ref, 0, j).start(priority=lane))

    @pl.when(i + 1 < pl.num_programs(0))
    def _():
        _for_each_row(2 * tm, lambda j, lane: copy(next_ref, 1 - cur, j).start(priority=lane))

    _for_each_row(2 * tm, lambda j, lane: copy(dest_ref, cur, j).wait())
    g1 = gate_ref[:, 0:1]
    g2 = gate_ref[:, 1:2]
    d = x_ref.shape[1]
    for slot in range(2):
        @pl.when(cur == slot)
        def _(slot=slot):
            rows = buf_ref.at[slot]
            y1 = _from_row_tiles(rows, 0, tm, d)
            y2 = _from_row_tiles(rows, tm * ROW_TILE, tm, d)
            o_ref[...] = x_ref[...] + (g1 * y1 + g2 * y2)


def _combine(dest3, ys, x2d, gates, tm):
    t, d = x2d.shape
    n = t // tm
    return pl.pallas_call(
        functools.partial(_combine_kernel, tm=tm),
        grid=(n,),
        in_specs=[
            pl.BlockSpec((1, 1, 2 * tm), lambda i: (i, 0, 0), memory_space=pltpu.SMEM),
            pl.BlockSpec((1, 1, 2 * tm), lambda i: (jnp.minimum(i + 1, n - 1), 0, 0),
                         memory_space=pltpu.SMEM),
            pl.BlockSpec(memory_space=pl.ANY),
            pl.BlockSpec((tm, d), lambda i: (i, 0)),
            pl.BlockSpec((tm, gates.shape[1]), lambda i: (i, 0)),
        ],
        out_specs=pl.BlockSpec((tm, d), lambda i: (i, 0)),
        out_shape=jax.ShapeDtypeStruct((t, d), F32),
        scratch_shapes=[pltpu.VMEM((2, 2 * tm * ROW_TILE, LANES), F32),
                        pltpu.SemaphoreType.DMA((2,))],
        compiler_params=_cparams(("arbitrary",)),
        name="moe_combine",
    )(dest3, dest3, ys, x2d, gates)


def _pad_heads_t(w_cols, offsets):
    d = w_cols.shape[0]
    nh = w_cols.shape[1] // HEAD_DIM
    wt = w_cols.T.reshape(nh, HEAD_DIM, d)
    hi = jnp.asarray(offsets, I32).reshape(nh, 1, 1) > 0
    z = jnp.zeros_like(wt)
    slab = jnp.concatenate([jnp.where(hi, z, wt), jnp.where(hi, wt, z)], axis=1)
    return slab.reshape(nh * HEAD_SLAB, d)


def _pad_head_cols(w):
    rows = w.shape[0]
    nh = w.shape[1] // HEAD_DIM
    w3 = w.reshape(rows, nh, HEAD_DIM)
    return jnp.concatenate([w3, jnp.zeros_like(w3)], axis=2).reshape(rows, nh * HEAD_SLAB)


def _head_block_ones(width):
    idx = jnp.arange(width) // HEAD_DIM
    return (idx[:, None] == idx[None, :]).astype(F32)


def _tile_gain(g, n_heads):
    return jnp.tile(g.astype(F32), n_heads).reshape(1, n_heads * HEAD_DIM)


def _s5_params(lam_re, lam_im, log_dt, b_re, b_im, c_re, c_im):
    f32 = F32
    g, p = lam_re.shape
    n = b_re.shape[2]
    lam = lax.complex(lam_re.astype(f32), lam_im.astype(f32))
    dt = jnp.exp(log_dt.astype(f32))[:, None]
    lam_bar = jnp.exp(lam * dt)
    b_bar = ((lam_bar - 1.0) / lam)[..., None] * lax.complex(b_re.astype(f32), b_im.astype(f32))
    ns = g // S5_GROUPS_PER_SLAB
    eye = jnp.eye(S5_GROUPS_PER_SLAB, dtype=f32)

    def in_map(part):
        blk = part.reshape(ns, S5_GROUPS_PER_SLAB, p, n).transpose(0, 1, 3, 2)
        return jnp.einsum('sgnp,gh->sgnhp', blk, eye).reshape(
            ns, S5_GROUPS_PER_SLAB * n, S5_GROUPS_PER_SLAB * p)

    def out_map(part):
        blk = part.reshape(ns, S5_GROUPS_PER_SLAB, n, p).transpose(0, 1, 3, 2)
        return jnp.einsum('sgpn,gh->sgphn', blk, eye).reshape(
            ns, S5_GROUPS_PER_SLAB * p, S5_GROUPS_PER_SLAB * n)

    bd = jnp.concatenate([in_map(jnp.real(b_bar)), in_map(jnp.imag(b_bar))], axis=2)
    cd = jnp.concatenate([out_map(c_re.astype(f32)), out_map(-c_im.astype(f32))], axis=1)
    ar = jnp.real(lam_bar).reshape(ns, S5_GROUPS_PER_SLAB * p)
    ai = jnp.imag(lam_bar).reshape(ns, S5_GROUPS_PER_SLAB * p)
    return bd.astype(BF16), cd.astype(BF16), ar, ai


def kernel(x, mem, l0_mix_norm, l0_w_in, l0_s5_lam_re, l0_s5_lam_im, l0_s5_log_dt, l0_s5_b_re, l0_s5_b_im, l0_s5_c_re, l0_s5_c_im, l0_s5_d, l0_s5_w_glu, l0_s5_b_glu, l0_mem_norm, l0_w_mem_k, l0_w_mem_v, l0_mem_q_gain, l0_mem_k_gain, l0_w_out, l0_ffn_norm, l0_ffn_w_gate, l0_ffn_w_up, l0_ffn_w_down, kv_norm, kv_w_k, kv_w_v, kv_k_gain, l1_mix_norm, l1_w_in, l1_moba_q_gain, l1_mem_norm, l1_w_mem_k, l1_w_mem_v, l1_mem_q_gain, l1_mem_k_gain, l1_w_out, l1_ffn_norm, l1_moe_router, l1_moe_w_gate, l1_moe_w_up, l1_moe_w_down):
    batch, seq, d = x.shape
    t = batch * seq
    main_w = l0_s5_w_glu.shape[0]
    mem_w = l0_w_mem_k.shape[1]
    n_q = main_w // HEAD_DIM
    tm = 512
    row = lambda v: v.astype(F32).reshape(1, -1)

    mem_off = [HEAD_DIM * (h % 2) for h in range(MEM_HEADS)]
    ones_kv = _head_block_ones(mem_w)
    x2d = x.reshape(t, d)

    main_in, qm0_t = _inproj0(
        x2d, row(l0_mix_norm), l0_w_in[:, :main_w].astype(BF16),
        _pad_heads_t(l0_w_in[:, main_w:], mem_off).astype(BF16), batch, seq, tm)
    k0, v0_t = _mem_kv(
        mem, row(l0_mem_norm), l0_w_mem_k.astype(BF16), l0_w_mem_v.T.astype(BF16), ones_kv,
        _tile_gain(l0_mem_k_gain, MEM_HEADS) * _tile_gain(l0_mem_q_gain, MEM_HEADS))

    bd, cd, ar, ai = _s5_params(l0_s5_lam_re, l0_s5_lam_im, l0_s5_log_dt, l0_s5_b_re, l0_s5_b_im,
                                l0_s5_c_re, l0_s5_c_im)
    s5_out = _s5(main_in.reshape(batch, seq, main_w), bd, cd, ar, ai, row(l0_s5_d),
                 l0_s5_w_glu.astype(BF16), row(l0_s5_b_glu), 64).reshape(t, main_w)

    x2 = _outproj0_ffn(s5_out, (qm0_t, k0, v0_t), l0_w_out[:main_w].astype(BF16),
                       l0_w_out[main_w:].astype(BF16), x2d, row(l0_ffn_norm),
                       l0_ffn_w_gate.astype(BF16), l0_ffn_w_up.astype(BF16),
                       l0_ffn_w_down.astype(BF16), seq, tm)

    wq1_t = jnp.concatenate([l1_w_in[:, :main_w].T,
                             _pad_heads_t(l1_w_in[:, main_w:], mem_off)], axis=0).astype(BF16)
    q_t, qm1_t, kq, km, v_t = _proj1(
        x2, row(l1_mix_norm), row(kv_norm), wq1_t, _pad_head_cols(kv_w_k).astype(BF16),
        kv_w_v.T.astype(BF16), _pad_head_cols(_tile_gain(kv_k_gain, MOBA_KV_HEADS)),
        _pad_head_cols(_tile_gain(l1_moba_q_gain, MOBA_KV_HEADS)), batch, seq, tm)
    nb = seq // MOBA_BLOCK
    moba_t = _moba(q_t, kq.reshape(batch, seq, -1), v_t, km.reshape(batch, nb, -1))
    k1, v1_t = _mem_kv(
        mem, row(l1_mem_norm), l1_w_mem_k.astype(BF16), l1_w_mem_v.T.astype(BF16), ones_kv,
        _tile_gain(l1_mem_k_gain, MEM_HEADS) * _tile_gain(l1_mem_q_gain, MEM_HEADS))

    wr_t = jnp.zeros((ROUTER_ROWS, d), F32).at[:N_EXPERTS].set(l1_moe_router.astype(F32).T)
    wr_hi = wr_t.astype(BF16)
    wr_lo = (wr_t - wr_hi.astype(F32)).astype(BF16)
    x3, h3, rec, cnt = _outproj1(moba_t, (qm1_t, k1, v1_t), l1_w_out[:main_w].astype(BF16),
                                 l1_w_out[main_w:].astype(BF16), x2, row(l1_ffn_norm),
                                 wr_hi, wr_lo, seq, tm)

    assert d == ROW_TILE * LANES
    tme = 1024
    counts = cnt[:N_EXPERTS, 0].astype(I32)
    padded = ((counts + tme - 1) // tme) * tme
    ends = jnp.cumsum(padded)
    starts = ends - padded
    max_tiles = (2 * t) // tme + N_EXPERTS
    def lookup(table, idx):
        hit = idx[None, :] == jnp.arange(N_EXPERTS, dtype=I32)[:, None]
        return jnp.sum(jnp.where(hit, table[:, None], 0), axis=0)

    tile_ids = jnp.arange(max_tiles, dtype=I32)
    tile_expert = jnp.minimum(
        jnp.sum((ends // tme)[None, :] <= tile_ids[:, None], axis=1), N_EXPERTS - 1).astype(I32)
    n_valid = jnp.clip(lookup(counts, tile_expert) + lookup(starts, tile_expert)
                       - tile_ids * tme, 0, tme)
    n_used = (ends[-1] // tme).astype(I32).reshape(1)
    d1 = lookup(starts, rec[0].astype(I32)) + rec[2].astype(I32)
    d2 = lookup(starts, rec[1].astype(I32)) + rec[3].astype(I32)
    gates = rec[4:6].T
    def tile_dest(tile):
        return jnp.concatenate([d1.reshape(t // tile, 1, tile), d2.reshape(t // tile, 1, tile)],
                               axis=2)

    tmd, tmc = 1024, 512
    xs = _dispatch(starts + counts, padded - counts, n_used, tile_dest(tmd), h3, max_tiles, tme,
                   tmd)
    ys = _moe_ffn(tile_expert, n_valid.astype(I32), n_used, xs, l1_moe_w_gate, l1_moe_w_up,
                  l1_moe_w_down, tme, 512)
    out = _combine(tile_dest(tmc), ys, x3, gates, tmc)
    return out.reshape(batch, seq, d)
```

```python
import functools

import jax
import jax.numpy as jnp
from jax import lax
from jax.experimental import pallas as pl
from jax.experimental.pallas import tpu as pltpu

F32 = jnp.float32
BF16 = jnp.bfloat16
I32 = jnp.int32

RMS_EPS = 1e-6
NEG_INF = -1e30
HEAD_DIM = 64
HEAD_SLAB = 128
MEM_HEADS = 4
MOBA_KV_HEADS = 4
MOBA_GROUP = 3
MOBA_BLOCK = 256
MOBA_TOP_BLOCKS = 3
MOBA_KV_PER_STEP = 2
S5_GROUP_DIM = 16
S5_STATE = 64
S5_GROUPS_PER_SLAB = HEAD_SLAB // S5_GROUP_DIM
S5_SLAB_STATES = S5_GROUPS_PER_SLAB * S5_STATE
N_EXPERTS = 8
ROUTER_LANES = 128
ROUTER_ROWS = 16
SUBLANES = 8
LANES = 128
ROW_TILE = SUBLANES
DMA_UNROLL = 8
LOG2_E = 1.4426950408889634
V_ROWS = HEAD_DIM + 16

VMEM_LIMIT_BYTES = 56 * 1024 * 1024

TOKEN_TILE = 512
S5_CHUNK = 64
MOE_TILE = 1024
MOE_FF_TILE = 512
DISPATCH_TILE = 1024
COMBINE_TILE = 512

_NT = (((1,), (1,)), ((), ()))
_TN = (((0,), (0,)), ((), ()))


def _cparams(sem):
    return pltpu.CompilerParams(dimension_semantics=sem, vmem_limit_bytes=VMEM_LIMIT_BYTES)


def _rms(x, g):
    return x * lax.rsqrt(jnp.mean(x * x, axis=-1, keepdims=True) + RMS_EPS) * g


def _store_vt_with_ones(vt_ref, vt, n_heads):
    n = vt.shape[1]
    ones_row = jnp.where(lax.broadcasted_iota(I32, (V_ROWS - HEAD_DIM, n), 0) == 0, 1.0, 0.0)
    for h in range(n_heads):
        vt_ref[0, V_ROWS * h:V_ROWS * h + HEAD_DIM, :] = (
            vt[HEAD_DIM * h:HEAD_DIM * (h + 1), :].astype(BF16))
        vt_ref[0, V_ROWS * h + HEAD_DIM:V_ROWS * (h + 1), :] = ones_row.astype(BF16)


def _slab_norm_t(qt, n_heads, scale):
    outs = []
    for h in range(n_heads):
        s = qt[HEAD_SLAB * h:HEAD_SLAB * (h + 1), :]
        ms = jnp.sum(s * s, axis=0, keepdims=True) * (1.0 / HEAD_DIM)
        outs.append(s * (lax.rsqrt(ms + RMS_EPS) * scale))
    return outs


def _inproj0_kernel(x_ref, g_ref, wm_ref, wqt_ref, main_ref, qt_ref):
    hb = _rms(x_ref[...], g_ref[...]).astype(BF16)
    main_ref[...] = jnp.dot(hb, wm_ref[...], preferred_element_type=F32)
    qt = lax.dot_general(wqt_ref[...], hb, _NT, preferred_element_type=F32)
    for h, s in enumerate(_slab_norm_t(qt, MEM_HEADS, HEAD_DIM ** -0.5 * LOG2_E)):
        qt_ref[0, HEAD_SLAB * h:HEAD_SLAB * (h + 1), :] = s.astype(BF16)


def _inproj0(x2d, g, wm, wqt, batch, seq, tm):
    t, d = x2d.shape
    per_b = seq // tm
    nm = wm.shape[1]
    nq = wqt.shape[0]
    return pl.pallas_call(
        _inproj0_kernel,
        grid=(t // tm,),
        in_specs=[
            pl.BlockSpec((tm, d), lambda i: (i, 0)),
            pl.BlockSpec((1, d), lambda i: (0, 0)),
            pl.BlockSpec((d, nm), lambda i: (0, 0)),
            pl.BlockSpec((nq, d), lambda i: (0, 0)),
        ],
        out_specs=[
            pl.BlockSpec((tm, nm), lambda i: (i, 0)),
            pl.BlockSpec((1, nq, tm), lambda i: (i // per_b, 0, i % per_b)),
        ],
        out_shape=[
            jax.ShapeDtypeStruct((t, nm), F32),
            jax.ShapeDtypeStruct((batch, nq, seq), BF16),
        ],
        compiler_params=_cparams(("parallel",)),
        name="inproj0",
    )(x2d, g, wm, wqt)


def _mem_kv_kernel(mem_ref, g_ref, wk_ref, wvt_ref, ones_ref, kg_ref, k_ref, vt_ref):
    mb = _rms(mem_ref[0], g_ref[...]).astype(BF16)
    k = jnp.dot(mb, wk_ref[...], preferred_element_type=F32)
    ms = jnp.dot(k * k, ones_ref[...], preferred_element_type=F32,
                 precision=lax.Precision.HIGHEST) * (1.0 / HEAD_DIM)
    k_ref[0] = (k * lax.rsqrt(ms + RMS_EPS) * kg_ref[...]).astype(BF16)
    _store_vt_with_ones(vt_ref, lax.dot_general(wvt_ref[...], mb, _NT,
                                                preferred_element_type=F32), MEM_HEADS)


def _mem_kv(mem, g, wk, wvt, ones, kg):
    b, m, d = mem.shape
    w = wk.shape[1]
    v_rows = MEM_HEADS * V_ROWS
    return pl.pallas_call(
        _mem_kv_kernel,
        grid=(b,),
        in_specs=[
            pl.BlockSpec((1, m, d), lambda i: (i, 0, 0)),
            pl.BlockSpec((1, d), lambda i: (0, 0)),
            pl.BlockSpec((d, w), lambda i: (0, 0)),
            pl.BlockSpec((w, d), lambda i: (0, 0)),
            pl.BlockSpec((w, w), lambda i: (0, 0)),
            pl.BlockSpec((1, w), lambda i: (0, 0)),
        ],
        out_specs=[
            pl.BlockSpec((1, m, w), lambda i: (i, 0, 0)),
            pl.BlockSpec((1, v_rows, m), lambda i: (i, 0, 0)),
        ],
        out_shape=[
            jax.ShapeDtypeStruct((b, m, w), BF16),
            jax.ShapeDtypeStruct((b, v_rows, m), BF16),
        ],
        compiler_params=_cparams(("parallel",)),
        name="mem_kv",
    )(mem, g, wk, wvt, ones, kg)


def _mem_scores_t(qt_ref, k_ref):
    scores = []
    for h in range(MEM_HEADS):
        pair = h // 2
        k2 = k_ref[0, :, HEAD_SLAB * pair:HEAD_SLAB * (pair + 1)]
        q = qt_ref[0, HEAD_SLAB * h:HEAD_SLAB * (h + 1), :]
        scores.append(jnp.dot(k2, q, preferred_element_type=F32))
    return scores


def _mem_attend_t(scores, vt_ref):
    outs = []
    for h, s in enumerate(scores):
        p = jnp.exp2(s - jnp.max(s, axis=0, keepdims=True)).astype(BF16)
        v = vt_ref[0, V_ROWS * h:V_ROWS * (h + 1), :]
        acc = jnp.dot(v, p, preferred_element_type=F32)
        outs.append((acc[0:HEAD_DIM, :] / acc[HEAD_DIM:HEAD_DIM + 1, :]).astype(BF16))
    return jnp.concatenate(outs, axis=0)


def _mem_attn_specs(qt, k, vt, per_b, tm):
    return [
        pl.BlockSpec((1, qt.shape[1], tm), lambda i: (i // per_b, 0, i % per_b)),
        pl.BlockSpec((1,) + k.shape[1:], lambda i: (i // per_b, 0, 0)),
        pl.BlockSpec((1,) + vt.shape[1:], lambda i: (i // per_b, 0, 0)),
    ]


def _s5_kernel(u_ref, bd_ref, cd_ref, ar_ref, ai_ref, d_ref, wglu_ref, bglu_ref, o_ref,
               buf_ref, st_ref, ut_ref, ot_ref, *, ts, n_slabs):
    half = S5_SLAB_STATES
    width = 2 * half
    batch = u_ref.shape[0]

    @pl.when(pl.program_id(0) == 0)
    def _():
        st_ref[...] = jnp.zeros_like(st_ref)

    for b in range(batch):
        for j in range(n_slabs):
            ut_ref[j, pl.ds(b, ts, stride=batch), :] = u_ref[b, :, HEAD_SLAB * j:HEAD_SLAB * (j + 1)]

    for j in range(n_slabs):
        buf_ref[:, width * j:width * (j + 1)] = jnp.dot(
            ut_ref[j].astype(BF16), bd_ref[j], preferred_element_type=F32)

    for j in range(n_slabs):
        re = slice(width * j, width * j + half)
        im = slice(width * j + half, width * (j + 1))
        ar = jnp.broadcast_to(ar_ref[j:j + 1, :], (SUBLANES, half))
        ai = jnp.broadcast_to(ai_ref[j:j + 1, :], (SUBLANES, half))

        def step(t, carry, re=re, im=im, ar=ar, ai=ai):
            xr, xi = carry
            rows = pl.ds(pl.multiple_of(t * SUBLANES, SUBLANES), SUBLANES)
            nxr = ar * xr - ai * xi + buf_ref[rows, re]
            nxi = ar * xi + ai * xr + buf_ref[rows, im]
            buf_ref[rows, re] = nxr
            buf_ref[rows, im] = nxi
            return nxr, nxi

        xr, xi = lax.fori_loop(0, ts, step, (st_ref[:, re], st_ref[:, im]), unroll=True)
        st_ref[:, re] = xr
        st_ref[:, im] = xi

    ys = []
    for j in range(n_slabs):
        cols = slice(HEAD_SLAB * j, HEAD_SLAB * (j + 1))
        ys.append(jnp.dot(buf_ref[:, width * j:width * (j + 1)].astype(BF16), cd_ref[j],
                          preferred_element_type=F32) + d_ref[:, cols] * ut_ref[j])
    g = jax.nn.gelu(jnp.concatenate(ys, axis=1))
    z = jnp.dot(g.astype(BF16), wglu_ref[...], preferred_element_type=F32) + bglu_ref[...]
    out = g / (1.0 + jnp.exp(-z))
    for j in range(n_slabs):
        ot_ref[j] = out[:, HEAD_SLAB * j:HEAD_SLAB * (j + 1)]
    for b in range(batch):
        for j in range(n_slabs):
            o_ref[b, :, HEAD_SLAB * j:HEAD_SLAB * (j + 1)] = (
                ot_ref[j, pl.ds(b, ts, stride=batch), :].astype(BF16))


def _s5(u, bd, cd, ar, ai, dskip, wglu, bglu, ts):
    batch, seq, c = u.shape
    n_slabs = c // HEAD_SLAB
    width = 2 * S5_SLAB_STATES
    tile = ts * batch
    return pl.pallas_call(
        functools.partial(_s5_kernel, ts=ts, n_slabs=n_slabs),
        grid=(seq // ts,),
        in_specs=[
            pl.BlockSpec((batch, ts, c), lambda i: (0, i, 0)),
            pl.BlockSpec((n_slabs, HEAD_SLAB, width), lambda i: (0, 0, 0)),
            pl.BlockSpec((n_slabs, width, HEAD_SLAB), lambda i: (0, 0, 0)),
            pl.BlockSpec((n_slabs, S5_SLAB_STATES), lambda i: (0, 0)),
            pl.BlockSpec((n_slabs, S5_SLAB_STATES), lambda i: (0, 0)),
            pl.BlockSpec((1, c), lambda i: (0, 0)),
            pl.BlockSpec((c, c), lambda i: (0, 0)),
            pl.BlockSpec((1, c), lambda i: (0, 0)),
        ],
        out_specs=pl.BlockSpec((batch, ts, c), lambda i: (0, i, 0)),
        out_shape=jax.ShapeDtypeStruct((batch, seq, c), BF16),
        scratch_shapes=[
            pltpu.VMEM((tile, n_slabs * width), F32),
            pltpu.VMEM((batch, n_slabs * width), F32),
            pltpu.VMEM((n_slabs, tile, HEAD_SLAB), F32),
            pltpu.VMEM((n_slabs, tile, HEAD_SLAB), F32),
        ],
        compiler_params=_cparams(("arbitrary",)),
        name="s5_mixer",
    )(u, bd, cd, ar, ai, dskip, wglu, bglu)


def _outproj_body(a, a_transposed, mem_refs, wa_ref, wb_ref, x_ref, g_ref):
    qt_ref, mk_ref, mvt_ref = mem_refs
    dn = _TN if a_transposed else (((1,), (0,)), ((), ()))
    scores = _mem_scores_t(qt_ref, mk_ref)
    y = lax.dot_general(a, wa_ref[...], dn, preferred_element_type=F32)
    y = y + lax.dot_general(_mem_attend_t(scores, mvt_ref), wb_ref[...], _TN,
                            preferred_element_type=F32)
    xn = x_ref[...] + y
    return xn, _rms(xn, g_ref[...])


def _layer0_tail_kernel(a_ref, qt_ref, mk_ref, mvt_ref, wa_ref, wb_ref, x_ref, g_ref,
                        wg_ref, wu_ref, wd_ref, *rest, tm, n_q, per_b):
    proj1_in, (o_ref, *proj1_out, h_ref) = rest[:7], rest[7:]
    xn, h = _outproj_body(a_ref[...], False, (qt_ref, mk_ref, mvt_ref), wa_ref, wb_ref, x_ref,
                          g_ref)
    h_ref[...] = h.astype(BF16)
    half = h_ref.shape[0] // 2
    halves = (slice(0, half), slice(half, 2 * half))
    mids = []
    for rows in halves:
        hb = h_ref[rows, :]
        gt = jnp.dot(hb, wg_ref[...], preferred_element_type=F32)
        up = jnp.dot(hb, wu_ref[...], preferred_element_type=F32)
        mids.append(((gt / (1.0 + jnp.exp(-gt))) * up).astype(BF16))
    for rows, mid in zip(halves, mids):
        o_ref[rows, :] = xn[rows, :] + jnp.dot(mid, wd_ref[...], preferred_element_type=F32)
    _proj1_body(o_ref[...], *proj1_in, *proj1_out, tm=tm, n_q=n_q, per_b=per_b)


def _layer0_tail(a, mem_qkv, wa, wb, x2d, g, wg, wu, wd, g1, g2, wqt, wk, wvt, kg, qg,
                 batch, seq, tm):
    t, d = x2d.shape
    per_b = seq // tm
    ka, kb = a.shape[1], wb.shape[0]
    ff = wg.shape[1]
    nq_rows = wqt.shape[0]
    n_q = (nq_rows - MEM_HEADS * HEAD_SLAB) // HEAD_DIM
    kw = wk.shape[1]
    vw = wvt.shape[0]
    v_rows = MOBA_KV_HEADS * V_ROWS
    nblk = tm // MOBA_BLOCK
    resident = dict(pipeline_mode=pl.Buffered(1))
    return pl.pallas_call(
        functools.partial(_layer0_tail_kernel, tm=tm, n_q=n_q, per_b=per_b),
        grid=(t // tm,),
        in_specs=[
            pl.BlockSpec((tm, ka), lambda i: (i, 0)),
            *_mem_attn_specs(*mem_qkv, per_b, tm),
            pl.BlockSpec((ka, d), lambda i: (0, 0), **resident),
            pl.BlockSpec((kb, d), lambda i: (0, 0), **resident),
            pl.BlockSpec((tm, d), lambda i: (i, 0)),
            pl.BlockSpec((1, d), lambda i: (0, 0)),
            pl.BlockSpec((d, ff), lambda i: (0, 0), **resident),
            pl.BlockSpec((d, ff), lambda i: (0, 0), **resident),
            pl.BlockSpec((ff, d), lambda i: (0, 0), **resident),
            pl.BlockSpec((1, d), lambda i: (0, 0)),
            pl.BlockSpec((1, d), lambda i: (0, 0)),
            pl.BlockSpec((nq_rows, d), lambda i: (0, 0), **resident),
            pl.BlockSpec((d, kw), lambda i: (0, 0), **resident),
            pl.BlockSpec((vw, d), lambda i: (0, 0), **resident),
            pl.BlockSpec((1, kw), lambda i: (0, 0)),
            pl.BlockSpec((1, kw), lambda i: (0, 0)),
        ],
        out_specs=[
            pl.BlockSpec((tm, d), lambda i: (i, 0)),
            pl.BlockSpec((1, n_q * HEAD_DIM, tm), lambda i: (i // per_b, 0, i % per_b)),
            pl.BlockSpec((1, MEM_HEADS * HEAD_SLAB, tm), lambda i: (i // per_b, 0, i % per_b)),
            pl.BlockSpec((tm, kw), lambda i: (i, 0)),
            pl.BlockSpec((nblk, 1, kw), lambda i: (i, 0, 0)),
            pl.BlockSpec((1, v_rows, tm), lambda i: (i // per_b, 0, i % per_b)),
        ],
        out_shape=[
            jax.ShapeDtypeStruct((t, d), F32),
            jax.ShapeDtypeStruct((batch, n_q * HEAD_DIM, seq), BF16),
            jax.ShapeDtypeStruct((batch, MEM_HEADS * HEAD_SLAB, seq), BF16),
            jax.ShapeDtypeStruct((t, kw), BF16),
            jax.ShapeDtypeStruct((t // MOBA_BLOCK, 1, kw), F32),
            jax.ShapeDtypeStruct((batch, v_rows, seq), BF16),
        ],
        scratch_shapes=[pltpu.VMEM((tm, d), BF16)],
        compiler_params=_cparams(("parallel",)),
        name="layer0_tail",
    )(a, *mem_qkv, wa, wb, x2d, g, wg, wu, wd, g1, g2, wqt, wk, wvt, kg, qg)


def _outproj1_kernel(at_ref, qt_ref, mk_ref, mvt_ref, wa_ref, wb_ref, x_ref, g_ref, wrh_ref,
                     wrl_ref, xo_ref, h_ref, rec_ref, cnt_ref, run_ref, *, tm):
    i = pl.program_id(0)

    @pl.when(i == 0)
    def _():
        run_ref[...] = jnp.zeros_like(run_ref)

    xn, h = _outproj_body(at_ref[0], True, (qt_ref, mk_ref, mvt_ref), wa_ref, wb_ref, x_ref,
                          g_ref)
    xo_ref[...] = xn
    _to_row_tiles(h_ref, h)

    h_hi = h.astype(BF16)
    h_lo = (h - h_hi.astype(F32)).astype(BF16)
    logits = (lax.dot_general(wrh_ref[...], h_hi, _NT, preferred_element_type=F32)
              + lax.dot_general(wrh_ref[...], h_lo, _NT, preferred_element_type=F32)
              + lax.dot_general(wrl_ref[...], h_hi, _NT, preferred_element_type=F32))
    ex = lax.broadcasted_iota(I32, (ROUTER_ROWS, tm), 0)
    logits = jnp.where(ex < N_EXPERTS, logits, -jnp.inf)
    m1 = jnp.max(logits, axis=0, keepdims=True)
    e1 = jnp.min(jnp.where(logits == m1, ex, ROUTER_ROWS), axis=0, keepdims=True)
    rest = jnp.where(ex == e1, -jnp.inf, logits)
    m2 = jnp.max(rest, axis=0, keepdims=True)
    e2 = jnp.min(jnp.where(rest == m2, ex, ROUTER_ROWS), axis=0, keepdims=True)
    w2 = jnp.exp(m2 - m1)
    g1 = 1.0 / (1.0 + w2)
    g2 = w2 / (1.0 + w2)

    hot = jnp.where((ex == e1) | (ex == e2), 1.0, 0.0)
    r_io = lax.broadcasted_iota(I32, (tm, tm), 0)
    c_io = lax.broadcasted_iota(I32, (tm, tm), 1)
    triu = jnp.where(r_io < c_io, 1.0, 0.0).astype(BF16)
    before = (jnp.dot(hot.astype(BF16), triu, preferred_element_type=F32)
              + run_ref[:, 0:1])
    r1 = jnp.sum(jnp.where(ex == e1, before, 0.0), axis=0, keepdims=True)
    r2 = jnp.sum(jnp.where(ex == e2, before, 0.0), axis=0, keepdims=True)
    run = run_ref[...] + jnp.sum(hot, axis=1, keepdims=True)
    run_ref[...] = run
    cnt_ref[...] = run

    slot = lax.broadcasted_iota(I32, (SUBLANES, tm), 0)
    rec = jnp.where(slot == 0, e1.astype(F32), 0.0)
    rec = jnp.where(slot == 1, e2.astype(F32), rec)
    rec = jnp.where(slot == 2, r1, rec)
    rec = jnp.where(slot == 3, r2, rec)
    rec = jnp.where(slot == 4, g1, rec)
    rec = jnp.where(slot == 5, g2, rec)
    rec_ref[...] = rec


def _outproj1(at, mem_qkv, wa, wb, x2d, g, wr_hi, wr_lo, seq, tm):
    t, d = x2d.shape
    per_b = seq // tm
    ka, kb = at.shape[1], wb.shape[0]
    return pl.pallas_call(
        functools.partial(_outproj1_kernel, tm=tm),
        grid=(t // tm,),
        in_specs=[
            pl.BlockSpec((1, ka, tm), lambda i: (i // per_b, 0, i % per_b)),
            *_mem_attn_specs(*mem_qkv, per_b, tm),
            pl.BlockSpec((ka, d), lambda i: (0, 0)),
            pl.BlockSpec((kb, d), lambda i: (0, 0)),
            pl.BlockSpec((tm, d), lambda i: (i, 0)),
            pl.BlockSpec((1, d), lambda i: (0, 0)),
            pl.BlockSpec((ROUTER_ROWS, d), lambda i: (0, 0)),
            pl.BlockSpec((ROUTER_ROWS, d), lambda i: (0, 0)),
        ],
        out_specs=[
            pl.BlockSpec((tm, d), lambda i: (i, 0)),
            pl.BlockSpec((tm * ROW_TILE, LANES), lambda i: (i, 0)),
            pl.BlockSpec((SUBLANES, tm), lambda i: (0, i)),
            pl.BlockSpec((ROUTER_ROWS, ROUTER_LANES), lambda i: (0, 0)),
        ],
        out_shape=[
            jax.ShapeDtypeStruct((t, d), F32),
            jax.ShapeDtypeStruct((t * ROW_TILE, LANES), F32),
            jax.ShapeDtypeStruct((SUBLANES, t), F32),
            jax.ShapeDtypeStruct((ROUTER_ROWS, ROUTER_LANES), F32),
        ],
        scratch_shapes=[pltpu.VMEM((ROUTER_ROWS, ROUTER_LANES), F32)],
        compiler_params=_cparams(("arbitrary",)),
        name="outproj1_router",
    )(at, *mem_qkv, wa, wb, x2d, g, wr_hi, wr_lo)


def _from_row_tiles(ref, base, n_rows, n_feat):
    parts = [ref[pl.ds(base + k, n_rows, stride=ROW_TILE), :] for k in range(n_feat // LANES)]
    return jnp.concatenate(parts, axis=1)


def _to_row_tiles(ref, value):
    for k in range(value.shape[1] // LANES):
        ref[pl.ds(k, value.shape[0], stride=ROW_TILE), :] = value[:, LANES * k:LANES * (k + 1)]


def _moe_ffn_kernel(te_ref, nv_ref, nu_ref, xs_ref, wg_ref, wu_ref, wd_ref, o_ref,
                    acc_ref, xb_ref, *, tm):
    del te_ref
    i = pl.program_id(0)
    f = pl.program_id(1)
    nf = pl.num_programs(1)
    half = tm // 2
    d = xb_ref.shape[1]

    def partial_out(row_sets):
        wg = wg_ref[0].astype(BF16)
        wu = wu_ref[0].astype(BF16)
        wd = wd_ref[0].astype(BF16)
        mids = []
        for rows in row_sets:
            xb = xb_ref[rows, :]
            gt = jnp.dot(xb, wg, preferred_element_type=F32)
            up = jnp.dot(xb, wu, preferred_element_type=F32)
            mids.append(((gt / (1.0 + jnp.exp(-gt))) * up).astype(BF16))
        for rows, mid in zip(row_sets, mids):
            acc_ref[rows, :] += jnp.dot(mid, wd, preferred_element_type=F32)

    @pl.when(i < nu_ref[0])
    def _():
        @pl.when(f == 0)
        def _():
            acc_ref[...] = jnp.zeros_like(acc_ref)
            xb_ref[...] = _from_row_tiles(xs_ref, 0, tm, d).astype(BF16)

        quarter = tm // 4
        nv = nv_ref[i]

        @pl.when(nv > 3 * quarter)
        def _():
            partial_out([slice(0, half), slice(half, tm)])

        @pl.when((nv > half) & (nv <= 3 * quarter))
        def _():
            partial_out([slice(0, half), slice(half, 3 * quarter)])

        @pl.when((nv > quarter) & (nv <= half))
        def _():
            partial_out([slice(0, half)])

        @pl.when(nv <= quarter)
        def _():
            partial_out([slice(0, quarter)])

        @pl.when(f == nf - 1)
        def _():
            _to_row_tiles(o_ref, acc_ref[...])

    @pl.when((i >= nu_ref[0]) & (f == nf - 1))
    def _():
        o_ref[...] = jnp.zeros_like(o_ref)


def _moe_ffn(tile_expert, n_valid, n_used, xs_rt, wg, wu, wd, tm, tf):
    d = wg.shape[1]
    ff = wg.shape[2]
    nf = ff // tf
    n_tiles = xs_rt.shape[0] // (tm * ROW_TILE)

    def live(i, nu):
        return jnp.minimum(i, jnp.maximum(nu[0] - 1, 0))

    def f_of(i, f, nu):
        return jnp.where(i < nu[0], f, nf - 1)

    def wcol_map(i, f, te, nv, nu):
        return (te[live(i, nu)], 0, f_of(i, f, nu))

    def wrow_map(i, f, te, nv, nu):
        return (te[live(i, nu)], f_of(i, f, nu), 0)

    grid_spec = pltpu.PrefetchScalarGridSpec(
        num_scalar_prefetch=3,
        grid=(n_tiles, nf),
        in_specs=[
            pl.BlockSpec((tm * ROW_TILE, LANES), lambda i, f, te, nv, nu: (live(i, nu), 0)),
            pl.BlockSpec((1, d, tf), wcol_map),
            pl.BlockSpec((1, d, tf), wcol_map),
            pl.BlockSpec((1, tf, d), wrow_map),
        ],
        out_specs=pl.BlockSpec((tm * ROW_TILE, LANES), lambda i, f, te, nv, nu: (i, 0)),
        scratch_shapes=[pltpu.VMEM((tm, d), F32), pltpu.VMEM((tm, d), BF16)],
    )
    return pl.pallas_call(
        functools.partial(_moe_ffn_kernel, tm=tm),
        grid_spec=grid_spec,
        out_shape=jax.ShapeDtypeStruct(xs_rt.shape, F32),
        compiler_params=_cparams(("arbitrary", "arbitrary")),
        name="ffn_moe",
    )(tile_expert, n_valid, n_used, xs_rt, wg, wu, wd)


def _proj1_body(x, g1_ref, g2_ref, wqt_ref, wk_ref, wvt_ref, kg_ref, qg_ref,
                qt_ref, qmt_ref, k_ref, km_ref, vt_ref, *, tm, n_q, per_b):
    h1 = _rms(x, g1_ref[...]).astype(BF16)
    h2 = _rms(x, g2_ref[...]).astype(BF16)
    qt = lax.dot_general(wqt_ref[...], h1, _NT, preferred_element_type=F32)

    for h in range(n_q):
        s = qt[HEAD_DIM * h:HEAD_DIM * (h + 1), :]
        ms = jnp.mean(s * s, axis=0, keepdims=True)
        qt_ref[0, HEAD_DIM * h:HEAD_DIM * (h + 1), :] = (
            s * (lax.rsqrt(ms + RMS_EPS) * (HEAD_DIM ** -0.5 * LOG2_E))).astype(BF16)
    qm = qt[HEAD_DIM * n_q:, :]
    for h, s in enumerate(_slab_norm_t(qm, MEM_HEADS, HEAD_DIM ** -0.5 * LOG2_E)):
        qmt_ref[0, HEAD_SLAB * h:HEAD_SLAB * (h + 1), :] = s.astype(BF16)

    k = jnp.dot(h2, wk_ref[...], preferred_element_type=F32)
    nblk = tm // MOBA_BLOCK
    first_blk = lax.rem(pl.program_id(0), per_b) * nblk
    lane = lax.broadcasted_iota(I32, (MOBA_BLOCK, HEAD_SLAB), 1)
    for kv in range(MOBA_KV_HEADS):
        cols = slice(HEAD_SLAB * kv, HEAD_SLAB * (kv + 1))
        ks = k[:, cols]
        ms = jnp.sum(ks * ks, axis=-1, keepdims=True) * (1.0 / HEAD_DIM)
        kn = ks * lax.rsqrt(ms + RMS_EPS) * kg_ref[:, cols]
        kq = kn * qg_ref[:, cols]
        for j in range(nblk):
            rows = slice(MOBA_BLOCK * j, MOBA_BLOCK * (j + 1))
            km_ref[j, :, cols] = jnp.mean(kn[rows], axis=0, keepdims=True) * qg_ref[:, cols]
            k_ref[rows, cols] = jnp.where(lane == HEAD_DIM + first_blk + j, 1.0,
                                          kq[rows]).astype(BF16)

    _store_vt_with_ones(vt_ref, lax.dot_general(wvt_ref[...], h2, _NT,
                                                preferred_element_type=F32), MOBA_KV_HEADS)


def _moba_kernel(qt_ref, qtn_ref, k_ref, vt_ref, km_ref, o_ref, s_ref, bias_ref):
    qb = pl.program_id(2)
    nb = km_ref.shape[1]
    bq = MOBA_BLOCK
    nq = MOBA_GROUP * bq
    q_rows = MOBA_GROUP * HEAD_DIM
    blk = lax.broadcasted_iota(I32, (nb, nq), 0)

    def queries(ref, kv):
        return jnp.concatenate(
            [ref[0, q_rows * kv + HEAD_DIM * g:q_rows * kv + HEAD_DIM * (g + 1), :]
             for g in range(MOBA_GROUP)], axis=1)

    def block_bias(q64, kv, q_blk):
        km = km_ref[0, :, HEAD_SLAB * kv:HEAD_SLAB * kv + HEAD_DIM]
        km_hi = km.astype(BF16)
        km_lo = (km - km_hi.astype(F32)).astype(BF16)
        gate = (jnp.dot(km_hi, q64, preferred_element_type=F32)
                + jnp.dot(km_lo, q64, preferred_element_type=F32))
        gate = jnp.where(blk < q_blk, gate, NEG_INF)
        rank = jnp.zeros((nb, nq), F32)
        for m in range(nb):
            gm = gate[m:m + 1, :]
            rank = rank + jnp.where(gm > gate, 1.0,
                                    jnp.where((gm == gate) & (blk > m), 1.0, 0.0))
        dropped = (blk < q_blk) & (rank >= MOBA_TOP_BLOCKS)
        bias = jnp.where(dropped, NEG_INF, 0.0)
        return jnp.concatenate([bias, jnp.zeros((nb, nq), F32)], axis=0).astype(BF16)

    key_pos = lax.broadcasted_iota(I32, (bq, nq), 0)
    qry_pos = lax.broadcasted_iota(I32, (bq, nq), 1) & (bq - 1)
    causal = key_pos <= qry_pos

    for n_blocks in range(1, nb + 1):
        @pl.when(qb == n_blocks - 1)
        def _(n_blocks=n_blocks):
            mx = []
            for kv in range(MOBA_KV_PER_STEP):
                bias = bias_ref[kv] if n_blocks > 1 else jnp.zeros((2 * nb, nq), BF16)
                q3 = jnp.concatenate(
                    [queries(qt_ref, kv), bias,
                     jnp.zeros((HEAD_SLAB - HEAD_DIM - 2 * nb, nq), BF16)], axis=0)
                m = None
                for n in range(n_blocks):
                    s = jnp.dot(k_ref[0, bq * n:bq * (n + 1), HEAD_SLAB * kv:HEAD_SLAB * (kv + 1)],
                                q3, preferred_element_type=F32)
                    if n == n_blocks - 1:
                        s = jnp.where(causal, s, NEG_INF)
                    s_ref[kv, n] = s
                    cm = jnp.max(s, axis=0, keepdims=True)
                    m = cm if m is None else jnp.maximum(m, cm)
                mx.append(m)
            if n_blocks < nb:
                for kv in range(MOBA_KV_PER_STEP):
                    bias_ref[kv] = block_bias(queries(qtn_ref, kv), kv, n_blocks)
            for kv in range(MOBA_KV_PER_STEP):
                acc = jnp.zeros((V_ROWS, nq), F32)
                for n in range(n_blocks):
                    p = jnp.exp2(s_ref[kv, n] - mx[kv]).astype(BF16)
                    acc = acc + jnp.dot(vt_ref[0, V_ROWS * kv:V_ROWS * (kv + 1),
                                               bq * n:bq * (n + 1)], p,
                                        preferred_element_type=F32)
                out = acc[0:HEAD_DIM, :] / acc[HEAD_DIM:HEAD_DIM + 1, :]
                for g in range(MOBA_GROUP):
                    r0 = q_rows * kv + HEAD_DIM * g
                    o_ref[0, r0:r0 + HEAD_DIM, :] = out[:, bq * g:bq * (g + 1)].astype(BF16)


def _moba(qt, k3, vt, km3):
    b, nq_rows, s = qt.shape
    nb = s // MOBA_BLOCK
    per = MOBA_KV_PER_STEP
    rows = per * MOBA_GROUP * HEAD_DIM
    return pl.pallas_call(
        _moba_kernel,
        grid=(b, MOBA_KV_HEADS // per, nb),
        in_specs=[
            pl.BlockSpec((1, rows, MOBA_BLOCK), lambda i, j, q: (i, j, q)),
            pl.BlockSpec((1, rows, MOBA_BLOCK), lambda i, j, q: (i, j, jnp.minimum(q + 1, nb - 1))),
            pl.BlockSpec((1, s, per * HEAD_SLAB), lambda i, j, q: (i, 0, j)),
            pl.BlockSpec((1, per * V_ROWS, s), lambda i, j, q: (i, j, 0)),
            pl.BlockSpec((1, nb, per * HEAD_SLAB), lambda i, j, q: (i, 0, j)),
        ],
        out_specs=pl.BlockSpec((1, rows, MOBA_BLOCK), lambda i, j, q: (i, j, q)),
        out_shape=jax.ShapeDtypeStruct((b, nq_rows, s), BF16),
        scratch_shapes=[pltpu.VMEM((per, nb, MOBA_BLOCK, MOBA_GROUP * MOBA_BLOCK), F32),
                        pltpu.VMEM((per, 2 * nb, MOBA_GROUP * MOBA_BLOCK), BF16)],
        compiler_params=_cparams(("parallel", "parallel", "arbitrary")),
        name="moba_attn",
    )(qt, qt, k3, vt, km3)


def _row_copy(src_ref, src_row, dst_ref, dst_row, sem):
    def tile(ref, r):
        return ref.at[pl.ds(pl.multiple_of(r * ROW_TILE, ROW_TILE), ROW_TILE)]

    return pltpu.make_async_copy(tile(src_ref, src_row), tile(dst_ref, dst_row), sem)


def _for_each_row(n_rows, fn):
    def body(j8, carry):
        for u in range(DMA_UNROLL):
            fn(j8 * DMA_UNROLL + u, u % 2)
        return carry

    lax.fori_loop(0, n_rows // DMA_UNROLL, body, 0)


def _dispatch_kernel(pad_lo_ref, pad_n_ref, nu_ref, dest_ref, h_ref, xs_ref, sem, pad_sem, *,
                     tm, pad_bits, tile_rows, n_tiles):
    @pl.when(pl.program_id(0) == 0)
    def _():
        def fill_tile(j, carry):
            cp = pltpu.make_async_copy(
                h_ref.at[pl.ds(0, tile_rows * ROW_TILE)],
                xs_ref.at[pl.ds(pl.multiple_of(j * tile_rows * ROW_TILE, ROW_TILE),
                                tile_rows * ROW_TILE)], pad_sem)
            cp.start()
            cp.wait()
            return carry

        lax.fori_loop(nu_ref[0], n_tiles, fill_tile, 0)

        for start in (True, False):
            for e in range(N_EXPERTS):
                n = pad_n_ref[e]
                for bit in reversed(range(pad_bits)):
                    size = 1 << bit

                    @pl.when((n & size) != 0)
                    def _(n=n, e=e, bit=bit, size=size, start=start):
                        first = pad_lo_ref[e] + ((n >> (bit + 1)) << (bit + 1))
                        cp = pltpu.make_async_copy(
                            h_ref.at[pl.ds(0, size * ROW_TILE)],
                            xs_ref.at[pl.ds(pl.multiple_of(first * ROW_TILE, ROW_TILE),
                                            size * ROW_TILE)], pad_sem)
                        cp.start() if start else cp.wait()

    def copy(j, slot):
        return _row_copy(h_ref, j, xs_ref, dest_ref[0, 0, slot * tm + j], sem)

    for slot in range(2):
        _for_each_row(tm, lambda j, lane, slot=slot: copy(j, slot).start(priority=lane))
    for slot in range(2):
        _for_each_row(tm, lambda j, lane, slot=slot: copy(j, slot).wait())


def _dispatch(pad_lo, pad_n, n_used, dest3, h_rt, n_tiles, tile_rows, tm):
    assert tile_rows <= tm
    grid_spec = pltpu.PrefetchScalarGridSpec(
        num_scalar_prefetch=3,
        grid=(h_rt.shape[0] // (tm * ROW_TILE),),
        in_specs=[
            pl.BlockSpec((1, 1, 2 * tm), lambda i, lo, n, nu: (i, 0, 0), memory_space=pltpu.SMEM),
            pl.BlockSpec((tm * ROW_TILE, LANES), lambda i, lo, n, nu: (i, 0)),
        ],
        out_specs=pl.BlockSpec(memory_space=pl.ANY),
        scratch_shapes=[pltpu.SemaphoreType.DMA(()), pltpu.SemaphoreType.DMA(())],
    )
    return pl.pallas_call(
        functools.partial(_dispatch_kernel, tm=tm, pad_bits=(tile_rows - 1).bit_length(),
                          tile_rows=tile_rows, n_tiles=n_tiles),
        grid_spec=grid_spec,
        out_shape=jax.ShapeDtypeStruct((n_tiles * tile_rows * ROW_TILE, LANES), F32),
        compiler_params=_cparams(("arbitrary",)),
        name="moe_dispatch",
    )(pad_lo, pad_n, n_used, dest3, h_rt)


def _combine_kernel(dest_ref, next_ref, ys_ref, x_ref, gate_ref, o_ref, buf_ref, sem, *, tm):
    i = pl.program_id(0)
    cur = lax.rem(i, 2)

    def copy(idx_ref, buf_slot, j):
        return _row_copy(ys_ref, idx_ref[0, 0, j], buf_ref.at[buf_slot], j, sem.at[buf_slot])

    @pl.when(i == 0)
    def _():
        _for_each_row(2 * tm, lambda j, lane: copy(dest_ref, 0, j).start(priority=lane))

    @pl.when(i + 1 < pl.num_programs(0))
    def _():
        _for_each_row(2 * tm, lambda j, lane: copy(next_ref, 1 - cur, j).start(priority=lane))

    _for_each_row(2 * tm, lambda j, lane: copy(dest_ref, cur, j).wait())
    g1 = gate_ref[:, 0:1]
    g2 = gate_ref[:, 1:2]
    d = x_ref.shape[1]
    for slot in range(2):
        @pl.when(cur == slot)
        def _(slot=slot):
            rows = buf_ref.at[slot]
            y1 = _from_row_tiles(rows, 0, tm, d)
            y2 = _from_row_tiles(rows, tm * ROW_TILE, tm, d)
            o_ref[...] = x_ref[...] + (g1 * y1 + g2 * y2)


def _combine(dest3, ys, x2d, gates, tm):
    t, d = x2d.shape
    n = t // tm
    return pl.pallas_call(
        functools.partial(_combine_kernel, tm=tm),
        grid=(n,),
        in_specs=[
            pl.BlockSpec((1, 1, 2 * tm), lambda i: (i, 0, 0), memory_space=pltpu.SMEM),
            pl.BlockSpec((1, 1, 2 * tm), lambda i: (jnp.minimum(i + 1, n - 1), 0, 0),
                         memory_space=pltpu.SMEM),
            pl.BlockSpec(memory_space=pl.ANY),
            pl.BlockSpec((tm, d), lambda i: (i, 0)),
            pl.BlockSpec((tm, gates.shape[1]), lambda i: (i, 0)),
        ],
        out_specs=pl.BlockSpec((tm, d), lambda i: (i, 0)),
        out_shape=jax.ShapeDtypeStruct((t, d), F32),
        scratch_shapes=[pltpu.VMEM((2, 2 * tm * ROW_TILE, LANES), F32),
                        pltpu.SemaphoreType.DMA((2,))],
        compiler_params=_cparams(("arbitrary",)),
        name="moe_combine",
    )(dest3, dest3, ys, x2d, gates)


def _pad_heads_t(w_cols, offsets):
    d = w_cols.shape[0]
    nh = w_cols.shape[1] // HEAD_DIM
    wt = w_cols.T.reshape(nh, HEAD_DIM, d)
    hi = jnp.asarray(offsets, I32).reshape(nh, 1, 1) > 0
    z = jnp.zeros_like(wt)
    slab = jnp.concatenate([jnp.where(hi, z, wt), jnp.where(hi, wt, z)], axis=1)
    return slab.reshape(nh * HEAD_SLAB, d)


def _pad_head_cols(w):
    rows = w.shape[0]
    nh = w.shape[1] // HEAD_DIM
    w3 = w.reshape(rows, nh, HEAD_DIM)
    return jnp.concatenate([w3, jnp.zeros_like(w3)], axis=2).reshape(rows, nh * HEAD_SLAB)


def _head_block_ones(width):
    idx = jnp.arange(width) // HEAD_DIM
    return (idx[:, None] == idx[None, :]).astype(F32)


def _tile_gain(g, n_heads):
    return jnp.tile(g.astype(F32), n_heads).reshape(1, n_heads * HEAD_DIM)


def _s5_params(lam_re, lam_im, log_dt, b_re, b_im, c_re, c_im):
    f32 = F32
    g, p = lam_re.shape
    n = b_re.shape[2]
    lam = lax.complex(lam_re.astype(f32), lam_im.astype(f32))
    dt = jnp.exp(log_dt.astype(f32))[:, None]
    lam_bar = jnp.exp(lam * dt)
    b_bar = ((lam_bar - 1.0) / lam)[..., None] * lax.complex(b_re.astype(f32), b_im.astype(f32))
    ns = g // S5_GROUPS_PER_SLAB
    eye = jnp.eye(S5_GROUPS_PER_SLAB, dtype=f32)

    def in_map(part):
        blk = part.reshape(ns, S5_GROUPS_PER_SLAB, p, n).transpose(0, 1, 3, 2)
        return jnp.einsum('sgnp,gh->sgnhp', blk, eye).reshape(
            ns, S5_GROUPS_PER_SLAB * n, S5_GROUPS_PER_SLAB * p)

    def out_map(part):
        blk = part.reshape(ns, S5_GROUPS_PER_SLAB, n, p).transpose(0, 1, 3, 2)
        return jnp.einsum('sgpn,gh->sgphn', blk, eye).reshape(
            ns, S5_GROUPS_PER_SLAB * p, S5_GROUPS_PER_SLAB * n)

    bd = jnp.concatenate([in_map(jnp.real(b_bar)), in_map(jnp.imag(b_bar))], axis=2)
    cd = jnp.concatenate([out_map(c_re.astype(f32)), out_map(-c_im.astype(f32))], axis=1)
    ar = jnp.real(lam_bar).reshape(ns, S5_GROUPS_PER_SLAB * p)
    ai = jnp.imag(lam_bar).reshape(ns, S5_GROUPS_PER_SLAB * p)
    return bd.astype(BF16), cd.astype(BF16), ar, ai


def kernel(x, mem, l0_mix_norm, l0_w_in, l0_s5_lam_re, l0_s5_lam_im, l0_s5_log_dt, l0_s5_b_re, l0_s5_b_im, l0_s5_c_re, l0_s5_c_im, l0_s5_d, l0_s5_w_glu, l0_s5_b_glu, l0_mem_norm, l0_w_mem_k, l0_w_mem_v, l0_mem_q_gain, l0_mem_k_gain, l0_w_out, l0_ffn_norm, l0_ffn_w_gate, l0_ffn_w_up, l0_ffn_w_down, kv_norm, kv_w_k, kv_w_v, kv_k_gain, l1_mix_norm, l1_w_in, l1_moba_q_gain, l1_mem_norm, l1_w_mem_k, l1_w_mem_v, l1_mem_q_gain, l1_mem_k_gain, l1_w_out, l1_ffn_norm, l1_moe_router, l1_moe_w_gate, l1_moe_w_up, l1_moe_w_down):
    batch, seq, d = x.shape
    t = batch * seq
    main_w = l0_s5_w_glu.shape[0]
    mem_w = l0_w_mem_k.shape[1]
    tm = TOKEN_TILE
    row = lambda v: v.astype(F32).reshape(1, -1)

    mem_off = [HEAD_DIM * (h % 2) for h in range(MEM_HEADS)]
    ones_kv = _head_block_ones(mem_w)
    x2d = x.reshape(t, d)

    main_in, qm0_t = _inproj0(
        x2d, row(l0_mix_norm), l0_w_in[:, :main_w].astype(BF16),
        _pad_heads_t(l0_w_in[:, main_w:], mem_off).astype(BF16), batch, seq, tm)
    k0, v0_t = _mem_kv(
        mem, row(l0_mem_norm), l0_w_mem_k.astype(BF16), l0_w_mem_v.T.astype(BF16), ones_kv,
        _tile_gain(l0_mem_k_gain, MEM_HEADS) * _tile_gain(l0_mem_q_gain, MEM_HEADS))

    bd, cd, ar, ai = _s5_params(l0_s5_lam_re, l0_s5_lam_im, l0_s5_log_dt, l0_s5_b_re, l0_s5_b_im,
                                l0_s5_c_re, l0_s5_c_im)
    s5_out = _s5(main_in.reshape(batch, seq, main_w), bd, cd, ar, ai, row(l0_s5_d),
                 l0_s5_w_glu.astype(BF16), row(l0_s5_b_glu), S5_CHUNK).reshape(t, main_w)

    wq1_t = jnp.concatenate([l1_w_in[:, :main_w].T,
                             _pad_heads_t(l1_w_in[:, main_w:], mem_off)], axis=0).astype(BF16)
    x2, q_t, qm1_t, kq, km, v_t = _layer0_tail(
        s5_out, (qm0_t, k0, v0_t), l0_w_out[:main_w].astype(BF16),
        l0_w_out[main_w:].astype(BF16), x2d, row(l0_ffn_norm),
        l0_ffn_w_gate.astype(BF16), l0_ffn_w_up.astype(BF16), l0_ffn_w_down.astype(BF16),
        row(l1_mix_norm), row(kv_norm), wq1_t, _pad_head_cols(kv_w_k).astype(BF16),
        kv_w_v.T.astype(BF16), _pad_head_cols(_tile_gain(kv_k_gain, MOBA_KV_HEADS)),
        _pad_head_cols(_tile_gain(l1_moba_q_gain, MOBA_KV_HEADS)), batch, seq, tm)
    nb = seq // MOBA_BLOCK
    moba_t = _moba(q_t, kq.reshape(batch, seq, -1), v_t, km.reshape(batch, nb, -1))
    k1, v1_t = _mem_kv(
        mem, row(l1_mem_norm), l1_w_mem_k.astype(BF16), l1_w_mem_v.T.astype(BF16), ones_kv,
        _tile_gain(l1_mem_k_gain, MEM_HEADS) * _tile_gain(l1_mem_q_gain, MEM_HEADS))

    wr_t = jnp.zeros((ROUTER_ROWS, d), F32).at[:N_EXPERTS].set(l1_moe_router.astype(F32).T)
    wr_hi = wr_t.astype(BF16)
    wr_lo = (wr_t - wr_hi.astype(F32)).astype(BF16)
    x3, h3, rec, cnt = _outproj1(moba_t, (qm1_t, k1, v1_t), l1_w_out[:main_w].astype(BF16),
                                 l1_w_out[main_w:].astype(BF16), x2, row(l1_ffn_norm),
                                 wr_hi, wr_lo, seq, tm)

    assert d == ROW_TILE * LANES
    tme = MOE_TILE
    counts = cnt[:N_EXPERTS, 0].astype(I32)
    padded = ((counts + tme - 1) // tme) * tme
    ends = jnp.cumsum(padded)
    starts = ends - padded
    max_tiles = (2 * t) // tme + N_EXPERTS
    def lookup(table, idx):
        hit = idx[None, :] == jnp.arange(N_EXPERTS, dtype=I32)[:, None]
        return jnp.sum(jnp.where(hit, table[:, None], 0), axis=0)

    tile_ids = jnp.arange(max_tiles, dtype=I32)
    tile_expert = jnp.minimum(
        jnp.sum((ends // tme)[None, :] <= tile_ids[:, None], axis=1), N_EXPERTS - 1).astype(I32)
    n_valid = jnp.clip(lookup(counts, tile_expert) + lookup(starts, tile_expert)
                       - tile_ids * tme, 0, tme)
    n_used = (ends[-1] // tme).astype(I32).reshape(1)
    d1 = lookup(starts, rec[0].astype(I32)) + rec[2].astype(I32)
    d2 = lookup(starts, rec[1].astype(I32)) + rec[3].astype(I32)
    gates = rec[4:6].T
    def tile_dest(tile):
        return jnp.concatenate([d1.reshape(t // tile, 1, tile), d2.reshape(t // tile, 1, tile)],
                               axis=2)

    xs = _dispatch(starts + counts, padded - counts, n_used, tile_dest(DISPATCH_TILE), h3,
                   max_tiles, tme, DISPATCH_TILE)
    ys = _moe_ffn(tile_expert, n_valid.astype(I32), n_used, xs, l1_moe_w_gate, l1_moe_w_up,
                  l1_moe_w_down, tme, MOE_FF_TILE)
    out = _combine(tile_dest(COMBINE_TILE), ys, x3, gates, COMBINE_TILE)
    return out.reshape(batch, seq, d)
```

```python
import functools

import jax
import jax.numpy as jnp
from jax import lax
from jax.experimental import pallas as pl
from jax.experimental.pallas import tpu as pltpu

F32 = jnp.float32
BF16 = jnp.bfloat16
I32 = jnp.int32

RMS_EPS = 1e-6
NEG_INF = -1e30
HEAD_DIM = 64
HEAD_SLAB = 128
MEM_HEADS = 4
MOBA_KV_HEADS = 4
MOBA_GROUP = 3
MOBA_BLOCK = 256
MOBA_TOP_BLOCKS = 3
MOBA_KV_PER_STEP = 4
S5_GROUP_DIM = 16
S5_STATE = 64
S5_GROUPS_PER_SLAB = HEAD_SLAB // S5_GROUP_DIM
S5_SLAB_STATES = S5_GROUPS_PER_SLAB * S5_STATE
N_EXPERTS = 8
ROUTER_LANES = 128
ROUTER_ROWS = 16
SUBLANES = 8
LANES = 128
ROW_TILE = SUBLANES
DMA_UNROLL = 8
LOG2_E = 1.4426950408889634
V_ROWS = HEAD_DIM + 16

VMEM_LIMIT_BYTES = 56 * 1024 * 1024

TOKEN_TILE = 512
S5_CHUNK = 64
MOE_TILE = 1024
MOE_FF_TILE = 512
DISPATCH_TILE = 1024
COMBINE_TILE = 512

_NT = (((1,), (1,)), ((), ()))
_TN = (((0,), (0,)), ((), ()))


def _cparams(sem):
    return pltpu.CompilerParams(dimension_semantics=sem, vmem_limit_bytes=VMEM_LIMIT_BYTES)


def _rms(x, g):
    return x * lax.rsqrt(jnp.mean(x * x, axis=-1, keepdims=True) + RMS_EPS) * g


def _store_vt_with_ones(vt_ref, vt, n_heads):
    n = vt.shape[1]
    ones_row = jnp.where(lax.broadcasted_iota(I32, (V_ROWS - HEAD_DIM, n), 0) == 0, 1.0, 0.0)
    for h in range(n_heads):
        vt_ref[0, V_ROWS * h:V_ROWS * h + HEAD_DIM, :] = (
            vt[HEAD_DIM * h:HEAD_DIM * (h + 1), :].astype(BF16))
        vt_ref[0, V_ROWS * h + HEAD_DIM:V_ROWS * (h + 1), :] = ones_row.astype(BF16)


def _slab_norm_t(qt, n_heads, scale):
    outs = []
    for h in range(n_heads):
        s = qt[HEAD_SLAB * h:HEAD_SLAB * (h + 1), :]
        ms = jnp.sum(s * s, axis=0, keepdims=True) * (1.0 / HEAD_DIM)
        outs.append(s * (lax.rsqrt(ms + RMS_EPS) * scale))
    return outs


def _inproj0_kernel(x_ref, g_ref, wm_ref, wqt_ref, main_ref, qt_ref):
    hb = _rms(x_ref[...], g_ref[...]).astype(BF16)
    main_ref[...] = jnp.dot(hb, wm_ref[...], preferred_element_type=F32)
    qt = lax.dot_general(wqt_ref[...], hb, _NT, preferred_element_type=F32)
    for h, s in enumerate(_slab_norm_t(qt, MEM_HEADS, HEAD_DIM ** -0.5 * LOG2_E)):
        qt_ref[0, HEAD_SLAB * h:HEAD_SLAB * (h + 1), :] = s.astype(BF16)


def _inproj0(x2d, g, wm, wqt, batch, seq, tm):
    t, d = x2d.shape
    per_b = seq // tm
    nm = wm.shape[1]
    nq = wqt.shape[0]
    return pl.pallas_call(
        _inproj0_kernel,
        grid=(t // tm,),
        in_specs=[
            pl.BlockSpec((tm, d), lambda i: (i, 0)),
            pl.BlockSpec((1, d), lambda i: (0, 0)),
            pl.BlockSpec((d, nm), lambda i: (0, 0)),
            pl.BlockSpec((nq, d), lambda i: (0, 0)),
        ],
        out_specs=[
            pl.BlockSpec((tm, nm), lambda i: (i, 0)),
            pl.BlockSpec((1, nq, tm), lambda i: (i // per_b, 0, i % per_b)),
        ],
        out_shape=[
            jax.ShapeDtypeStruct((t, nm), F32),
            jax.ShapeDtypeStruct((batch, nq, seq), BF16),
        ],
        compiler_params=_cparams(("parallel",)),
        name="inproj0",
    )(x2d, g, wm, wqt)


def _mem_kv_kernel(mem_ref, g_ref, wk_ref, wvt_ref, ones_ref, kg_ref, k_ref, vt_ref):
    mb = _rms(mem_ref[0], g_ref[...]).astype(BF16)
    k = jnp.dot(mb, wk_ref[...], preferred_element_type=F32)
    ms = jnp.dot(k * k, ones_ref[...], preferred_element_type=F32,
                 precision=lax.Precision.HIGHEST) * (1.0 / HEAD_DIM)
    k_ref[0] = (k * lax.rsqrt(ms + RMS_EPS) * kg_ref[...]).astype(BF16)
    _store_vt_with_ones(vt_ref, lax.dot_general(wvt_ref[...], mb, _NT,
                                                preferred_element_type=F32), MEM_HEADS)


def _mem_kv(mem, g, wk, wvt, ones, kg):
    b, m, d = mem.shape
    w = wk.shape[1]
    v_rows = MEM_HEADS * V_ROWS
    return pl.pallas_call(
        _mem_kv_kernel,
        grid=(b,),
        in_specs=[
            pl.BlockSpec((1, m, d), lambda i: (i, 0, 0)),
            pl.BlockSpec((1, d), lambda i: (0, 0)),
            pl.BlockSpec((d, w), lambda i: (0, 0)),
            pl.BlockSpec((w, d), lambda i: (0, 0)),
            pl.BlockSpec((w, w), lambda i: (0, 0)),
            pl.BlockSpec((1, w), lambda i: (0, 0)),
        ],
        out_specs=[
            pl.BlockSpec((1, m, w), lambda i: (i, 0, 0)),
            pl.BlockSpec((1, v_rows, m), lambda i: (i, 0, 0)),
        ],
        out_shape=[
            jax.ShapeDtypeStruct((b, m, w), BF16),
            jax.ShapeDtypeStruct((b, v_rows, m), BF16),
        ],
        compiler_params=_cparams(("parallel",)),
        name="mem_kv",
    )(mem, g, wk, wvt, ones, kg)


def _mem_scores_t(qt_ref, k_ref):
    scores = []
    for h in range(MEM_HEADS):
        pair = h // 2
        k2 = k_ref[0, :, HEAD_SLAB * pair:HEAD_SLAB * (pair + 1)]
        q = qt_ref[0, HEAD_SLAB * h:HEAD_SLAB * (h + 1), :]
        scores.append(jnp.dot(k2, q, preferred_element_type=F32))
    return scores


def _mem_attend_t(scores, vt_ref):
    outs = []
    for h, s in enumerate(scores):
        p = jnp.exp2(s - jnp.max(s, axis=0, keepdims=True)).astype(BF16)
        v = vt_ref[0, V_ROWS * h:V_ROWS * (h + 1), :]
        acc = jnp.dot(v, p, preferred_element_type=F32)
        outs.append((acc[0:HEAD_DIM, :] / acc[HEAD_DIM:HEAD_DIM + 1, :]).astype(BF16))
    return jnp.concatenate(outs, axis=0)


def _mem_attn_specs(qt, k, vt, per_b, tm):
    return [
        pl.BlockSpec((1, qt.shape[1], tm), lambda i: (i // per_b, 0, i % per_b)),
        pl.BlockSpec((1,) + k.shape[1:], lambda i: (i // per_b, 0, 0)),
        pl.BlockSpec((1,) + vt.shape[1:], lambda i: (i // per_b, 0, 0)),
    ]


def _s5_kernel(u_ref, bd_ref, cd_ref, ar_ref, ai_ref, d_ref, wglu_ref, bglu_ref, o_ref,
               buf_ref, st_ref, ut_ref, ot_ref, *, ts, n_slabs):
    half = S5_SLAB_STATES
    width = 2 * half
    batch = u_ref.shape[0]

    @pl.when(pl.program_id(0) == 0)
    def _():
        st_ref[...] = jnp.zeros_like(st_ref)

    for b in range(batch):
        for j in range(n_slabs):
            ut_ref[j, pl.ds(b, ts, stride=batch), :] = u_ref[b, :, HEAD_SLAB * j:HEAD_SLAB * (j + 1)]

    for j in range(n_slabs):
        buf_ref[:, width * j:width * (j + 1)] = jnp.dot(
            ut_ref[j].astype(BF16), bd_ref[j], preferred_element_type=F32)

    for j in range(n_slabs):
        re = slice(width * j, width * j + half)
        im = slice(width * j + half, width * (j + 1))
        ar = jnp.broadcast_to(ar_ref[j:j + 1, :], (SUBLANES, half))
        ai = jnp.broadcast_to(ai_ref[j:j + 1, :], (SUBLANES, half))

        def step(t, carry, re=re, im=im, ar=ar, ai=ai):
            xr, xi = carry
            rows = pl.ds(pl.multiple_of(t * SUBLANES, SUBLANES), SUBLANES)
            nxr = ar * xr - ai * xi + buf_ref[rows, re]
            nxi = ar * xi + ai * xr + buf_ref[rows, im]
            buf_ref[rows, re] = nxr
            buf_ref[rows, im] = nxi
            return nxr, nxi

        xr, xi = lax.fori_loop(0, ts, step, (st_ref[:, re], st_ref[:, im]), unroll=True)
        st_ref[:, re] = xr
        st_ref[:, im] = xi

    ys = []
    for j in range(n_slabs):
        cols = slice(HEAD_SLAB * j, HEAD_SLAB * (j + 1))
        ys.append(jnp.dot(buf_ref[:, width * j:width * (j + 1)].astype(BF16), cd_ref[j],
                          preferred_element_type=F32) + d_ref[:, cols] * ut_ref[j])
    g = jax.nn.gelu(jnp.concatenate(ys, axis=1))
    z = jnp.dot(g.astype(BF16), wglu_ref[...], preferred_element_type=F32) + bglu_ref[...]
    out = g / (1.0 + jnp.exp(-z))
    for j in range(n_slabs):
        ot_ref[j] = out[:, HEAD_SLAB * j:HEAD_SLAB * (j + 1)]
    for b in range(batch):
        for j in range(n_slabs):
            o_ref[b, :, HEAD_SLAB * j:HEAD_SLAB * (j + 1)] = (
                ot_ref[j, pl.ds(b, ts, stride=batch), :].astype(BF16))


def _s5(u, bd, cd, ar, ai, dskip, wglu, bglu, ts):
    batch, seq, c = u.shape
    n_slabs = c // HEAD_SLAB
    width = 2 * S5_SLAB_STATES
    tile = ts * batch
    return pl.pallas_call(
        functools.partial(_s5_kernel, ts=ts, n_slabs=n_slabs),
        grid=(seq // ts,),
        in_specs=[
            pl.BlockSpec((batch, ts, c), lambda i: (0, i, 0)),
            pl.BlockSpec((n_slabs, HEAD_SLAB, width), lambda i: (0, 0, 0)),
            pl.BlockSpec((n_slabs, width, HEAD_SLAB), lambda i: (0, 0, 0)),
            pl.BlockSpec((n_slabs, S5_SLAB_STATES), lambda i: (0, 0)),
            pl.BlockSpec((n_slabs, S5_SLAB_STATES), lambda i: (0, 0)),
            pl.BlockSpec((1, c), lambda i: (0, 0)),
            pl.BlockSpec((c, c), lambda i: (0, 0)),
            pl.BlockSpec((1, c), lambda i: (0, 0)),
        ],
        out_specs=pl.BlockSpec((batch, ts, c), lambda i: (0, i, 0)),
        out_shape=jax.ShapeDtypeStruct((batch, seq, c), BF16),
        scratch_shapes=[
            pltpu.VMEM((tile, n_slabs * width), F32),
            pltpu.VMEM((batch, n_slabs * width), F32),
            pltpu.VMEM((n_slabs, tile, HEAD_SLAB), F32),
            pltpu.VMEM((n_slabs, tile, HEAD_SLAB), F32),
        ],
        compiler_params=_cparams(("arbitrary",)),
        name="s5_mixer",
    )(u, bd, cd, ar, ai, dskip, wglu, bglu)


def _outproj_body(a, a_transposed, mem_refs, wa_ref, wb_ref, x_ref, g_ref):
    qt_ref, mk_ref, mvt_ref = mem_refs
    dn = _TN if a_transposed else (((1,), (0,)), ((), ()))
    scores = _mem_scores_t(qt_ref, mk_ref)
    y = lax.dot_general(a, wa_ref[...], dn, preferred_element_type=F32)
    y = y + lax.dot_general(_mem_attend_t(scores, mvt_ref), wb_ref[...], _TN,
                            preferred_element_type=F32)
    xn = x_ref[...] + y
    return xn, _rms(xn, g_ref[...])


def _layer0_tail_kernel(a_ref, qt_ref, mk_ref, mvt_ref, wa_ref, wb_ref, x_ref, g_ref,
                        wg_ref, wu_ref, wd_ref, *rest, tm, n_q, per_b):
    proj1_in, (o_ref, *proj1_out, h_ref) = rest[:7], rest[7:]
    xn, h = _outproj_body(a_ref[...], False, (qt_ref, mk_ref, mvt_ref), wa_ref, wb_ref, x_ref,
                          g_ref)
    h_ref[...] = h.astype(BF16)
    half = h_ref.shape[0] // 2
    halves = (slice(0, half), slice(half, 2 * half))
    mids = []
    for rows in halves:
        hb = h_ref[rows, :]
        gt = jnp.dot(hb, wg_ref[...], preferred_element_type=F32)
        up = jnp.dot(hb, wu_ref[...], preferred_element_type=F32)
        mids.append(((gt / (1.0 + jnp.exp(-gt))) * up).astype(BF16))
    for rows, mid in zip(halves, mids):
        o_ref[rows, :] = xn[rows, :] + jnp.dot(mid, wd_ref[...], preferred_element_type=F32)
    _proj1_body(o_ref[...], *proj1_in, *proj1_out, tm=tm, n_q=n_q, per_b=per_b)


def _layer0_tail(a, mem_qkv, wa, wb, x2d, g, wg, wu, wd, g1, g2, wqt, wk, wvt, kg, qg,
                 batch, seq, tm):
    t, d = x2d.shape
    per_b = seq // tm
    ka, kb = a.shape[1], wb.shape[0]
    ff = wg.shape[1]
    nq_rows = wqt.shape[0]
    n_q = (nq_rows - MEM_HEADS * HEAD_SLAB) // HEAD_DIM
    kw = wk.shape[1]
    vw = wvt.shape[0]
    v_rows = MOBA_KV_HEADS * V_ROWS
    nblk = tm // MOBA_BLOCK
    resident = dict(pipeline_mode=pl.Buffered(1))
    return pl.pallas_call(
        functools.partial(_layer0_tail_kernel, tm=tm, n_q=n_q, per_b=per_b),
        grid=(t // tm,),
        in_specs=[
            pl.BlockSpec((tm, ka), lambda i: (i, 0)),
            *_mem_attn_specs(*mem_qkv, per_b, tm),
            pl.BlockSpec((ka, d), lambda i: (0, 0), **resident),
            pl.BlockSpec((kb, d), lambda i: (0, 0), **resident),
            pl.BlockSpec((tm, d), lambda i: (i, 0)),
            pl.BlockSpec((1, d), lambda i: (0, 0)),
            pl.BlockSpec((d, ff), lambda i: (0, 0), **resident),
            pl.BlockSpec((d, ff), lambda i: (0, 0), **resident),
            pl.BlockSpec((ff, d), lambda i: (0, 0), **resident),
            pl.BlockSpec((1, d), lambda i: (0, 0)),
            pl.BlockSpec((1, d), lambda i: (0, 0)),
            pl.BlockSpec((nq_rows, d), lambda i: (0, 0), **resident),
            pl.BlockSpec((d, kw), lambda i: (0, 0), **resident),
            pl.BlockSpec((vw, d), lambda i: (0, 0), **resident),
            pl.BlockSpec((1, kw), lambda i: (0, 0)),
            pl.BlockSpec((1, kw), lambda i: (0, 0)),
        ],
        out_specs=[
            pl.BlockSpec((tm, d), lambda i: (i, 0)),
            pl.BlockSpec((1, n_q * HEAD_DIM, tm), lambda i: (i // per_b, 0, i % per_b)),
            pl.BlockSpec((1, MEM_HEADS * HEAD_SLAB, tm), lambda i: (i // per_b, 0, i % per_b)),
            pl.BlockSpec((tm, kw), lambda i: (i, 0)),
            pl.BlockSpec((nblk, 1, kw), lambda i: (i, 0, 0)),
            pl.BlockSpec((1, v_rows, tm), lambda i: (i // per_b, 0, i % per_b)),
        ],
        out_shape=[
            jax.ShapeDtypeStruct((t, d), F32),
            jax.ShapeDtypeStruct((batch, n_q * HEAD_DIM, seq), BF16),
            jax.ShapeDtypeStruct((batch, MEM_HEADS * HEAD_SLAB, seq), BF16),
            jax.ShapeDtypeStruct((t, kw), BF16),
            jax.ShapeDtypeStruct((t // MOBA_BLOCK, 1, kw), F32),
            jax.ShapeDtypeStruct((batch, v_rows, seq), BF16),
        ],
        scratch_shapes=[pltpu.VMEM((tm, d), BF16)],
        compiler_params=_cparams(("parallel",)),
        name="layer0_tail",
    )(a, *mem_qkv, wa, wb, x2d, g, wg, wu, wd, g1, g2, wqt, wk, wvt, kg, qg)


def _outproj1_kernel(at_ref, qt_ref, mk_ref, mvt_ref, wa_ref, wb_ref, x_ref, g_ref, wrh_ref,
                     wrl_ref, xo_ref, h_ref, rec_ref, cnt_ref, run_ref, *, tm):
    i = pl.program_id(0)

    @pl.when(i == 0)
    def _():
        run_ref[...] = jnp.zeros_like(run_ref)

    xn, h = _outproj_body(at_ref[0], True, (qt_ref, mk_ref, mvt_ref), wa_ref, wb_ref, x_ref,
                          g_ref)
    xo_ref[...] = xn
    _to_row_tiles(h_ref, h)

    h_hi = h.astype(BF16)
    h_lo = (h - h_hi.astype(F32)).astype(BF16)
    logits = (lax.dot_general(wrh_ref[...], h_hi, _NT, preferred_element_type=F32)
              + lax.dot_general(wrh_ref[...], h_lo, _NT, preferred_element_type=F32)
              + lax.dot_general(wrl_ref[...], h_hi, _NT, preferred_element_type=F32))
    ex = lax.broadcasted_iota(I32, (ROUTER_ROWS, tm), 0)
    logits = jnp.where(ex < N_EXPERTS, logits, -jnp.inf)
    m1 = jnp.max(logits, axis=0, keepdims=True)
    e1 = jnp.min(jnp.where(logits == m1, ex, ROUTER_ROWS), axis=0, keepdims=True)
    rest = jnp.where(ex == e1, -jnp.inf, logits)
    m2 = jnp.max(rest, axis=0, keepdims=True)
    e2 = jnp.min(jnp.where(rest == m2, ex, ROUTER_ROWS), axis=0, keepdims=True)
    w2 = jnp.exp(m2 - m1)
    g1 = 1.0 / (1.0 + w2)
    g2 = w2 / (1.0 + w2)

    hot = jnp.where((ex == e1) | (ex == e2), 1.0, 0.0)
    r_io = lax.broadcasted_iota(I32, (tm, tm), 0)
    c_io = lax.broadcasted_iota(I32, (tm, tm), 1)
    triu = jnp.where(r_io < c_io, 1.0, 0.0).astype(BF16)
    before = (jnp.dot(hot.astype(BF16), triu, preferred_element_type=F32)
              + run_ref[:, 0:1])
    r1 = jnp.sum(jnp.where(ex == e1, before, 0.0), axis=0, keepdims=True)
    r2 = jnp.sum(jnp.where(ex == e2, before, 0.0), axis=0, keepdims=True)
    run = run_ref[...] + jnp.sum(hot, axis=1, keepdims=True)
    run_ref[...] = run
    cnt_ref[...] = run

    slot = lax.broadcasted_iota(I32, (SUBLANES, tm), 0)
    rec = jnp.where(slot == 0, e1.astype(F32), 0.0)
    rec = jnp.where(slot == 1, e2.astype(F32), rec)
    rec = jnp.where(slot == 2, r1, rec)
    rec = jnp.where(slot == 3, r2, rec)
    rec = jnp.where(slot == 4, g1, rec)
    rec = jnp.where(slot == 5, g2, rec)
    rec_ref[...] = rec


def _outproj1(at, mem_qkv, wa, wb, x2d, g, wr_hi, wr_lo, seq, tm):
    t, d = x2d.shape
    per_b = seq // tm
    ka, kb = at.shape[1], wb.shape[0]
    return pl.pallas_call(
        functools.partial(_outproj1_kernel, tm=tm),
        grid=(t // tm,),
        in_specs=[
            pl.BlockSpec((1, ka, tm), lambda i: (i // per_b, 0, i % per_b)),
            *_mem_attn_specs(*mem_qkv, per_b, tm),
            pl.BlockSpec((ka, d), lambda i: (0, 0)),
            pl.BlockSpec((kb, d), lambda i: (0, 0)),
            pl.BlockSpec((tm, d), lambda i: (i, 0)),
            pl.BlockSpec((1, d), lambda i: (0, 0)),
            pl.BlockSpec((ROUTER_ROWS, d), lambda i: (0, 0)),
            pl.BlockSpec((ROUTER_ROWS, d), lambda i: (0, 0)),
        ],
        out_specs=[
            pl.BlockSpec((tm, d), lambda i: (i, 0)),
            pl.BlockSpec((tm * ROW_TILE, LANES), lambda i: (i, 0)),
            pl.BlockSpec((SUBLANES, tm), lambda i: (0, i)),
            pl.BlockSpec((ROUTER_ROWS, ROUTER_LANES), lambda i: (0, 0)),
        ],
        out_shape=[
            jax.ShapeDtypeStruct((t, d), F32),
            jax.ShapeDtypeStruct((t * ROW_TILE, LANES), F32),
            jax.ShapeDtypeStruct((SUBLANES, t), F32),
            jax.ShapeDtypeStruct((ROUTER_ROWS, ROUTER_LANES), F32),
        ],
        scratch_shapes=[pltpu.VMEM((ROUTER_ROWS, ROUTER_LANES), F32)],
        compiler_params=_cparams(("arbitrary",)),
        name="outproj1_router",
    )(at, *mem_qkv, wa, wb, x2d, g, wr_hi, wr_lo)


def _from_row_tiles(ref, base, n_rows, n_feat):
    parts = [ref[pl.ds(base + k, n_rows, stride=ROW_TILE), :] for k in range(n_feat // LANES)]
    return jnp.concatenate(parts, axis=1)


def _to_row_tiles(ref, value):
    for k in range(value.shape[1] // LANES):
        ref[pl.ds(k, value.shape[0], stride=ROW_TILE), :] = value[:, LANES * k:LANES * (k + 1)]


def _moe_ffn_kernel(te_ref, nv_ref, nu_ref, xs_ref, wg_ref, wu_ref, wd_ref, o_ref,
                    acc_ref, xb_ref, *, tm):
    del te_ref
    i = pl.program_id(0)
    f = pl.program_id(1)
    nf = pl.num_programs(1)
    half = tm // 2
    d = xb_ref.shape[1]

    def partial_out(row_sets):
        wg = wg_ref[0].astype(BF16)
        wu = wu_ref[0].astype(BF16)
        wd = wd_ref[0].astype(BF16)
        mids = []
        for rows in row_sets:
            xb = xb_ref[rows, :]
            gt = jnp.dot(xb, wg, preferred_element_type=F32)
            up = jnp.dot(xb, wu, preferred_element_type=F32)
            mids.append(((gt / (1.0 + jnp.exp(-gt))) * up).astype(BF16))
        for rows, mid in zip(row_sets, mids):
            acc_ref[rows, :] += jnp.dot(mid, wd, preferred_element_type=F32)

    @pl.when(i < nu_ref[0])
    def _():
        @pl.when(f == 0)
        def _():
            acc_ref[...] = jnp.zeros_like(acc_ref)
            xb_ref[...] = _from_row_tiles(xs_ref, 0, tm, d).astype(BF16)

        quarter = tm // 4
        nv = nv_ref[i]

        @pl.when(nv > 3 * quarter)
        def _():
            partial_out([slice(0, half), slice(half, tm)])

        @pl.when((nv > half) & (nv <= 3 * quarter))
        def _():
            partial_out([slice(0, half), slice(half, 3 * quarter)])

        @pl.when((nv > quarter) & (nv <= half))
        def _():
            partial_out([slice(0, half)])

        @pl.when(nv <= quarter)
        def _():
            partial_out([slice(0, quarter)])

        @pl.when(f == nf - 1)
        def _():
            _to_row_tiles(o_ref, acc_ref[...])

    @pl.when((i >= nu_ref[0]) & (f == nf - 1))
    def _():
        o_ref[...] = jnp.zeros_like(o_ref)


def _moe_ffn(tile_expert, n_valid, n_used, xs_rt, wg, wu, wd, tm, tf):
    d = wg.shape[1]
    ff = wg.shape[2]
    nf = ff // tf
    n_tiles = xs_rt.shape[0] // (tm * ROW_TILE)

    def live(i, nu):
        return jnp.minimum(i, jnp.maximum(nu[0] - 1, 0))

    def f_of(i, f, nu):
        return jnp.where(i < nu[0], f, nf - 1)

    def wcol_map(i, f, te, nv, nu):
        return (te[live(i, nu)], 0, f_of(i, f, nu))

    def wrow_map(i, f, te, nv, nu):
        return (te[live(i, nu)], f_of(i, f, nu), 0)

    grid_spec = pltpu.PrefetchScalarGridSpec(
        num_scalar_prefetch=3,
        grid=(n_tiles, nf),
        in_specs=[
            pl.BlockSpec((tm * ROW_TILE, LANES), lambda i, f, te, nv, nu: (live(i, nu), 0)),
            pl.BlockSpec((1, d, tf), wcol_map),
            pl.BlockSpec((1, d, tf), wcol_map),
            pl.BlockSpec((1, tf, d), wrow_map),
        ],
        out_specs=pl.BlockSpec((tm * ROW_TILE, LANES), lambda i, f, te, nv, nu: (i, 0)),
        scratch_shapes=[pltpu.VMEM((tm, d), F32), pltpu.VMEM((tm, d), BF16)],
    )
    return pl.pallas_call(
        functools.partial(_moe_ffn_kernel, tm=tm),
        grid_spec=grid_spec,
        out_shape=jax.ShapeDtypeStruct(xs_rt.shape, F32),
        compiler_params=_cparams(("arbitrary", "arbitrary")),
        name="ffn_moe",
    )(tile_expert, n_valid, n_used, xs_rt, wg, wu, wd)


def _proj1_body(x, g1_ref, g2_ref, wqt_ref, wk_ref, wvt_ref, kg_ref, qg_ref,
                qt_ref, qmt_ref, k_ref, km_ref, vt_ref, *, tm, n_q, per_b):
    h1 = _rms(x, g1_ref[...]).astype(BF16)
    h2 = _rms(x, g2_ref[...]).astype(BF16)
    qt = lax.dot_general(wqt_ref[...], h1, _NT, preferred_element_type=F32)

    for h in range(n_q):
        s = qt[HEAD_DIM * h:HEAD_DIM * (h + 1), :]
        ms = jnp.mean(s * s, axis=0, keepdims=True)
        qt_ref[0, HEAD_DIM * h:HEAD_DIM * (h + 1), :] = (
            s * (lax.rsqrt(ms + RMS_EPS) * (HEAD_DIM ** -0.5 * LOG2_E))).astype(BF16)
    qm = qt[HEAD_DIM * n_q:, :]
    for h, s in enumerate(_slab_norm_t(qm, MEM_HEADS, HEAD_DIM ** -0.5 * LOG2_E)):
        qmt_ref[0, HEAD_SLAB * h:HEAD_SLAB * (h + 1), :] = s.astype(BF16)

    k = jnp.dot(h2, wk_ref[...], preferred_element_type=F32)
    nblk = tm // MOBA_BLOCK
    first_blk = lax.rem(pl.program_id(0), per_b) * nblk
    lane = lax.broadcasted_iota(I32, (MOBA_BLOCK, HEAD_SLAB), 1)
    for kv in range(MOBA_KV_HEADS):
        cols = slice(HEAD_SLAB * kv, HEAD_SLAB * (kv + 1))
        ks = k[:, cols]
        ms = jnp.sum(ks * ks, axis=-1, keepdims=True) * (1.0 / HEAD_DIM)
        kn = ks * lax.rsqrt(ms + RMS_EPS) * kg_ref[:, cols]
        kq = kn * qg_ref[:, cols]
        for j in range(nblk):
            rows = slice(MOBA_BLOCK * j, MOBA_BLOCK * (j + 1))
            km_ref[j, :, cols] = jnp.mean(kn[rows], axis=0, keepdims=True) * qg_ref[:, cols]
            k_ref[rows, cols] = jnp.where(lane == HEAD_DIM + first_blk + j, 1.0,
                                          kq[rows]).astype(BF16)

    _store_vt_with_ones(vt_ref, lax.dot_general(wvt_ref[...], h2, _NT,
                                                preferred_element_type=F32), MOBA_KV_HEADS)


def _moba_kernel(qt_ref, qtn_ref, k_ref, vt_ref, km_ref, o_ref, s_ref, bias_ref):
    qb = pl.program_id(2)
    nb = km_ref.shape[1]
    bq = MOBA_BLOCK
    nq = MOBA_GROUP * bq
    q_rows = MOBA_GROUP * HEAD_DIM
    blk = lax.broadcasted_iota(I32, (nb, nq), 0)

    def queries(ref, kv):
        return jnp.concatenate(
            [ref[0, q_rows * kv + HEAD_DIM * g:q_rows * kv + HEAD_DIM * (g + 1), :]
             for g in range(MOBA_GROUP)], axis=1)

    def block_bias(q64, kv, q_blk):
        km = km_ref[0, :, HEAD_SLAB * kv:HEAD_SLAB * kv + HEAD_DIM]
        km_hi = km.astype(BF16)
        km_lo = (km - km_hi.astype(F32)).astype(BF16)
        gate = (jnp.dot(km_hi, q64, preferred_element_type=F32)
                + jnp.dot(km_lo, q64, preferred_element_type=F32))
        gate = jnp.where(blk < q_blk, gate, NEG_INF)
        rank = jnp.zeros((nb, nq), F32)
        for m in range(nb):
            gm = gate[m:m + 1, :]
            rank = rank + jnp.where(gm > gate, 1.0,
                                    jnp.where((gm == gate) & (blk > m), 1.0, 0.0))
        dropped = (blk < q_blk) & (rank >= MOBA_TOP_BLOCKS)
        bias = jnp.where(dropped, NEG_INF, 0.0)
        return jnp.concatenate([bias, jnp.zeros((nb, nq), F32)], axis=0).astype(BF16)

    key_pos = lax.broadcasted_iota(I32, (bq, nq), 0)
    qry_pos = lax.broadcasted_iota(I32, (bq, nq), 1) & (bq - 1)
    causal = key_pos <= qry_pos

    for n_blocks in range(1, nb + 1):
        @pl.when(qb == n_blocks - 1)
        def _(n_blocks=n_blocks):
            mx = []
            for kv in range(MOBA_KV_PER_STEP):
                bias = bias_ref[kv] if n_blocks > 1 else jnp.zeros((2 * nb, nq), BF16)
                q3 = jnp.concatenate(
                    [queries(qt_ref, kv), bias,
                     jnp.zeros((HEAD_SLAB - HEAD_DIM - 2 * nb, nq), BF16)], axis=0)
                m = None
                for n in range(n_blocks):
                    s = jnp.dot(k_ref[0, bq * n:bq * (n + 1), HEAD_SLAB * kv:HEAD_SLAB * (kv + 1)],
                                q3, preferred_element_type=F32)
                    if n == n_blocks - 1:
                        s = jnp.where(causal, s, NEG_INF)
                    s_ref[kv, n] = s
                    cm = jnp.max(s, axis=0, keepdims=True)
                    m = cm if m is None else jnp.maximum(m, cm)
                mx.append(m)
            if n_blocks < nb:
                for kv in range(MOBA_KV_PER_STEP):
                    bias_ref[kv] = block_bias(queries(qtn_ref, kv), kv, n_blocks)
            for kv in range(MOBA_KV_PER_STEP):
                acc = jnp.zeros((V_ROWS, nq), F32)
                for n in range(n_blocks):
                    p = jnp.exp2(s_ref[kv, n] - mx[kv]).astype(BF16)
                    acc = acc + jnp.dot(vt_ref[0, V_ROWS * kv:V_ROWS * (kv + 1),
                                               bq * n:bq * (n + 1)], p,
                                        preferred_element_type=F32)
                out = acc[0:HEAD_DIM, :] / acc[HEAD_DIM:HEAD_DIM + 1, :]
                for g in range(MOBA_GROUP):
                    r0 = q_rows * kv + HEAD_DIM * g
                    o_ref[0, r0:r0 + HEAD_DIM, :] = out[:, bq * g:bq * (g + 1)].astype(BF16)


def _moba(qt, k3, vt, km3):
    b, nq_rows, s = qt.shape
    nb = s // MOBA_BLOCK
    per = MOBA_KV_PER_STEP
    rows = per * MOBA_GROUP * HEAD_DIM
    return pl.pallas_call(
        _moba_kernel,
        grid=(b, MOBA_KV_HEADS // per, nb),
        in_specs=[
            pl.BlockSpec((1, rows, MOBA_BLOCK), lambda i, j, q: (i, j, q)),
            pl.BlockSpec((1, rows, MOBA_BLOCK), lambda i, j, q: (i, j, jnp.minimum(q + 1, nb - 1))),
            pl.BlockSpec((1, s, per * HEAD_SLAB), lambda i, j, q: (i, 0, j)),
            pl.BlockSpec((1, per * V_ROWS, s), lambda i, j, q: (i, j, 0)),
            pl.BlockSpec((1, nb, per * HEAD_SLAB), lambda i, j, q: (i, 0, j)),
        ],
        out_specs=pl.BlockSpec((1, rows, MOBA_BLOCK), lambda i, j, q: (i, j, q)),
        out_shape=jax.ShapeDtypeStruct((b, nq_rows, s), BF16),
        scratch_shapes=[pltpu.VMEM((per, nb, MOBA_BLOCK, MOBA_GROUP * MOBA_BLOCK), F32),
                        pltpu.VMEM((per, 2 * nb, MOBA_GROUP * MOBA_BLOCK), BF16)],
        compiler_params=_cparams(("parallel", "parallel", "arbitrary")),
        name="moba_attn",
    )(qt, qt, k3, vt, km3)


def _row_copy(src_ref, src_row, dst_ref, dst_row, sem):
    def tile(ref, r):
        return ref.at[pl.ds(pl.multiple_of(r * ROW_TILE, ROW_TILE), ROW_TILE)]

    return pltpu.make_async_copy(tile(src_ref, src_row), tile(dst_ref, dst_row), sem)


def _for_each_row(n_rows, fn):
    def body(j8, carry):
        for u in range(DMA_UNROLL):
            fn(j8 * DMA_UNROLL + u, u % 2)
        return carry

    lax.fori_loop(0, n_rows // DMA_UNROLL, body, 0)


def _dispatch_kernel(pad_lo_ref, pad_n_ref, nu_ref, dest_ref, h_ref, xs_ref, sem, pad_sem, *,
                     tm, pad_bits, tile_rows, n_tiles):
    @pl.when(pl.program_id(0) == 0)
    def _():
        def fill_tile(j, carry):
            cp = pltpu.make_async_copy(
                h_ref.at[pl.ds(0, tile_rows * ROW_TILE)],
                xs_ref.at[pl.ds(pl.multiple_of(j * tile_rows * ROW_TILE, ROW_TILE),
                                tile_rows * ROW_TILE)], pad_sem)
            cp.start()
            cp.wait()
            return carry

        lax.fori_loop(nu_ref[0], n_tiles, fill_tile, 0)

        for start in (True, False):
            for e in range(N_EXPERTS):
                n = pad_n_ref[e]
                for bit in reversed(range(pad_bits)):
                    size = 1 << bit

                    @pl.when((n & size) != 0)
                    def _(n=n, e=e, bit=bit, size=size, start=start):
                        first = pad_lo_ref[e] + ((n >> (bit + 1)) << (bit + 1))
                        cp = pltpu.make_async_copy(
                            h_ref.at[pl.ds(0, size * ROW_TILE)],
                            xs_ref.at[pl.ds(pl.multiple_of(first * ROW_TILE, ROW_TILE),
                                            size * ROW_TILE)], pad_sem)
                        cp.start() if start else cp.wait()

    def copy(j, slot):
        return _row_copy(h_ref, j, xs_ref, dest_ref[0, 0, slot * tm + j], sem)

    for slot in range(2):
        _for_each_row(tm, lambda j, lane, slot=slot: copy(j, slot).start(priority=lane))
    for slot in range(2):
        _for_each_row(tm, lambda j, lane, slot=slot: copy(j, slot).wait())


def _dispatch(pad_lo, pad_n, n_used, dest3, h_rt, n_tiles, tile_rows, tm):
    assert tile_rows <= tm
    grid_spec = pltpu.PrefetchScalarGridSpec(
        num_scalar_prefetch=3,
        grid=(h_rt.shape[0] // (tm * ROW_TILE),),
        in_specs=[
            pl.BlockSpec((1, 1, 2 * tm), lambda i, lo, n, nu: (i, 0, 0), memory_space=pltpu.SMEM),
            pl.BlockSpec((tm * ROW_TILE, LANES), lambda i, lo, n, nu: (i, 0)),
        ],
        out_specs=pl.BlockSpec(memory_space=pl.ANY),
        scratch_shapes=[pltpu.SemaphoreType.DMA(()), pltpu.SemaphoreType.DMA(())],
    )
    return pl.pallas_call(
        functools.partial(_dispatch_kernel, tm=tm, pad_bits=(tile_rows - 1).bit_length(),
                          tile_rows=tile_rows, n_tiles=n_tiles),
        grid_spec=grid_spec,
        out_shape=jax.ShapeDtypeStruct((n_tiles * tile_rows * ROW_TILE, LANES), F32),
        compiler_params=_cparams(("arbitrary",)),
        name="moe_dispatch",
    )(pad_lo, pad_n, n_used, dest3, h_rt)


def _combine_kernel(dest_ref, next_ref, ys_ref, x_ref, gate_ref, o_ref, buf_ref, sem, *, tm):
    i = pl.program_id(0)
    cur = lax.rem(i, 2)

    def copy(idx_ref, buf_slot, j):
        return _row_copy(ys_ref, idx_ref[0, 0, j], buf_ref.at[buf_slot], j, sem.at[buf_slot])

    @pl.when(i == 0)
    def _():
        _for_each_row(2 * tm, lambda j, lane: copy(dest_ref, 0, j).start(priority=lane))

    @pl.when(i + 1 < pl.num_programs(0))
    def _():
        _for_each_row(2 * tm, lambda j, lane: copy(next_ref, 1 - cur, j).start(priority=lane))

    _for_each_row(2 * tm, lambda j, lane: copy(dest_ref, cur, j).wait())
    g1 = gate_ref[:, 0:1]
    g2 = gate_ref[:, 1:2]
    d = x_ref.shape[1]
    for slot in range(2):
        @pl.when(cur == slot)
        def _(slot=slot):
            rows = buf_ref.at[slot]
            y1 = _from_row_tiles(rows, 0, tm, d)
            y2 = _from_row_tiles(rows, tm * ROW_TILE, tm, d)
            o_ref[...] = x_ref[...] + (g1 * y1 + g2 * y2)


def _combine(dest3, ys, x2d, gates, tm):
    t, d = x2d.shape
    n = t // tm
    return pl.pallas_call(
        functools.partial(_combine_kernel, tm=tm),
        grid=(n,),
        in_specs=[
            pl.BlockSpec((1, 1, 2 * tm), lambda i: (i, 0, 0), memory_space=pltpu.SMEM),
            pl.BlockSpec((1, 1, 2 * tm), lambda i: (jnp.minimum(i + 1, n - 1), 0, 0),
                         memory_space=pltpu.SMEM),
            pl.BlockSpec(memory_space=pl.ANY),
            pl.BlockSpec((tm, d), lambda i: (i, 0)),
            pl.BlockSpec((tm, gates.shape[1]), lambda i: (i, 0)),
        ],
        out_specs=pl.BlockSpec((tm, d), lambda i: (i, 0)),
        out_shape=jax.ShapeDtypeStruct((t, d), F32),
        scratch_shapes=[pltpu.VMEM((2, 2 * tm * ROW_TILE, LANES), F32),
                        pltpu.SemaphoreType.DMA((2,))],
        compiler_params=_cparams(("arbitrary",)),
        name="moe_combine",
    )(dest3, dest3, ys, x2d, gates)


def _pad_heads_t(w_cols, offsets):
    d = w_cols.shape[0]
    nh = w_cols.shape[1] // HEAD_DIM
    wt = w_cols.T.reshape(nh, HEAD_DIM, d)
    hi = jnp.asarray(offsets, I32).reshape(nh, 1, 1) > 0
    z = jnp.zeros_like(wt)
    slab = jnp.concatenate([jnp.where(hi, z, wt), jnp.where(hi, wt, z)], axis=1)
    return slab.reshape(nh * HEAD_SLAB, d)


def _pad_head_cols(w):
    rows = w.shape[0]
    nh = w.shape[1] // HEAD_DIM
    w3 = w.reshape(rows, nh, HEAD_DIM)
    return jnp.concatenate([w3, jnp.zeros_like(w3)], axis=2).reshape(rows, nh * HEAD_SLAB)


def _head_block_ones(width):
    idx = jnp.arange(width) // HEAD_DIM
    return (idx[:, None] == idx[None, :]).astype(F32)


def _tile_gain(g, n_heads):
    return jnp.tile(g.astype(F32), n_heads).reshape(1, n_heads * HEAD_DIM)


def _s5_params(lam_re, lam_im, log_dt, b_re, b_im, c_re, c_im):
    f32 = F32
    g, p = lam_re.shape
    n = b_re.shape[2]
    lam = lax.complex(lam_re.astype(f32), lam_im.astype(f32))
    dt = jnp.exp(log_dt.astype(f32))[:, None]
    lam_bar = jnp.exp(lam * dt)
    b_bar = ((lam_bar - 1.0) / lam)[..., None] * lax.complex(b_re.astype(f32), b_im.astype(f32))
    ns = g // S5_GROUPS_PER_SLAB
    eye = jnp.eye(S5_GROUPS_PER_SLAB, dtype=f32)

    def in_map(part):
        blk = part.reshape(ns, S5_GROUPS_PER_SLAB, p, n).transpose(0, 1, 3, 2)
        return jnp.einsum('sgnp,gh->sgnhp', blk, eye).reshape(
            ns, S5_GROUPS_PER_SLAB * n, S5_GROUPS_PER_SLAB * p)

    def out_map(part):
        blk = part.reshape(ns, S5_GROUPS_PER_SLAB, n, p).transpose(0, 1, 3, 2)
        return jnp.einsum('sgpn,gh->sgphn', blk, eye).reshape(
            ns, S5_GROUPS_PER_SLAB * p, S5_GROUPS_PER_SLAB * n)

    bd = jnp.concatenate([in_map(jnp.real(b_bar)), in_map(jnp.imag(b_bar))], axis=2)
    cd = jnp.concatenate([out_map(c_re.astype(f32)), out_map(-c_im.astype(f32))], axis=1)
    ar = jnp.real(lam_bar).reshape(ns, S5_GROUPS_PER_SLAB * p)
    ai = jnp.imag(lam_bar).reshape(ns, S5_GROUPS_PER_SLAB * p)
    return bd.astype(BF16), cd.astype(BF16), ar, ai


def kernel(x, mem, l0_mix_norm, l0_w_in, l0_s5_lam_re, l0_s5_lam_im, l0_s5_log_dt, l0_s5_b_re, l0_s5_b_im, l0_s5_c_re, l0_s5_c_im, l0_s5_d, l0_s5_w_glu, l0_s5_b_glu, l0_mem_norm, l0_w_mem_k, l0_w_mem_v, l0_mem_q_gain, l0_mem_k_gain, l0_w_out, l0_ffn_norm, l0_ffn_w_gate, l0_ffn_w_up, l0_ffn_w_down, kv_norm, kv_w_k, kv_w_v, kv_k_gain, l1_mix_norm, l1_w_in, l1_moba_q_gain, l1_mem_norm, l1_w_mem_k, l1_w_mem_v, l1_mem_q_gain, l1_mem_k_gain, l1_w_out, l1_ffn_norm, l1_moe_router, l1_moe_w_gate, l1_moe_w_up, l1_moe_w_down):
    batch, seq, d = x.shape
    t = batch * seq
    main_w = l0_s5_w_glu.shape[0]
    mem_w = l0_w_mem_k.shape[1]
    tm = TOKEN_TILE
    row = lambda v: v.astype(F32).reshape(1, -1)

    mem_off = [HEAD_DIM * (h % 2) for h in range(MEM_HEADS)]
    ones_kv = _head_block_ones(mem_w)
    x2d = x.reshape(t, d)

    main_in, qm0_t = _inproj0(
        x2d, row(l0_mix_norm), l0_w_in[:, :main_w].astype(BF16),
        _pad_heads_t(l0_w_in[:, main_w:], mem_off).astype(BF16), batch, seq, tm)
    k0, v0_t = _mem_kv(
        mem, row(l0_mem_norm), l0_w_mem_k.astype(BF16), l0_w_mem_v.T.astype(BF16), ones_kv,
        _tile_gain(l0_mem_k_gain, MEM_HEADS) * _tile_gain(l0_mem_q_gain, MEM_HEADS))

    bd, cd, ar, ai = _s5_params(l0_s5_lam_re, l0_s5_lam_im, l0_s5_log_dt, l0_s5_b_re, l0_s5_b_im,
                                l0_s5_c_re, l0_s5_c_im)
    s5_out = _s5(main_in.reshape(batch, seq, main_w), bd, cd, ar, ai, row(l0_s5_d),
                 l0_s5_w_glu.astype(BF16), row(l0_s5_b_glu), S5_CHUNK).reshape(t, main_w)

    wq1_t = jnp.concatenate([l1_w_in[:, :main_w].T,
                             _pad_heads_t(l1_w_in[:, main_w:], mem_off)], axis=0).astype(BF16)
    x2, q_t, qm1_t, kq, km, v_t = _layer0_tail(
        s5_out, (qm0_t, k0, v0_t), l0_w_out[:main_w].astype(BF16),
        l0_w_out[main_w:].astype(BF16), x2d, row(l0_ffn_norm),
        l0_ffn_w_gate.astype(BF16), l0_ffn_w_up.astype(BF16), l0_ffn_w_down.astype(BF16),
        row(l1_mix_norm), row(kv_norm), wq1_t, _pad_head_cols(kv_w_k).astype(BF16),
        kv_w_v.T.astype(BF16), _pad_head_cols(_tile_gain(kv_k_gain, MOBA_KV_HEADS)),
        _pad_head_cols(_tile_gain(l1_moba_q_gain, MOBA_KV_HEADS)), batch, seq, tm)
    nb = seq // MOBA_BLOCK
    moba_t = _moba(q_t, kq.reshape(batch, seq, -1), v_t, km.reshape(batch, nb, -1))
    k1, v1_t = _mem_kv(
        mem, row(l1_mem_norm), l1_w_mem_k.astype(BF16), l1_w_mem_v.T.astype(BF16), ones_kv,
        _tile_gain(l1_mem_k_gain, MEM_HEADS) * _tile_gain(l1_mem_q_gain, MEM_HEADS))

    wr_t = jnp.zeros((ROUTER_ROWS, d), F32).at[:N_EXPERTS].set(l1_moe_router.astype(F32).T)
    wr_hi = wr_t.astype(BF16)
    wr_lo = (wr_t - wr_hi.astype(F32)).astype(BF16)
    x3, h3, rec, cnt = _outproj1(moba_t, (qm1_t, k1, v1_t), l1_w_out[:main_w].astype(BF16),
                                 l1_w_out[main_w:].astype(BF16), x2, row(l1_ffn_norm),
                                 wr_hi, wr_lo, seq, tm)

    assert d == ROW_TILE * LANES
    tme = MOE_TILE
    counts = cnt[:N_EXPERTS, 0].astype(I32)
    padded = ((counts + tme - 1) // tme) * tme
    ends = jnp.cumsum(padded)
    starts = ends - padded
    max_tiles = (2 * t) // tme + N_EXPERTS
    def lookup(table, idx):
        hit = idx[None, :] == jnp.arange(N_EXPERTS, dtype=I32)[:, None]
        return jnp.sum(jnp.where(hit, table[:, None], 0), axis=0)

    tile_ids = jnp.arange(max_tiles, dtype=I32)
    tile_expert = jnp.minimum(
        jnp.sum((ends // tme)[None, :] <= tile_ids[:, None], axis=1), N_EXPERTS - 1).astype(I32)
    n_valid = jnp.clip(lookup(counts, tile_expert) + lookup(starts, tile_expert)
                       - tile_ids * tme, 0, tme)
    n_used = (ends[-1] // tme).astype(I32).reshape(1)
    d1 = lookup(starts, rec[0].astype(I32)) + rec[2].astype(I32)
    d2 = lookup(starts, rec[1].astype(I32)) + rec[3].astype(I32)
    gates = rec[4:6].T
    def tile_dest(tile):
        return jnp.concatenate([d1.reshape(t // tile, 1, tile), d2.reshape(t // tile, 1, tile)],
                               axis=2)

    xs = _dispatch(starts + counts, padded - counts, n_used, tile_dest(DISPATCH_TILE), h3,
                   max_tiles, tme, DISPATCH_TILE)
    ys = _moe_ffn(tile_expert, n_valid.astype(I32), n_used, xs, l1_moe_w_gate, l1_moe_w_up,
                  l1_moe_w_down, tme, MOE_FF_TILE)
    out = _combine(tile_dest(COMBINE_TILE), ys, x3, gates, COMBINE_TILE)
    return out.reshape(batch, seq, d)
```
